```python
import math
import jax, jax.numpy as jnp
from jax import lax
import numpy as np

D_MODEL = 4096
BATCH = 4
SEQ = 2048
DEPTH = 1
DEC_BATCH = 8
DEC_SEQ = 16
PAST_LEN = 2048

CHUNK = 64
A_CHUNK = 128
A_GROUPS = 16
A_GROUP_DIM = 128
A_WIDTH = A_GROUPS * A_GROUP_DIM
B_GROUP_DIM = 16
B_GROUPS = 128
B_WIDTH = B_GROUPS * B_GROUP_DIM
B_STATE = 64
DT_MIN = 1e-3
DT_MAX = 1e-1
IN_WIDTH = 2 * A_WIDTH + B_WIDTH + 2 * D_MODEL
PEER_HEADS = 8
PEER_QDIM = 256
PEER_SUBDIM = PEER_QDIM // 2
PEER_NKEYS = 128
PEER_EXPERTS = PEER_NKEYS * PEER_NKEYS
PEER_TOPK = 16
PEER_BLOCK = 64
EPS = 1e-6

kernel_name = "hybrid_gmlp_s5_peer_stream_step"


def rmsnorm(x, g):
    xf = x.astype(jnp.float32)
    r = lax.rsqrt(jnp.mean(xf * xf, axis=-1, keepdims=True) + EPS)
    return (xf * r).astype(x.dtype) * g


def layernorm(x, g, b):
    xf = x.astype(jnp.float32)
    mu = jnp.mean(xf, axis=-1, keepdims=True)
    var = jnp.mean(jnp.square(xf - mu), axis=-1, keepdims=True)
    return ((xf - mu) * lax.rsqrt(var + EPS)).astype(x.dtype) * g + b


def chunk_mlp_mixer(z, ws, bs, ln_g, ln_b):
    bsz, t, _ = z.shape
    u, v = jnp.split(z, 2, axis=-1)
    v = layernorm(v, ln_g, ln_b)
    n_chunks = -(-t // A_CHUNK)
    pad = n_chunks * A_CHUNK - t
    vp = jnp.pad(v, ((0, 0), (0, pad), (0, 0))).reshape(bsz, n_chunks, A_CHUNK, A_GROUPS, A_GROUP_DIM)
    blk = jnp.arange(A_CHUNK) // CHUNK
    mask = blk[:, None] >= blk[None, :]
    w = jnp.where(mask[None], ws, jnp.zeros((), ws.dtype))
    mixed = jnp.einsum('gij,bnjgc->bnigc', w, vp) + bs.T[None, None, :, :, None]
    mixed = mixed.reshape(bsz, n_chunks * A_CHUNK, A_WIDTH)[:, :t]
    return u * mixed, v


def s5_combine(e1, e2):
    a1r, a1i, b1r, b1i = e1
    a2r, a2i, b2r, b2i = e2
    return (a2r * a1r - a2i * a1i,
            a2r * a1i + a2i * a1r,
            a2r * b1r - a2i * b1i + b2r,
            a2r * b1i + a2i * b1r + b2i)


def s5_mixer(xb, h0_re, h0_im, lam_re, lam_im, log_dt, b_re, b_im, c_re, c_im, d_skip, w_glu, b_glu):
    bsz, t, _ = xb.shape
    f32 = jnp.float32
    u = xb.reshape(bsz, t, B_GROUPS, B_GROUP_DIM)
    dt = jnp.exp(log_dt.astype(f32))[:, None]
    lr, li = lam_re.astype(f32), lam_im.astype(f32)
    mag = jnp.exp(lr * dt)
    ab_re, ab_im = mag * jnp.cos(li * dt), mag * jnp.sin(li * dt)
    den = lr * lr + li * li
    num_re = ab_re - 1.0
    coef_re = (num_re * lr + ab_im * li) / den
    coef_im = (ab_im * lr - num_re * li) / den
    br, bi = b_re.astype(f32), b_im.astype(f32)
    bb_re = coef_re[..., None] * br - coef_im[..., None] * bi
    bb_im = coef_re[..., None] * bi + coef_im[..., None] * br
    uf = u.astype(f32)
    bu_re = jnp.einsum('gpc,btgc->btgp', bb_re, uf)
    bu_im = jnp.einsum('gpc,btgc->btgp', bb_im, uf)
    h0r, h0i = h0_re.astype(f32), h0_im.astype(f32)
    bu_re = bu_re.at[:, 0].add(ab_re * h0r - ab_im * h0i)
    bu_im = bu_im.at[:, 0].add(ab_re * h0i + ab_im * h0r)
    a_re = jnp.broadcast_to(ab_re, bu_re.shape)
    a_im = jnp.broadcast_to(ab_im, bu_re.shape)
    _, _, h_re, h_im = lax.associative_scan(s5_combine, (a_re, a_im, bu_re, bu_im), axis=1)
    y = (jnp.einsum('gcp,btgp->btgc', c_re.astype(f32), h_re)
         - jnp.einsum('gcp,btgp->btgc', c_im.astype(f32), h_im)
         + d_skip.astype(f32) * uf)
    y = jax.nn.gelu(y.astype(xb.dtype)).reshape(bsz, t, B_WIDTH)
    out = y * jax.nn.sigmoid(y @ w_glu + b_glu)
    return out, h_re[:, -1].astype(xb.dtype), h_im[:, -1].astype(xb.dtype)


def peer_ffn(h, w_query, sub_keys, expert_u, expert_v):
    bsz, t, d = h.shape
    n = bsz * t
    nb = -(-n // PEER_BLOCK)
    tok = jnp.pad(h.reshape(n, d), ((0, nb * PEER_BLOCK - n), (0, 0))).reshape(nb, PEER_BLOCK, d)

    def block(xt):
        q = (xt @ w_query).reshape(PEER_BLOCK, PEER_HEADS, 2, PEER_SUBDIM)
        s = jnp.einsum('thsk,hsnk->thsn', q, sub_keys).astype(jnp.float32)
        sv, si = lax.top_k(s, PEER_TOPK)
        cand = sv[:, :, 0, :, None] + sv[:, :, 1, None, :]
        cv, ci = lax.top_k(cand.reshape(PEER_BLOCK, PEER_HEADS, PEER_TOPK * PEER_TOPK), PEER_TOPK)
        i1 = jnp.take_along_axis(si[:, :, 0], ci // PEER_TOPK, axis=-1)
        i2 = jnp.take_along_axis(si[:, :, 1], ci % PEER_TOPK, axis=-1)
        e = i1 * PEER_NKEYS + i2
        g = jax.nn.softmax(cv, axis=-1)
        act = jax.nn.gelu(jnp.einsum('td,thkd->thk', xt, expert_u[e]))
        wgt = (g * act.astype(jnp.float32)).astype(xt.dtype)
        return jnp.einsum('thk,thkd->td', wgt, expert_v[e])

    out = lax.map(block, tok)
    return out.reshape(nb * PEER_BLOCK, d)[:n].reshape(bsz, t, d)


def encoder_layer(x, c, h0_re, h0_im, w_ada, b_ada, g_pre_tok, g_post_tok, g_pre_ch, g_post_ch,
                  w_in, a_ws, a_bs, a_ln_g, a_ln_b, s5_lam_re, s5_lam_im, s5_log_dt, s5_b_re, s5_b_im,
                  s5_c_re, s5_c_im, s5_d, w_glu, b_glu, w_branch, w_out,
                  peer_w_query, peer_sub_keys, peer_u, peer_v):
    mod = (jax.nn.silu(c) @ w_ada + b_ada)[:, None, :]
    sh_t, sc_t, gt_t, sh_c, sc_c, gt_c = jnp.split(mod, 6, axis=-1)
    h = rmsnorm(x, g_pre_tok) * (1.0 + sc_t) + sh_t
    proj = h @ w_in
    za, xb, gates = jnp.split(proj, [2 * A_WIDTH, 2 * A_WIDTH + B_WIDTH], axis=-1)
    ga, gb = jnp.split(gates, 2, axis=-1)
    ya, v_rows = chunk_mlp_mixer(jax.nn.gelu(za), a_ws, a_bs, a_ln_g, a_ln_b)
    yb, hr, hi = s5_mixer(xb, h0_re, h0_im, s5_lam_re, s5_lam_im, s5_log_dt, s5_b_re, s5_b_im,
                          s5_c_re, s5_c_im, s5_d, w_glu, b_glu)
    merged = (jax.nn.sigmoid(ga) * (ya @ w_branch[:A_WIDTH])
              + jax.nn.sigmoid(gb) * (yb @ w_branch[A_WIDTH:]))
    x = x + gt_t * rmsnorm(merged @ w_out, g_post_tok)
    h = rmsnorm(x, g_pre_ch) * (1.0 + sc_c) + sh_c
    f = peer_ffn(h, peer_w_query, peer_sub_keys, peer_u, peer_v)
    x = x + gt_c * rmsnorm(f, g_post_ch)
    return x, hr, hi, v_rows


def setup_inputs(seed: int = 0) -> dict:
    key = jax.random.key(seed)
    ks = jax.random.split(key, 32)
    f32 = jnp.float32
    nrm = lambda k, s, sc: jax.random.normal(k, s, f32) * sc
    lam_im = jnp.pi * jnp.arange(B_STATE, dtype=f32)
    return {
        "x_prompt": nrm(ks[0], (BATCH, SEQ, D_MODEL), 1.0),
        "x_sample": nrm(ks[1], (DEC_BATCH, DEC_SEQ, D_MODEL), 1.0),
        "c_prompt": nrm(ks[2], (BATCH, D_MODEL), 1.0),
        "c_sample": nrm(ks[3], (DEC_BATCH, D_MODEL), 1.0),
        "state_ssm_re": nrm(ks[4], (DEPTH, DEC_BATCH, B_GROUPS, B_STATE), 0.1),
        "state_ssm_im": nrm(ks[5], (DEPTH, DEC_BATCH, B_GROUPS, B_STATE), 0.1),
        "w_ada": nrm(ks[6], (DEPTH, D_MODEL, 6 * D_MODEL), 0.5 * D_MODEL ** -0.5),
        "b_ada": nrm(ks[7], (DEPTH, 6 * D_MODEL), 0.01),
        "g_pre_tok": 1.0 + nrm(ks[8], (DEPTH, D_MODEL), 0.01),
        "g_post_tok": 1.0 + nrm(ks[9], (DEPTH, D_MODEL), 0.01),
        "g_pre_ch": 1.0 + nrm(ks[10], (DEPTH, D_MODEL), 0.01),
        "g_post_ch": 1.0 + nrm(ks[11], (DEPTH, D_MODEL), 0.01),
        "w_in": nrm(ks[12], (DEPTH, D_MODEL, IN_WIDTH), D_MODEL ** -0.5),
        "a_ws": nrm(ks[13], (DEPTH, A_GROUPS, A_CHUNK, A_CHUNK), A_CHUNK ** -0.5),
        "a_bs": 1.0 + nrm(ks[14], (DEPTH, A_GROUPS, A_CHUNK), 0.01),
        "a_ln_g": 1.0 + nrm(ks[15], (DEPTH, A_WIDTH), 0.01),
        "a_ln_b": nrm(ks[16], (DEPTH, A_WIDTH), 0.01),
        "s5_lam_re": -0.5 + nrm(ks[17], (DEPTH, B_GROUPS, B_STATE), 0.01),
        "s5_lam_im": lam_im + nrm(ks[18], (DEPTH, B_GROUPS, B_STATE), 0.01),
        "s5_log_dt": jax.random.uniform(ks[19], (DEPTH, B_GROUPS), f32, math.log(DT_MIN), math.log(DT_MAX)),
        "s5_b_re": nrm(ks[20], (DEPTH, B_GROUPS, B_STATE, B_GROUP_DIM), (2 * B_GROUP_DIM) ** -0.5),
        "s5_b_im": nrm(ks[21], (DEPTH, B_GROUPS, B_STATE, B_GROUP_DIM), (2 * B_GROUP_DIM) ** -0.5),
        "s5_c_re": nrm(ks[22], (DEPTH, B_GROUPS, B_GROUP_DIM, B_STATE), B_STATE ** -0.5),
        "s5_c_im": nrm(ks[23], (DEPTH, B_GROUPS, B_GROUP_DIM, B_STATE), B_STATE ** -0.5),
        "s5_d": nrm(ks[24], (DEPTH, B_GROUPS, B_GROUP_DIM), 1.0),
        "w_glu": nrm(ks[25], (DEPTH, B_WIDTH, B_WIDTH), B_WIDTH ** -0.5),
        "b_glu": nrm(ks[26], (DEPTH, B_WIDTH), 0.01),
        "w_branch": nrm(ks[27], (DEPTH, A_WIDTH + B_WIDTH, D_MODEL), A_WIDTH ** -0.5),
        "w_out": nrm(ks[28], (DEPTH, D_MODEL, D_MODEL), D_MODEL ** -0.5),
        "peer_w_query": nrm(ks[29], (DEPTH, D_MODEL, PEER_HEADS * PEER_QDIM), D_MODEL ** -0.5),
        "peer_sub_keys": nrm(ks[30], (DEPTH, PEER_HEADS, 2, PEER_NKEYS, PEER_SUBDIM), PEER_SUBDIM ** -0.5),
        "peer_u": nrm(ks[31], (DEPTH, PEER_EXPERTS, D_MODEL), D_MODEL ** -0.5),
        "peer_v": nrm(jax.random.fold_in(ks[31], 1), (DEPTH, PEER_EXPERTS, D_MODEL), 1.0),
    }


def reference(x_prompt, x_sample, c_prompt, c_sample, state_ssm_re, state_ssm_im,
              w_ada, b_ada, g_pre_tok, g_post_tok, g_pre_ch, g_post_ch,
              w_in, a_ws, a_bs, a_ln_g, a_ln_b, s5_lam_re, s5_lam_im, s5_log_dt, s5_b_re, s5_b_im,
              s5_c_re, s5_c_im, s5_d, w_glu, b_glu, w_branch, w_out,
              peer_w_query, peer_sub_keys, peer_u, peer_v):
    y_prompt, y_sample = x_prompt, x_sample
    re_p, im_p, re_s, im_s, v_s = [], [], [], [], []
    for l in range(DEPTH):
        lp = (w_ada[l], b_ada[l], g_pre_tok[l], g_post_tok[l], g_pre_ch[l], g_post_ch[l],
              w_in[l], a_ws[l], a_bs[l], a_ln_g[l], a_ln_b[l], s5_lam_re[l], s5_lam_im[l], s5_log_dt[l],
              s5_b_re[l], s5_b_im[l], s5_c_re[l], s5_c_im[l], s5_d[l], w_glu[l], b_glu[l],
              w_branch[l], w_out[l], peer_w_query[l], peer_sub_keys[l], peer_u[l], peer_v[l])
        h0 = jnp.zeros((BATCH, B_GROUPS, B_STATE), x_prompt.dtype)
        y_prompt, hr, hi, _ = encoder_layer(y_prompt, c_prompt, h0, h0, *lp)
        re_p.append(hr)
        im_p.append(hi)
        y_sample, hr, hi, v_rows = encoder_layer(y_sample, c_sample, state_ssm_re[l], state_ssm_im[l], *lp)
        re_s.append(hr)
        im_s.append(hi)
        v_s.append(v_rows)
    ssm_re_prompt = jnp.stack(re_p, axis=0)
    ssm_im_prompt = jnp.stack(im_p, axis=0)
    ssm_re_sample = jnp.stack(re_s, axis=0)
    ssm_im_sample = jnp.stack(im_s, axis=0)
    a_v_sample = jnp.stack(v_s, axis=0)
    return (y_prompt, y_sample, ssm_re_prompt, ssm_im_prompt, ssm_re_sample, ssm_im_sample, a_v_sample)
```

```python
import functools

import jax
import jax.numpy as jnp
from jax import lax
from jax.experimental import pallas as pl
from jax.experimental.pallas import tpu as pltpu

F32 = jnp.float32
BF16 = jnp.bfloat16

D_MODEL = 4096
CHUNK = 64
A_CHUNK = 128
A_GROUPS = 16
A_GROUP_DIM = 128
A_WIDTH = A_GROUPS * A_GROUP_DIM
B_GROUP_DIM = 16
B_GROUPS = 128
B_WIDTH = B_GROUPS * B_GROUP_DIM
B_STATE = 64
S5_STATES = B_GROUPS * B_STATE
S5_GROUPS_PER_BLOCK = 8
S5_BLOCKS = B_GROUPS // S5_GROUPS_PER_BLOCK
S5_BLOCK_STATES = S5_GROUPS_PER_BLOCK * B_STATE
IN_WIDTH = 2 * A_WIDTH + B_WIDTH + 2 * D_MODEL
PEER_HEADS = 8
PEER_SUBDIM = 128
PEER_NKEYS = 128
PEER_EXPERTS = PEER_NKEYS * PEER_NKEYS
PEER_TOPK = 16
EPS = 1e-6

V7X_LANES = 128
V7X_SUBLANES = 8
V7X_VMEM_BYTES = 64 * 1024 * 1024
VMEM_LIMIT_BYTES = 56 * 1024 * 1024


def _params(*semantics):
    return pltpu.CompilerParams(dimension_semantics=semantics, vmem_limit_bytes=VMEM_LIMIT_BYTES)


def _rms(x):
    return x * lax.rsqrt(jnp.mean(x * x, axis=-1, keepdims=True) + EPS)


def _ada_kernel(c_ref, w_ref, b_ref, o_ref):
    s = jax.nn.silu(c_ref[...]).astype(BF16)
    o_ref[...] = jnp.dot(s, w_ref[...].astype(BF16), preferred_element_type=F32) + b_ref[...]


def _ada(c, w, b):
    rows, d = c.shape
    n = w.shape[1]
    tn = 512
    return pl.pallas_call(
        _ada_kernel,
        grid=(n // tn,),
        in_specs=[pl.BlockSpec((rows, d), lambda j: (0, 0)),
                  pl.BlockSpec((d, tn), lambda j: (0, j)),
                  pl.BlockSpec((1, tn), lambda j: (0, j))],
        out_specs=pl.BlockSpec((rows, tn), lambda j: (0, j)),
        out_shape=jax.ShapeDtypeStruct((rows, n), F32),
        compiler_params=_params("parallel"),
        name="ada_mod",
    )(c, w, b.reshape(1, n))


def _mod_spec(mod, tiles_per_group):
    return pl.BlockSpec((None,) + mod.shape[1:], lambda i: (i // tiles_per_group, 0, 0))


def _prenorm_kernel(x_ref, g_ref, sc_ref, sh_ref, o_ref):
    h = _rms(x_ref[...]) * g_ref[...] * (1.0 + sc_ref[...]) + sh_ref[...]
    o_ref[...] = h.astype(BF16)


def _prenorm(x, g, sc, sh, tr, tiles_per_group):
    n, d = x.shape
    return pl.pallas_call(
        _prenorm_kernel,
        grid=(n // tr,),
        in_specs=[pl.BlockSpec((tr, d), lambda i: (i, 0)),
                  pl.BlockSpec((1, d), lambda i: (0, 0)),
                  _mod_spec(sc, tiles_per_group),
                  _mod_spec(sh, tiles_per_group)],
        out_specs=pl.BlockSpec((tr, d), lambda i: (i, 0)),
        out_shape=jax.ShapeDtypeStruct((n, d), BF16),
        compiler_params=_params("parallel"),
        name="prenorm_tok",
    )(x, g.reshape(1, d), sc, sh)


def _resid_kernel(x_ref, m_ref, gpost_ref, gt_ref, gpre_ref, sc_ref, sh_ref, x1_ref, h_ref, ht_ref):
    x1 = x_ref[...] + gt_ref[...] * (_rms(m_ref[...]) * gpost_ref[...])
    x1_ref[...] = x1
    h = _rms(x1) * gpre_ref[...] * (1.0 + sc_ref[...]) + sh_ref[...]
    h_ref[...] = h.astype(BF16)
    ht_ref[...] = h.T.astype(BF16)


def _resid_prenorm(x, m, gpost, gt, gpre, sc, sh, tr, tiles_per_group):
    n, d = x.shape
    row = pl.BlockSpec((tr, d), lambda i: (i, 0))
    vec = pl.BlockSpec((1, d), lambda i: (0, 0))
    return pl.pallas_call(
        _resid_kernel,
        grid=(n // tr,),
        in_specs=[row, row, vec, _mod_spec(gt, tiles_per_group), vec,
                  _mod_spec(sc, tiles_per_group), _mod_spec(sh, tiles_per_group)],
        out_specs=[row, row, pl.BlockSpec((d, tr), lambda i: (0, i))],
        out_shape=[jax.ShapeDtypeStruct((n, d), F32),
                   jax.ShapeDtypeStruct((n, d), BF16),
                   jax.ShapeDtypeStruct((d, n), BF16)],
        compiler_params=_params("parallel"),
        name="resid_prenorm_ch",
    )(x, m, gpost.reshape(1, d), gt, gpre.reshape(1, d), sc, sh)


def _final_kernel(x_ref, ft_ref, gpost_ref, gt_ref, o_ref):
    f = ft_ref[...].T
    o_ref[...] = x_ref[...] + gt_ref[...] * (_rms(f) * gpost_ref[...])


def _final_resid(x, ft, gpost, gt, tr, tiles_per_group):
    n, d = x.shape
    row = pl.BlockSpec((tr, d), lambda i: (i, 0))
    return pl.pallas_call(
        _final_kernel,
        grid=(n // tr,),
        in_specs=[row, pl.BlockSpec((d, tr), lambda i: (0, i)),
                  pl.BlockSpec((1, d), lambda i: (0, 0)), _mod_spec(gt, tiles_per_group)],
        out_specs=row,
        out_shape=jax.ShapeDtypeStruct((n, d), F32),
        compiler_params=_params("parallel"),
        name="final_resid",
    )(x, ft, gpost.reshape(1, d), gt)


def _mm_kernel(a_ref, w_ref, o_ref):
    o_ref[...] = jnp.dot(a_ref[...].astype(BF16), w_ref[...].astype(BF16),
                         preferred_element_type=F32).astype(o_ref.dtype)


def _matmul(a, w, tm, tn, out_dtype=F32, name="matmul"):
    m, k = a.shape
    n = w.shape[1]
    tm = min(tm, m)
    return pl.pallas_call(
        _mm_kernel,
        grid=(m // tm, n // tn),
        in_specs=[pl.BlockSpec((tm, k), lambda i, j: (i, 0)),
                  pl.BlockSpec((k, tn), lambda i, j: (0, j))],
        out_specs=pl.BlockSpec((tm, tn), lambda i, j: (i, j)),
        out_shape=jax.ShapeDtypeStruct((m, n), out_dtype),
        compiler_params=_params("parallel", "parallel"),
        name=name,
    )(a, w)


def _glu_kernel(y_ref, w_ref, b_ref, yj_ref, o_ref):
    acc = jnp.dot(y_ref[...].astype(BF16), w_ref[...].astype(BF16), preferred_element_type=F32)
    o_ref[...] = (yj_ref[...] * jax.nn.sigmoid(acc + b_ref[...])).astype(BF16)


def _glu(y, w, b, tm, tn):
    m, k = y.shape
    n = w.shape[1]
    tm = min(tm, m)
    return pl.pallas_call(
        _glu_kernel,
        grid=(m // tm, n // tn),
        in_specs=[pl.BlockSpec((tm, k), lambda i, j: (i, 0)),
                  pl.BlockSpec((k, tn), lambda i, j: (0, j)),
                  pl.BlockSpec((1, tn), lambda i, j: (0, j)),
                  pl.BlockSpec((tm, tn), lambda i, j: (i, j))],
        out_specs=pl.BlockSpec((tm, tn), lambda i, j: (i, j)),
        out_shape=jax.ShapeDtypeStruct((m, n), BF16),
        compiler_params=_params("parallel", "parallel"),
        name="s5_glu",
    )(y, w, b.reshape(1, n), y)


def _branch_kernel(ya_ref, yb_ref, wa_ref, wb_ref, ga_ref, gb_ref, o_ref):
    a = jnp.dot(ya_ref[...], wa_ref[...].astype(BF16), preferred_element_type=F32)
    b = jnp.dot(yb_ref[...], wb_ref[...].astype(BF16), preferred_element_type=F32)
    o_ref[...] = (jax.nn.sigmoid(ga_ref[...]) * a + jax.nn.sigmoid(gb_ref[...]) * b).astype(BF16)


def _branch_merge(ya, yb, w_branch, proj, tm, tn):
    m = ya.shape[0]
    n = w_branch.shape[1]
    tm = min(tm, m)
    ga_blk = (2 * A_WIDTH + B_WIDTH) // tn
    gb_blk = ga_blk + D_MODEL // tn
    return pl.pallas_call(
        _branch_kernel,
        grid=(n // tn, m // tm),
        in_specs=[pl.BlockSpec((tm, A_WIDTH), lambda j, i: (i, 0)),
                  pl.BlockSpec((tm, B_WIDTH), lambda j, i: (i, 0)),
                  pl.BlockSpec((A_WIDTH, tn), lambda j, i: (0, j)),
                  pl.BlockSpec((B_WIDTH, tn), lambda j, i: (A_WIDTH // B_WIDTH, j)),
                  pl.BlockSpec((tm, tn), lambda j, i: (i, ga_blk + j)),
                  pl.BlockSpec((tm, tn), lambda j, i: (i, gb_blk + j))],
        out_specs=pl.BlockSpec((tm, tn), lambda j, i: (i, j)),
        out_shape=jax.ShapeDtypeStruct((m, n), BF16),
        compiler_params=_params("parallel", "parallel"),
        name="branch_merge",
    )(ya, yb, w_branch, w_branch, proj, proj)


def _mixer_a_kernel(za_ref, ws_ref, bias_ref, lng_ref, lnb_ref, ya_ref, *v_refs, rows):
    z = jax.nn.gelu(za_ref[...])
    u = z[:, :A_WIDTH]
    v = z[:, A_WIDTH:]
    d = v - jnp.mean(v, axis=-1, keepdims=True)
    vn = d * lax.rsqrt(jnp.mean(d * d, axis=-1, keepdims=True) + EPS) * lng_ref[...] + lnb_ref[...]
    if v_refs:
        v_refs[0][...] = vn
    if rows < A_CHUNK:
        vn = jnp.concatenate([vn, jnp.zeros((A_CHUNK - rows, A_WIDTH), F32)], axis=0)
    vb = vn.astype(BF16)
    blk_i = lax.broadcasted_iota(jnp.int32, (A_CHUNK, A_CHUNK), 0) // CHUNK
    blk_j = lax.broadcasted_iota(jnp.int32, (A_CHUNK, A_CHUNK), 1) // CHUNK
    causal = blk_i >= blk_j
    for g in range(A_GROUPS):
        cols = slice(g * A_GROUP_DIM, (g + 1) * A_GROUP_DIM)
        w = jnp.where(causal, ws_ref[g], 0.0).astype(BF16)
        mixed = jnp.dot(w, vb[:, cols], preferred_element_type=F32) + bias_ref[:, cols]
        ya_ref[:, cols] = (u[:, cols] * mixed[:rows]).astype(BF16)


def _mixer_a(proj3, ws, bs, ln_g, ln_b, rows, emit_v):
    nb, t, _ = proj3.shape
    bias = jnp.repeat(bs.T, A_GROUP_DIM, axis=1)
    out_shape = [jax.ShapeDtypeStruct((nb, t, A_WIDTH), BF16)]
    out_specs = [pl.BlockSpec((None, rows, A_WIDTH), lambda b, c: (b, c, 0))]
    if emit_v:
        out_shape.append(jax.ShapeDtypeStruct((nb, t, A_WIDTH), F32))
        out_specs.append(pl.BlockSpec((None, rows, A_WIDTH), lambda b, c: (b, c, 0)))
    return pl.pallas_call(
        functools.partial(_mixer_a_kernel, rows=rows),
        grid=(nb, t // rows),
        in_specs=[pl.BlockSpec((None, rows, 2 * A_WIDTH), lambda b, c: (b, c, 0)),
                  pl.BlockSpec((A_GROUPS, A_CHUNK, A_CHUNK), lambda b, c: (0, 0, 0)),
                  pl.BlockSpec((A_CHUNK, A_WIDTH), lambda b, c: (0, 0)),
                  pl.BlockSpec((1, A_WIDTH), lambda b, c: (0, 0)),
                  pl.BlockSpec((1, A_WIDTH), lambda b, c: (0, 0))],
        out_specs=out_specs,
        out_shape=out_shape,
        compiler_params=_params("parallel", "parallel"),
        name="mixer_a",
    )(proj3, ws, bias, ln_g.reshape(1, A_WIDTH), ln_b.reshape(1, A_WIDTH))


def _s5_discretize(lam_re, lam_im, log_dt, b_re, b_im, c_re, c_im):
    dt = jnp.exp(log_dt.astype(F32))[:, None]
    lr, li = lam_re.astype(F32), lam_im.astype(F32)
    mag = jnp.exp(lr * dt)
    ab_re, ab_im = mag * jnp.cos(li * dt), mag * jnp.sin(li * dt)
    den = lr * lr + li * li
    num_re = ab_re - 1.0
    coef_re = (num_re * lr + ab_im * li) / den
    coef_im = (ab_im * lr - num_re * li) / den
    br, bi = b_re.astype(F32), b_im.astype(F32)
    bb_re = coef_re[..., None] * br - coef_im[..., None] * bi
    bb_im = coef_re[..., None] * bi + coef_im[..., None] * br
    eye = jnp.eye(S5_GROUPS_PER_BLOCK, dtype=F32)

    def pack_in(bb):
        t = bb.reshape(S5_BLOCKS, S5_GROUPS_PER_BLOCK, B_STATE, B_GROUP_DIM).transpose(0, 1, 3, 2)
        return jnp.einsum("kgcp,gm->kgcmp", t, eye).reshape(S5_BLOCKS, V7X_LANES, S5_BLOCK_STATES)

    def pack_out(cc):
        t = cc.reshape(S5_BLOCKS, S5_GROUPS_PER_BLOCK, B_GROUP_DIM, B_STATE).transpose(0, 1, 3, 2)
        return jnp.einsum("kgpc,gm->kgpmc", t, eye).reshape(S5_BLOCKS, S5_BLOCK_STATES, V7X_LANES)

    w_in = jnp.concatenate([pack_in(bb_re), pack_in(bb_im)], axis=-1).astype(BF16)
    w_out = jnp.concatenate([pack_out(c_re.astype(F32)), -pack_out(c_im.astype(F32))], axis=1).astype(BF16)
    return ab_re.reshape(1, S5_STATES), ab_im.reshape(1, S5_STATES), w_in, w_out


def _s5_kernel(xb_ref, win_ref, wout_ref, are_ref, aim_ref, d_ref, h0r_ref, h0i_ref,
               ys_ref, hr_ref, hi_ref, sre, sim, *, nb):
    @pl.when(pl.program_id(0) == 0)
    def _():
        hr_ref[...] = h0r_ref[...]
        hi_ref[...] = h0i_ref[...]

    x = xb_ref[...]
    xbf = x.astype(BF16)
    for k in range(S5_BLOCKS):
        drive = jnp.dot(xbf[:, k * V7X_LANES:(k + 1) * V7X_LANES], win_ref[k], preferred_element_type=F32)
        sre[:, k * S5_BLOCK_STATES:(k + 1) * S5_BLOCK_STATES] = drive[:, :S5_BLOCK_STATES]
        sim[:, k * S5_BLOCK_STATES:(k + 1) * S5_BLOCK_STATES] = drive[:, S5_BLOCK_STATES:]

    a_re = are_ref[...]
    a_im = aim_ref[...]
    sub = lax.broadcasted_iota(jnp.int32, (V7X_SUBLANES, S5_STATES), 0)
    for i in range(x.shape[0] // V7X_SUBLANES):
        rows = slice(i * V7X_SUBLANES, (i + 1) * V7X_SUBLANES)
        prev = slice((i - 1) * V7X_SUBLANES, i * V7X_SUBLANES)
        c_re = hr_ref[...] if i == 0 else sre[prev, :]
        c_im = hi_ref[...] if i == 0 else sim[prev, :]
        d_re = sre[rows, :]
        d_im = sim[rows, :]
        o_re = o_im = None
        for s in range(V7X_SUBLANES // nb):
            if nb < V7X_SUBLANES:
                c_re = pltpu.roll(c_re, nb, 0)
                c_im = pltpu.roll(c_im, nb, 0)
            n_re = a_re * c_re - a_im * c_im + d_re
            n_im = a_re * c_im + a_im * c_re + d_im
            o_re = n_re if s == 0 else jnp.where(sub >= s * nb, n_re, o_re)
            o_im = n_im if s == 0 else jnp.where(sub >= s * nb, n_im, o_im)
            c_re, c_im = n_re, n_im
        sre[rows, :] = o_re
        sim[rows, :] = o_im
    last = slice(x.shape[0] - V7X_SUBLANES, x.shape[0])
    hr_ref[...] = sre[last, :]
    hi_ref[...] = sim[last, :]

    for k in range(S5_BLOCKS):
        st = slice(k * S5_BLOCK_STATES, (k + 1) * S5_BLOCK_STATES)
        ch = slice(k * V7X_LANES, (k + 1) * V7X_LANES)
        y = (jnp.dot(sre[:, st].astype(BF16), wout_ref[k, :S5_BLOCK_STATES, :], preferred_element_type=F32)
             + jnp.dot(sim[:, st].astype(BF16), wout_ref[k, S5_BLOCK_STATES:, :], preferred_element_type=F32)
             + d_ref[:, ch] * x[:, ch])
        ys_ref[:, ch] = jax.nn.gelu(y)


def _s5(xb_tm, nb, disc, d_skip, h0_re, h0_im, steps):
    rows = xb_tm.shape[0]
    a_re, a_im, w_in, w_out = disc
    const2 = lambda c: (0, 0)
    const3 = lambda c: (0, 0, 0)
    blk = steps * nb
    pad = jnp.zeros((V7X_SUBLANES - nb, S5_STATES), F32)
    state = pl.BlockSpec((V7X_SUBLANES, S5_STATES), const2)
    ys, h_re, h_im = pl.pallas_call(
        functools.partial(_s5_kernel, nb=nb),
        grid=(rows // blk,),
        in_specs=[pl.BlockSpec((blk, B_WIDTH), lambda c: (c, 0)),
                  pl.BlockSpec(w_in.shape, const3),
                  pl.BlockSpec(w_out.shape, const3),
                  pl.BlockSpec((1, S5_STATES), const2),
                  pl.BlockSpec((1, S5_STATES), const2),
                  pl.BlockSpec((1, B_WIDTH), const2),
                  state, state],
        out_specs=[pl.BlockSpec((blk, B_WIDTH), lambda c: (c, 0)), state, state],
        out_shape=[jax.ShapeDtypeStruct((rows, B_WIDTH), F32),
                   jax.ShapeDtypeStruct((V7X_SUBLANES, S5_STATES), F32),
                   jax.ShapeDtypeStruct((V7X_SUBLANES, S5_STATES), F32)],
        scratch_shapes=[pltpu.VMEM((blk, S5_STATES), F32),
                        pltpu.VMEM((blk, S5_STATES), F32)],
        compiler_params=_params("arbitrary"),
        name="s5_scan",
    )(xb_tm, w_in, w_out, a_re, a_im, d_skip.reshape(1, B_WIDTH),
      jnp.concatenate([pad, h0_re], axis=0), jnp.concatenate([pad, h0_im], axis=0))
    return ys, h_re[V7X_SUBLANES - nb:], h_im[V7X_SUBLANES - nb:]


def _cast_kernel(x_ref, o_ref):
    o_ref[...] = x_ref[...].astype(BF16)


def _cast_t_kernel(x_ref, o_ref):
    o_ref[...] = x_ref[...].T.astype(BF16)


def _to_bf16(x, transpose):
    r, c = x.shape
    tr, tc = 512, 2048
    if transpose:
        return pl.pallas_call(
            _cast_t_kernel, grid=(r // tr, c // tc),
            in_specs=[pl.BlockSpec((tr, tc), lambda i, j: (i, j))],
            out_specs=pl.BlockSpec((tc, tr), lambda i, j: (j, i)),
            out_shape=jax.ShapeDtypeStruct((c, r), BF16),
            compiler_params=_params("parallel", "parallel"), name="cast_transpose")(x)
    return pl.pallas_call(
        _cast_kernel, grid=(r // tr, c // tc),
        in_specs=[pl.BlockSpec((tr, tc), lambda i, j: (i, j))],
        out_specs=pl.BlockSpec((tr, tc), lambda i, j: (i, j)),
        out_shape=jax.ShapeDtypeStruct((r, c), BF16),
        compiler_params=_params("parallel", "parallel"), name="cast_bf16")(x)


_STAIR_LIMIT = [PEER_TOPK // (a + 1) for a in range(PEER_TOPK)]


def _extract_top(work, order, count):
    big = jnp.int32(1 << 30)
    vals, picks = [], []
    for _ in range(count):
        m = jnp.max(work, axis=0, keepdims=True)
        pick = jnp.min(jnp.where(work == m, order, big), axis=0, keepdims=True)
        work = jnp.where(order == pick, -jnp.inf, work)
        vals.append(m)
        picks.append(pick)
    return vals, picks, work


def _stack_rows(rows, row_idx):
    out = jnp.broadcast_to(rows[0], row_idx.shape)
    for r in range(1, len(rows)):
        out = jnp.where(row_idx == r, rows[r], out)
    return out


def _topk_kernel(q_ref, keys_ref, s1m_ref, e1n_ref, s2m_ref, e2_ref, tau_ref, *, tm):
    key_idx = lax.broadcasted_iota(jnp.int32, (PEER_NKEYS, tm), 0)
    iota16 = lax.broadcasted_iota(jnp.int32, (PEER_TOPK, tm), 0)
    iota8 = iota16[:8]
    flat = [iota16] + [a * PEER_TOPK + iota8 for a in range(1, 8)] + [(iota8 + 8) * PEER_TOPK]
    flat = jnp.concatenate(flat, axis=0)
    for h in range(PEER_HEADS):
        scores, kept, tops = [], [], []
        for s in range(2):
            col = (2 * h + s) * PEER_SUBDIM
            qs = q_ref[:, col:col + PEER_SUBDIM].astype(BF16)
            ks = keys_ref[h, s].astype(BF16)
            sc = lax.dot_general(ks, qs, (((1,), (1,)), ((), ())), preferred_element_type=F32)
            vals, _, left = _extract_top(sc, key_idx, PEER_TOPK)
            scores.append(sc)
            kept.append(jnp.where(left == -jnp.inf, sc, -jnp.inf))
            tops.append(vals)
        v1, v2 = tops
        v2_16 = _stack_rows(v2, iota16)
        v2_8 = v2_16[:8]
        cand = [v1[0] + v2_16]
        for a in range(1, 8):
            cand.append(jnp.where(iota8 < _STAIR_LIMIT[a], v1[a] + v2_8, -jnp.inf))
        cand.append(_stack_rows(v1[8:], iota8) + v2[0])
        cand = jnp.concatenate(cand, axis=0)
        cv, _, _ = _extract_top(cand, flat, PEER_TOPK)
        z = jnp.ones_like(cv[0])
        for r in range(1, PEER_TOPK):
            z = z + jnp.exp(cv[r] - cv[0])
        s1m_ref[h] = kept[0]
        s2m_ref[h] = kept[1]
        e1n_ref[h] = jnp.exp(scores[0] - v1[0]) / z
        e2_ref[h] = jnp.exp(scores[1] - v2[0])
        tau_ref[h:h + 1, :] = cv[PEER_TOPK - 1]


def _peer_topk(q, keys, tm):
    n = q.shape[0]
    per_key = pl.BlockSpec((PEER_HEADS, PEER_NKEYS, tm), lambda i: (0, 0, i))
    shape = jax.ShapeDtypeStruct((PEER_HEADS, PEER_NKEYS, n), F32)
    return pl.pallas_call(
        functools.partial(_topk_kernel, tm=tm),
        grid=(n // tm,),
        in_specs=[pl.BlockSpec((tm, q.shape[1]), lambda i: (i, 0)),
                  pl.BlockSpec(keys.shape, lambda i: (0, 0, 0, 0))],
        out_specs=[per_key, per_key, per_key, per_key, pl.BlockSpec((PEER_HEADS, tm), lambda i: (0, i))],
        out_shape=[shape, shape, shape, shape, jax.ShapeDtypeStruct((PEER_HEADS, n), F32)],
        compiler_params=_params("parallel"),
        name="peer_topk",
    )(q, keys)


def _peer_kernel(ht_ref, u_ref, vt_ref, s1m_ref, e1n_ref, s2m_ref, e2_ref, tau_ref, o_ref, pt_ref, *, te):
    j = pl.program_id(1)

    @pl.when(j == 0)
    def _():
        o_ref[...] = jnp.zeros_like(o_ref)

    act = jnp.dot(u_ref[...], ht_ref[...], preferred_element_type=F32)
    slabs = te // PEER_NKEYS
    for il in range(slabs):
        i1 = j * slabs + il
        rows = slice(il * PEER_NKEYS, (il + 1) * PEER_NKEYS)
        w = jnp.zeros((PEER_NKEYS, act.shape[1]), F32)
        for h in range(PEER_HEADS):
            c = s2m_ref[h] + s1m_ref[h, pl.ds(i1, 1), :]
            wt = e2_ref[h] * e1n_ref[h, pl.ds(i1, 1), :]
            w = w + jnp.where(c >= tau_ref[h:h + 1, :], wt, 0.0)
        pt_ref[rows, :] = (w * jax.nn.gelu(act[rows, :])).astype(BF16)
    o_ref[...] += jnp.dot(vt_ref[...], pt_ref[...], preferred_element_type=F32)


def _peer_dense(ht, u_bf, vt_bf, topk, tm, te):
    d, n = ht.shape
    s1m, e1n, s2m, e2, tau = topk
    per_tile = dict(pipeline_mode=pl.Buffered(1))
    per_key = pl.BlockSpec((PEER_HEADS, PEER_NKEYS, tm), lambda i, j: (0, 0, i), **per_tile)
    return pl.pallas_call(
        functools.partial(_peer_kernel, te=te),
        grid=(n // tm, PEER_EXPERTS // te),
        in_specs=[pl.BlockSpec((d, tm), lambda i, j: (0, i), **per_tile),
                  pl.BlockSpec((te, d), lambda i, j: (j, 0)),
                  pl.BlockSpec((d, te), lambda i, j: (0, j)),
                  per_key, per_key, per_key, per_key,
                  pl.BlockSpec((PEER_HEADS, tm), lambda i, j: (0, i), **per_tile)],
        out_specs=pl.BlockSpec((d, tm), lambda i, j: (0, i)),
        out_shape=jax.ShapeDtypeStruct((d, n), F32),
        scratch_shapes=[pltpu.VMEM((te, tm), BF16)],
        compiler_params=_params("parallel", "arbitrary"),
        name="peer_dense",
    )(ht, u_bf, vt_bf, s1m, e1n, s2m, e2, tau)


def _split_mod(mod, tokens_per_stream, per_token):
    if per_token:
        mod = jnp.repeat(mod, tokens_per_stream, axis=0)[None]
    else:
        mod = mod[:, None, :]
    return jnp.split(mod, 6, axis=-1)


def _encoder_layer(x, mod, h0_re, h0_im, lp, peer_w, *, tr, scan_steps, mixer_rows, per_token_mod, emit_v):
    (g_pre_tok, g_post_tok, g_pre_ch, g_post_ch, w_in, a_ws, a_bs, a_ln_g, a_ln_b, disc, s5_d,
     w_glu, b_glu, w_branch, w_out, peer_w_query, peer_sub_keys) = lp
    u_bf, vt_bf = peer_w
    nb, t, d = x.shape
    n = nb * t
    tiles_per_group = 1 if per_token_mod else t // tr
    sh_t, sc_t, gt_t, sh_c, sc_c, gt_c = _split_mod(mod, t, per_token_mod)
    x2 = x.reshape(n, d)

    h = _prenorm(x2, g_pre_tok, sc_t, sh_t, tr, tiles_per_group)
    proj = _matmul(h, w_in, 1024, 512, name="in_proj")
    proj3 = proj.reshape(nb, t, IN_WIDTH)
    mix = _mixer_a(proj3, a_ws, a_bs, a_ln_g, a_ln_b, mixer_rows, emit_v)
    ya = mix[0].reshape(n, A_WIDTH)
    v_rows = mix[1] if emit_v else None
    xb_tm = proj3[:, :, 2 * A_WIDTH:2 * A_WIDTH + B_WIDTH].transpose(1, 0, 2).reshape(n, B_WIDTH)
    ys_tm, h_re, h_im = _s5(xb_tm, nb, disc, s5_d, h0_re, h0_im, scan_steps)
    yb_tm = _glu(ys_tm, w_glu, b_glu, 1024, 512)
    yb = yb_tm.reshape(t, nb, B_WIDTH).transpose(1, 0, 2).reshape(n, B_WIDTH)
    merged = _branch_merge(ya, yb, w_branch, proj, 512, 512)
    m = _matmul(merged, w_out, 1024, 512, name="out_proj")

    x1, h2, h2t = _resid_prenorm(x2, m, g_post_tok, gt_t, g_pre_ch, sc_c, sh_c, tr, tiles_per_group)
    q = _matmul(h2, peer_w_query, 1024, 512, name="peer_query")
    tm = min(256, n)
    topk = _peer_topk(q, peer_sub_keys, min(128, n))
    ft = _peer_dense(h2t, u_bf, vt_bf, topk, tm, 512)
    y = _final_resid(x1, ft, g_post_ch, gt_c, tr, tiles_per_group)
    return (y.reshape(nb, t, d), h_re.reshape(nb, B_GROUPS, B_STATE), h_im.reshape(nb, B_GROUPS, B_STATE), v_rows)


def kernel(x_prompt, x_sample, c_prompt, c_sample, state_ssm_re, state_ssm_im, w_ada, b_ada, g_pre_tok,
           g_post_tok, g_pre_ch, g_post_ch, w_in, a_ws, a_bs, a_ln_g, a_ln_b, s5_lam_re, s5_lam_im,
           s5_log_dt, s5_b_re, s5_b_im, s5_c_re, s5_c_im, s5_d, w_glu, b_glu, w_branch, w_out,
           peer_w_query, peer_sub_keys, peer_u, peer_v):
    depth = w_ada.shape[0]
    nb_p, t_p, _ = x_prompt.shape
    nb_s, t_s, _ = x_sample.shape
    y_p, y_s = x_prompt, x_sample
    re_p, im_p, re_s, im_s, v_s = [], [], [], [], []
    pad = (-(nb_p + nb_s)) % 8
    for l in range(depth):
        c_all = jnp.concatenate([c_prompt, c_sample, jnp.zeros((pad, D_MODEL), F32)], axis=0)
        mod = _ada(c_all, w_ada[l], b_ada[l])
        disc = _s5_discretize(s5_lam_re[l], s5_lam_im[l], s5_log_dt[l], s5_b_re[l], s5_b_im[l],
                              s5_c_re[l], s5_c_im[l])
        lp = (g_pre_tok[l], g_post_tok[l], g_pre_ch[l], g_post_ch[l], w_in[l], a_ws[l], a_bs[l],
              a_ln_g[l], a_ln_b[l], disc, s5_d[l], w_glu[l], b_glu[l], w_branch[l], w_out[l],
              peer_w_query[l], peer_sub_keys[l])
        peer_w = (_to_bf16(peer_u[l], transpose=False), _to_bf16(peer_v[l], transpose=True))
        zeros = jnp.zeros((nb_p, S5_STATES), F32)
        y_p, hr, hi, _ = _encoder_layer(
            y_p, mod[:nb_p], zeros, zeros, lp, peer_w,
            tr=256, scan_steps=32, mixer_rows=A_CHUNK, per_token_mod=False, emit_v=False)
        re_p.append(hr)
        im_p.append(hi)
        y_s, hr, hi, v_rows = _encoder_layer(
            y_s, mod[nb_p:nb_p + nb_s], state_ssm_re[l].reshape(nb_s, S5_STATES),
            state_ssm_im[l].reshape(nb_s, S5_STATES), lp, peer_w,
            tr=nb_s * t_s, scan_steps=t_s, mixer_rows=t_s, per_token_mod=True, emit_v=True)
        re_s.append(hr)
        im_s.append(hi)
        v_s.append(v_rows)
    return (y_p, y_s, jnp.stack(re_p), jnp.stack(im_p), jnp.stack(re_s), jnp.stack(im_s), jnp.stack(v_s))
```

```python
import functools

import jax
import jax.numpy as jnp
from jax import lax
from jax.experimental import pallas as pl
from jax.experimental.pallas import tpu as pltpu

F32 = jnp.float32
BF16 = jnp.bfloat16

D_MODEL = 4096
CHUNK = 64
A_CHUNK = 128
A_GROUPS = 16
A_GROUP_DIM = 128
A_WIDTH = A_GROUPS * A_GROUP_DIM
B_GROUP_DIM = 16
B_GROUPS = 128
B_WIDTH = B_GROUPS * B_GROUP_DIM
B_STATE = 64
S5_STATES = B_GROUPS * B_STATE
S5_GROUPS_PER_BLOCK = 8
S5_BLOCKS = B_GROUPS // S5_GROUPS_PER_BLOCK
S5_BLOCK_STATES = S5_GROUPS_PER_BLOCK * B_STATE
IN_WIDTH = 2 * A_WIDTH + B_WIDTH + 2 * D_MODEL
PEER_HEADS = 8
PEER_SUBDIM = 128
PEER_NKEYS = 128
PEER_EXPERTS = PEER_NKEYS * PEER_NKEYS
PEER_TOPK = 16
PEER_GATE_ROWS = 32
PEER_K_CHUNK = 1024
PEER_M_CHUNK = 512
EPS = 1e-6

V7X_LANES = 128
V7X_SUBLANES = 8
V7X_VMEM_BYTES = 64 * 1024 * 1024
VMEM_LIMIT_BYTES = 56 * 1024 * 1024


def _params(*semantics):
    return pltpu.CompilerParams(dimension_semantics=semantics, vmem_limit_bytes=VMEM_LIMIT_BYTES)


def _rms(x):
    return x * lax.rsqrt(jnp.mean(x * x, axis=-1, keepdims=True) + EPS)


def _ada_kernel(c_ref, w_ref, b_ref, o_ref):
    s = jax.nn.silu(c_ref[...]).astype(BF16)
    o_ref[...] = jnp.dot(s, w_ref[...].astype(BF16), preferred_element_type=F32) + b_ref[...]


def _ada(c, w, b):
    rows, d = c.shape
    n = w.shape[1]
    tn = 512
    return pl.pallas_call(
        _ada_kernel,
        grid=(n // tn,),
        in_specs=[pl.BlockSpec((rows, d), lambda j: (0, 0)),
                  pl.BlockSpec((d, tn), lambda j: (0, j)),
                  pl.BlockSpec((1, tn), lambda j: (0, j))],
        out_specs=pl.BlockSpec((rows, tn), lambda j: (0, j)),
        out_shape=jax.ShapeDtypeStruct((rows, n), F32),
        compiler_params=_params("parallel"),
        name="ada_mod",
    )(c, w, b.reshape(1, n))


def _mod_spec(mod, tiles_per_group):
    return pl.BlockSpec((None,) + mod.shape[1:], lambda i: (i // tiles_per_group, 0, 0))


def _prenorm_kernel(x_ref, g_ref, sc_ref, sh_ref, o_ref):
    h = _rms(x_ref[...]) * g_ref[...] * (1.0 + sc_ref[...]) + sh_ref[...]
    o_ref[...] = h.astype(BF16)


def _prenorm(x, g, sc, sh, tr, tiles_per_group):
    n, d = x.shape
    return pl.pallas_call(
        _prenorm_kernel,
        grid=(n // tr,),
        in_specs=[pl.BlockSpec((tr, d), lambda i: (i, 0)),
                  pl.BlockSpec((1, d), lambda i: (0, 0)),
                  _mod_spec(sc, tiles_per_group),
                  _mod_spec(sh, tiles_per_group)],
        out_specs=pl.BlockSpec((tr, d), lambda i: (i, 0)),
        out_shape=jax.ShapeDtypeStruct((n, d), BF16),
        compiler_params=_params("parallel"),
        name="prenorm_tok",
    )(x, g.reshape(1, d), sc, sh)


def _resid_kernel(x_ref, m_ref, gpost_ref, gt_ref, gpre_ref, sc_ref, sh_ref, x1_ref, h_ref, ht_ref):
    x1 = x_ref[...] + gt_ref[...] * (_rms(m_ref[...]) * gpost_ref[...])
    x1_ref[...] = x1
    h = _rms(x1) * gpre_ref[...] * (1.0 + sc_ref[...]) + sh_ref[...]
    h_ref[...] = h.astype(BF16)
    ht_ref[...] = h.T.astype(BF16)


def _resid_prenorm(x, m, gpost, gt, gpre, sc, sh, tr, tiles_per_group):
    n, d = x.shape
    row = pl.BlockSpec((tr, d), lambda i: (i, 0))
    vec = pl.BlockSpec((1, d), lambda i: (0, 0))
    return pl.pallas_call(
        _resid_kernel,
        grid=(n // tr,),
        in_specs=[row, row, vec, _mod_spec(gt, tiles_per_group), vec,
                  _mod_spec(sc, tiles_per_group), _mod_spec(sh, tiles_per_group)],
        out_specs=[row, row, pl.BlockSpec((d, tr), lambda i: (0, i))],
        out_shape=[jax.ShapeDtypeStruct((n, d), F32),
                   jax.ShapeDtypeStruct((n, d), BF16),
                   jax.ShapeDtypeStruct((d, n), BF16)],
        compiler_params=_params("parallel"),
        name="resid_prenorm_ch",
    )(x, m, gpost.reshape(1, d), gt, gpre.reshape(1, d), sc, sh)


def _final_kernel(x_ref, ft_ref, gpost_ref, gt_ref, o_ref):
    f = ft_ref[...].T
    o_ref[...] = x_ref[...] + gt_ref[...] * (_rms(f) * gpost_ref[...])


def _final_resid(x, ft, gpost, gt, tr, tiles_per_group):
    n, d = x.shape
    row = pl.BlockSpec((tr, d), lambda i: (i, 0))
    return pl.pallas_call(
        _final_kernel,
        grid=(n // tr,),
        in_specs=[row, pl.BlockSpec((d, tr), lambda i: (0, i)),
                  pl.BlockSpec((1, d), lambda i: (0, 0)), _mod_spec(gt, tiles_per_group)],
        out_specs=row,
        out_shape=jax.ShapeDtypeStruct((n, d), F32),
        compiler_params=_params("parallel"),
        name="final_resid",
    )(x, ft, gpost.reshape(1, d), gt)


def _mm_kernel(a_ref, w_ref, o_ref):
    o_ref[...] = jnp.dot(a_ref[...].astype(BF16), w_ref[...].astype(BF16),
                         preferred_element_type=F32).astype(o_ref.dtype)


def _matmul(a, w, tm, tn, out_dtype=F32, name="matmul"):
    m, k = a.shape
    n = w.shape[1]
    tm = min(tm, m)
    return pl.pallas_call(
        _mm_kernel,
        grid=(m // tm, n // tn),
        in_specs=[pl.BlockSpec((tm, k), lambda i, j: (i, 0)),
                  pl.BlockSpec((k, tn), lambda i, j: (0, j))],
        out_specs=pl.BlockSpec((tm, tn), lambda i, j: (i, j)),
        out_shape=jax.ShapeDtypeStruct((m, n), out_dtype),
        compiler_params=_params("parallel", "parallel"),
        name=name,
    )(a, w)


def _glu_kernel(y_ref, w_ref, b_ref, yj_ref, o_ref):
    acc = jnp.dot(y_ref[...].astype(BF16), w_ref[...].astype(BF16), preferred_element_type=F32)
    o_ref[...] = (yj_ref[...] * jax.nn.sigmoid(acc + b_ref[...])).astype(BF16)


def _glu(y, w, b, tm, tn):
    m, k = y.shape
    n = w.shape[1]
    tm = min(tm, m)
    return pl.pallas_call(
        _glu_kernel,
        grid=(m // tm, n // tn),
        in_specs=[pl.BlockSpec((tm, k), lambda i, j: (i, 0)),
                  pl.BlockSpec((k, tn), lambda i, j: (0, j)),
                  pl.BlockSpec((1, tn), lambda i, j: (0, j)),
                  pl.BlockSpec((tm, tn), lambda i, j: (i, j))],
        out_specs=pl.BlockSpec((tm, tn), lambda i, j: (i, j)),
        out_shape=jax.ShapeDtypeStruct((m, n), BF16),
        compiler_params=_params("parallel", "parallel"),
        name="s5_glu",
    )(y, w, b.reshape(1, n), y)


def _branch_kernel(ya_ref, yb_ref, wa_ref, wb_ref, ga_ref, gb_ref, o_ref):
    a = jnp.dot(ya_ref[...], wa_ref[...].astype(BF16), preferred_element_type=F32)
    b = jnp.dot(yb_ref[...], wb_ref[...].astype(BF16), preferred_element_type=F32)
    o_ref[...] = (jax.nn.sigmoid(ga_ref[...]) * a + jax.nn.sigmoid(gb_ref[...]) * b).astype(BF16)


def _branch_merge(ya, yb, w_branch, proj, tm, tn):
    m = ya.shape[0]
    n = w_branch.shape[1]
    tm = min(tm, m)
    ga_blk = (2 * A_WIDTH + B_WIDTH) // tn
    gb_blk = ga_blk + D_MODEL // tn
    return pl.pallas_call(
        _branch_kernel,
        grid=(n // tn, m // tm),
        in_specs=[pl.BlockSpec((tm, A_WIDTH), lambda j, i: (i, 0)),
                  pl.BlockSpec((tm, B_WIDTH), lambda j, i: (i, 0)),
                  pl.BlockSpec((A_WIDTH, tn), lambda j, i: (0, j)),
                  pl.BlockSpec((B_WIDTH, tn), lambda j, i: (A_WIDTH // B_WIDTH, j)),
                  pl.BlockSpec((tm, tn), lambda j, i: (i, ga_blk + j)),
                  pl.BlockSpec((tm, tn), lambda j, i: (i, gb_blk + j))],
        out_specs=pl.BlockSpec((tm, tn), lambda j, i: (i, j)),
        out_shape=jax.ShapeDtypeStruct((m, n), BF16),
        compiler_params=_params("parallel", "parallel"),
        name="branch_merge",
    )(ya, yb, w_branch, w_branch, proj, proj)


def _mixer_a_kernel(za_ref, ws_ref, bias_ref, lng_ref, lnb_ref, ya_ref, *v_refs, rows):
    z = jax.nn.gelu(za_ref[...])
    u = z[:, :A_WIDTH]
    v = z[:, A_WIDTH:]
    d = v - jnp.mean(v, axis=-1, keepdims=True)
    vn = d * lax.rsqrt(jnp.mean(d * d, axis=-1, keepdims=True) + EPS) * lng_ref[...] + lnb_ref[...]
    if v_refs:
        v_refs[0][...] = vn
    if rows < A_CHUNK:
        vn = jnp.concatenate([vn, jnp.zeros((A_CHUNK - rows, A_WIDTH), F32)], axis=0)
    vb = vn.astype(BF16)
    blk_i = lax.broadcasted_iota(jnp.int32, (A_CHUNK, A_CHUNK), 0) // CHUNK
    blk_j = lax.broadcasted_iota(jnp.int32, (A_CHUNK, A_CHUNK), 1) // CHUNK
    causal = blk_i >= blk_j
    for g in range(A_GROUPS):
        cols = slice(g * A_GROUP_DIM, (g + 1) * A_GROUP_DIM)
        w = jnp.where(causal, ws_ref[g], 0.0).astype(BF16)
        mixed = jnp.dot(w, vb[:, cols], preferred_element_type=F32) + bias_ref[:, cols]
        ya_ref[:, cols] = (u[:, cols] * mixed[:rows]).astype(BF16)


def _mixer_a(proj3, ws, bs, ln_g, ln_b, rows, emit_v):
    nb, t, _ = proj3.shape
    bias = jnp.repeat(bs.T, A_GROUP_DIM, axis=1)
    out_shape = [jax.ShapeDtypeStruct((nb, t, A_WIDTH), BF16)]
    out_specs = [pl.BlockSpec((None, rows, A_WIDTH), lambda b, c: (b, c, 0))]
    if emit_v:
        out_shape.append(jax.ShapeDtypeStruct((nb, t, A_WIDTH), F32))
        out_specs.append(pl.BlockSpec((None, rows, A_WIDTH), lambda b, c: (b, c, 0)))
    return pl.pallas_call(
        functools.partial(_mixer_a_kernel, rows=rows),
        grid=(nb, t // rows),
        in_specs=[pl.BlockSpec((None, rows, 2 * A_WIDTH), lambda b, c: (b, c, 0)),
                  pl.BlockSpec((A_GROUPS, A_CHUNK, A_CHUNK), lambda b, c: (0, 0, 0)),
                  pl.BlockSpec((A_CHUNK, A_WIDTH), lambda b, c: (0, 0)),
                  pl.BlockSpec((1, A_WIDTH), lambda b, c: (0, 0)),
                  pl.BlockSpec((1, A_WIDTH), lambda b, c: (0, 0))],
        out_specs=out_specs,
        out_shape=out_shape,
        compiler_params=_params("parallel", "parallel"),
        name="mixer_a",
    )(proj3, ws, bias, ln_g.reshape(1, A_WIDTH), ln_b.reshape(1, A_WIDTH))


def _s5_discretize(lam_re, lam_im, log_dt, b_re, b_im, c_re, c_im):
    dt = jnp.exp(log_dt.astype(F32))[:, None]
    lr, li = lam_re.astype(F32), lam_im.astype(F32)
    mag = jnp.exp(lr * dt)
    ab_re, ab_im = mag * jnp.cos(li * dt), mag * jnp.sin(li * dt)
    den = lr * lr + li * li
    num_re = ab_re - 1.0
    coef_re = (num_re * lr + ab_im * li) / den
    coef_im = (ab_im * lr - num_re * li) / den
    br, bi = b_re.astype(F32), b_im.astype(F32)
    bb_re = coef_re[..., None] * br - coef_im[..., None] * bi
    bb_im = coef_re[..., None] * bi + coef_im[..., None] * br
    eye = jnp.eye(S5_GROUPS_PER_BLOCK, dtype=F32)

    def pack_in(bb):
        t = bb.reshape(S5_BLOCKS, S5_GROUPS_PER_BLOCK, B_STATE, B_GROUP_DIM).transpose(0, 1, 3, 2)
        return jnp.einsum("kgcp,gm->kgcmp", t, eye).reshape(S5_BLOCKS, V7X_LANES, S5_BLOCK_STATES)

    def pack_out(cc):
        t = cc.reshape(S5_BLOCKS, S5_GROUPS_PER_BLOCK, B_GROUP_DIM, B_STATE).transpose(0, 1, 3, 2)
        return jnp.einsum("kgpc,gm->kgpmc", t, eye).reshape(S5_BLOCKS, S5_BLOCK_STATES, V7X_LANES)

    w_in = jnp.concatenate([pack_in(bb_re), pack_in(bb_im)], axis=-1).astype(BF16)
    w_out = jnp.concatenate([pack_out(c_re.astype(F32)), -pack_out(c_im.astype(F32))], axis=1).astype(BF16)
    return ab_re.reshape(1, S5_STATES), ab_im.reshape(1, S5_STATES), w_in, w_out


def _s5_kernel(xb_ref, win_ref, wout_ref, are_ref, aim_ref, d_ref, h0r_ref, h0i_ref,
               ys_ref, hr_ref, hi_ref, sre, sim, *, nb):
    @pl.when(pl.program_id(0) == 0)
    def _():
        hr_ref[...] = h0r_ref[...]
        hi_ref[...] = h0i_ref[...]

    x = xb_ref[...]
    xbf = x.astype(BF16)
    for k in range(S5_BLOCKS):
        drive = jnp.dot(xbf[:, k * V7X_LANES:(k + 1) * V7X_LANES], win_ref[k], preferred_element_type=F32)
        sre[:, k * S5_BLOCK_STATES:(k + 1) * S5_BLOCK_STATES] = drive[:, :S5_BLOCK_STATES]
        sim[:, k * S5_BLOCK_STATES:(k + 1) * S5_BLOCK_STATES] = drive[:, S5_BLOCK_STATES:]

    a_re = are_ref[...]
    a_im = aim_ref[...]
    sub = lax.broadcasted_iota(jnp.int32, (V7X_SUBLANES, S5_STATES), 0)
    for i in range(x.shape[0] // V7X_SUBLANES):
        rows = slice(i * V7X_SUBLANES, (i + 1) * V7X_SUBLANES)
        prev = slice((i - 1) * V7X_SUBLANES, i * V7X_SUBLANES)
        c_re = hr_ref[...] if i == 0 else sre[prev, :]
        c_im = hi_ref[...] if i == 0 else sim[prev, :]
        d_re = sre[rows, :]
        d_im = sim[rows, :]
        o_re = o_im = None
        for s in range(V7X_SUBLANES // nb):
            if nb < V7X_SUBLANES:
                c_re = pltpu.roll(c_re, nb, 0)
                c_im = pltpu.roll(c_im, nb, 0)
            n_re = a_re * c_re - a_im * c_im + d_re
            n_im = a_re * c_im + a_im * c_re + d_im
            o_re = n_re if s == 0 else jnp.where(sub >= s * nb, n_re, o_re)
            o_im = n_im if s == 0 else jnp.where(sub >= s * nb, n_im, o_im)
            c_re, c_im = n_re, n_im
        sre[rows, :] = o_re
        sim[rows, :] = o_im
    last = slice(x.shape[0] - V7X_SUBLANES, x.shape[0])
    hr_ref[...] = sre[last, :]
    hi_ref[...] = sim[last, :]

    for k in range(S5_BLOCKS):
        st = slice(k * S5_BLOCK_STATES, (k + 1) * S5_BLOCK_STATES)
        ch = slice(k * V7X_LANES, (k + 1) * V7X_LANES)
        y = (jnp.dot(sre[:, st].astype(BF16), wout_ref[k, :S5_BLOCK_STATES, :], preferred_element_type=F32)
             + jnp.dot(sim[:, st].astype(BF16), wout_ref[k, S5_BLOCK_STATES:, :], preferred_element_type=F32)
             + d_ref[:, ch] * x[:, ch])
        ys_ref[:, ch] = jax.nn.gelu(y)


def _s5(xb_tm, nb, disc, d_skip, h0_re, h0_im, steps):
    rows = xb_tm.shape[0]
    a_re, a_im, w_in, w_out = disc
    const2 = lambda c: (0, 0)
    const3 = lambda c: (0, 0, 0)
    blk = steps * nb
    pad = jnp.zeros((V7X_SUBLANES - nb, S5_STATES), F32)
    state = pl.BlockSpec((V7X_SUBLANES, S5_STATES), const2)
    ys, h_re, h_im = pl.pallas_call(
        functools.partial(_s5_kernel, nb=nb),
        grid=(rows // blk,),
        in_specs=[pl.BlockSpec((blk, B_WIDTH), lambda c: (c, 0)),
                  pl.BlockSpec(w_in.shape, const3),
                  pl.BlockSpec(w_out.shape, const3),
                  pl.BlockSpec((1, S5_STATES), const2),
                  pl.BlockSpec((1, S5_STATES), const2),
                  pl.BlockSpec((1, B_WIDTH), const2),
                  state, state],
        out_specs=[pl.BlockSpec((blk, B_WIDTH), lambda c: (c, 0)), state, state],
        out_shape=[jax.ShapeDtypeStruct((rows, B_WIDTH), F32),
                   jax.ShapeDtypeStruct((V7X_SUBLANES, S5_STATES), F32),
                   jax.ShapeDtypeStruct((V7X_SUBLANES, S5_STATES), F32)],
        scratch_shapes=[pltpu.VMEM((blk, S5_STATES), F32),
                        pltpu.VMEM((blk, S5_STATES), F32)],
        compiler_params=_params("arbitrary"),
        name="s5_scan",
    )(xb_tm, w_in, w_out, a_re, a_im, d_skip.reshape(1, B_WIDTH),
      jnp.concatenate([pad, h0_re], axis=0), jnp.concatenate([pad, h0_im], axis=0))
    return ys, h_re[V7X_SUBLANES - nb:], h_im[V7X_SUBLANES - nb:]


def _cast_kernel(x_ref, o_ref):
    o_ref[...] = x_ref[...].astype(BF16)


def _cast_t_kernel(x_ref, o_ref):
    o_ref[...] = x_ref[...].T.astype(BF16)


def _to_bf16(x, transpose):
    r, c = x.shape
    tr, tc = 512, 2048
    if transpose:
        return pl.pallas_call(
            _cast_t_kernel, grid=(r // tr, c // tc),
            in_specs=[pl.BlockSpec((tr, tc), lambda i, j: (i, j))],
            out_specs=pl.BlockSpec((tc, tr), lambda i, j: (j, i)),
            out_shape=jax.ShapeDtypeStruct((c, r), BF16),
            compiler_params=_params("parallel", "parallel"), name="cast_transpose")(x)
    return pl.pallas_call(
        _cast_kernel, grid=(r // tr, c // tc),
        in_specs=[pl.BlockSpec((tr, tc), lambda i, j: (i, j))],
        out_specs=pl.BlockSpec((tr, tc), lambda i, j: (i, j)),
        out_shape=jax.ShapeDtypeStruct((r, c), BF16),
        compiler_params=_params("parallel", "parallel"), name="cast_bf16")(x)


_STAIR_LIMIT = [PEER_TOPK // (a + 1) for a in range(PEER_TOPK)]


def _extract_top(work, order, count):
    big = jnp.int32(1 << 30)
    vals, picks = [], []
    for _ in range(count):
        m = jnp.max(work, axis=0, keepdims=True)
        pick = jnp.min(jnp.where(work == m, order, big), axis=0, keepdims=True)
        work = jnp.where(order == pick, -jnp.inf, work)
        vals.append(m)
        picks.append(pick)
    return vals, picks, work


def _stack_rows(rows, row_idx):
    out = jnp.broadcast_to(rows[0], row_idx.shape)
    for r in range(1, len(rows)):
        out = jnp.where(row_idx == r, rows[r], out)
    return out


def _topk_kernel(q_ref, keys_ref, s1m_ref, e1n_ref, s2m_ref, e2_ref, tau_ref, *, tm):
    key_idx = lax.broadcasted_iota(jnp.int32, (PEER_NKEYS, tm), 0)
    iota16 = lax.broadcasted_iota(jnp.int32, (PEER_TOPK, tm), 0)
    iota8 = iota16[:8]
    flat = [iota16] + [a * PEER_TOPK + iota8 for a in range(1, 8)] + [(iota8 + 8) * PEER_TOPK]
    flat = jnp.concatenate(flat, axis=0)
    for h in range(PEER_HEADS):
        scores, kept, tops = [], [], []
        for s in range(2):
            col = (2 * h + s) * PEER_SUBDIM
            qs = q_ref[:, col:col + PEER_SUBDIM].astype(BF16)
            ks = keys_ref[h, s].astype(BF16)
            sc = lax.dot_general(ks, qs, (((1,), (1,)), ((), ())), preferred_element_type=F32)
            vals, _, left = _extract_top(sc, key_idx, PEER_TOPK)
            scores.append(sc)
            kept.append(jnp.where(left == -jnp.inf, sc, -jnp.inf))
            tops.append(vals)
        v1, v2 = tops
        v2_16 = _stack_rows(v2, iota16)
        v2_8 = v2_16[:8]
        cand = [v1[0] + v2_16]
        for a in range(1, 8):
            cand.append(jnp.where(iota8 < _STAIR_LIMIT[a], v1[a] + v2_8, -jnp.inf))
        cand.append(_stack_rows(v1[8:], iota8) + v2[0])
        cand = jnp.concatenate(cand, axis=0)
        cv, _, _ = _extract_top(cand, flat, PEER_TOPK)
        z = jnp.ones_like(cv[0])
        for r in range(1, PEER_TOPK):
            z = z + jnp.exp(cv[r] - cv[0])
        s1m_ref[h] = kept[0]
        s2m_ref[h] = kept[1]
        e1n_ref[h] = jnp.exp(scores[0] - v1[0]) / z
        e2_ref[h] = jnp.exp(scores[1] - v2[0])
        tau_ref[h:h + 1, :] = cv[PEER_TOPK - 1]


def _peer_topk(q, keys, tm):
    n = q.shape[0]
    per_key = pl.BlockSpec((PEER_HEADS, PEER_NKEYS, tm), lambda i: (0, 0, i))
    shape = jax.ShapeDtypeStruct((PEER_HEADS, PEER_NKEYS, n), F32)
    return pl.pallas_call(
        functools.partial(_topk_kernel, tm=tm),
        grid=(n // tm,),
        in_specs=[pl.BlockSpec((tm, q.shape[1]), lambda i: (i, 0)),
                  pl.BlockSpec(keys.shape, lambda i: (0, 0, 0, 0))],
        out_specs=[per_key, per_key, per_key, per_key, pl.BlockSpec((PEER_HEADS, tm), lambda i: (0, i))],
        out_shape=[shape, shape, shape, shape, jax.ShapeDtypeStruct((PEER_HEADS, n), F32)],
        compiler_params=_params("parallel"),
        name="peer_topk",
    )(q, keys)


def _peer_kernel(ht_ref, u_ref, vt_ref, s1m_ref, e1n_ref, s2m_ref, e2_ref, tau_ref, o_ref,
                 act_a, act_b, p_a, p_b, *, te, nblocks):
    j = pl.program_id(1)
    slabs = te // PEER_NKEYS

    @pl.when(j == 0)
    def _():
        o_ref[...] = jnp.zeros_like(o_ref)
        p_a[...] = jnp.zeros_like(p_a)
        act_b[...] = jnp.zeros_like(act_b)

    def step(act_new, act_old, p_new, p_old):
        first = jnp.clip(j - 1, 0, nblocks - 1) * slabs
        d = ht_ref.shape[0]

        def gate(il, r):
            i1 = first + il
            keys = slice(r * PEER_GATE_ROWS, (r + 1) * PEER_GATE_ROWS)
            rows = slice(il * PEER_NKEYS + r * PEER_GATE_ROWS, il * PEER_NKEYS + (r + 1) * PEER_GATE_ROWS)
            w = jnp.zeros((PEER_GATE_ROWS, act_old.shape[1]), F32)
            for h in range(PEER_HEADS):
                c = s2m_ref[h, keys, :] + s1m_ref[h, pl.ds(i1, 1), :]
                wt = e2_ref[h, keys, :] * e1n_ref[h, pl.ds(i1, 1), :]
                w = w + jnp.where(c >= tau_ref[h:h + 1, :], wt, 0.0)
            p_new[rows, :] = (w * jax.nn.gelu(act_old[rows, :])).astype(BF16)

        def pre_act(kc):
            ks = slice(kc * PEER_K_CHUNK, (kc + 1) * PEER_K_CHUNK)
            part = jnp.dot(u_ref[:, ks], ht_ref[ks, :], preferred_element_type=F32)
            if kc == 0:
                act_new[...] = part
            else:
                act_new[...] += part

        def accumulate(mc):
            ms = slice(mc * PEER_M_CHUNK, (mc + 1) * PEER_M_CHUNK)
            o_ref[ms, :] += jnp.dot(vt_ref[ms, :], p_old[...], preferred_element_type=F32)

        gates = [(il, r) for il in range(slabs) for r in range(PEER_NKEYS // PEER_GATE_ROWS)]
        per_k = PEER_K_CHUNK // PEER_M_CHUNK
        mxu = []
        for kc in range(d // PEER_K_CHUNK):
            mxu.append(functools.partial(pre_act, kc))
            mxu += [functools.partial(accumulate, kc * per_k + m) for m in range(per_k)]
        done = 0
        for n_mxu, piece in enumerate(mxu, start=1):
            piece()
            upto = (len(gates) * n_mxu) // len(mxu)
            for il, r in gates[done:upto]:
                gate(il, r)
            done = upto

    @pl.when(j % 2 == 0)
    def _():
        step(act_a, act_b, p_b, p_a)

    @pl.when(j % 2 == 1)
    def _():
        step(act_b, act_a, p_a, p_b)


def _peer_dense(ht, u_bf, vt_bf, topk, tm, te):
    d, n = ht.shape
    s1m, e1n, s2m, e2, tau = topk
    nblocks = PEER_EXPERTS // te
    per_tile = dict(pipeline_mode=pl.Buffered(1))
    per_key = pl.BlockSpec((PEER_HEADS, PEER_NKEYS, tm), lambda i, j: (0, 0, i), **per_tile)
    return pl.pallas_call(
        functools.partial(_peer_kernel, te=te, nblocks=nblocks),
        grid=(n // tm, nblocks + 2),
        in_specs=[pl.BlockSpec((d, tm), lambda i, j: (0, i), **per_tile),
                  pl.BlockSpec((te, d), lambda i, j: (jnp.minimum(j, nblocks - 1), 0)),
                  pl.BlockSpec((d, te), lambda i, j: (0, jnp.clip(j - 2, 0, nblocks - 1))),
                  per_key, per_key, per_key, per_key,
                  pl.BlockSpec((PEER_HEADS, tm), lambda i, j: (0, i), **per_tile)],
        out_specs=pl.BlockSpec((d, tm), lambda i, j: (0, i)),
        out_shape=jax.ShapeDtypeStruct((d, n), F32),
        scratch_shapes=[pltpu.VMEM((te, tm), F32), pltpu.VMEM((te, tm), F32),
                        pltpu.VMEM((te, tm), BF16), pltpu.VMEM((te, tm), BF16)],
        compiler_params=_params("parallel", "arbitrary"),
        name="peer_dense",
    )(ht, u_bf, vt_bf, s1m, e1n, s2m, e2, tau)


def _split_mod(mod, tokens_per_stream, per_token):
    if per_token:
        mod = jnp.repeat(mod, tokens_per_stream, axis=0)[None]
    else:
        mod = mod[:, None, :]
    return jnp.split(mod, 6, axis=-1)


def _encoder_layer(x, mod, h0_re, h0_im, lp, peer_w, *, tr, scan_steps, mixer_rows, per_token_mod, emit_v):
    (g_pre_tok, g_post_tok, g_pre_ch, g_post_ch, w_in, a_ws, a_bs, a_ln_g, a_ln_b, disc, s5_d,
     w_glu, b_glu, w_branch, w_out, peer_w_query, peer_sub_keys) = lp
    u_bf, vt_bf = peer_w
    nb, t, d = x.shape
    n = nb * t
    tiles_per_group = 1 if per_token_mod else t // tr
    sh_t, sc_t, gt_t, sh_c, sc_c, gt_c = _split_mod(mod, t, per_token_mod)
    x2 = x.reshape(n, d)

    h = _prenorm(x2, g_pre_tok, sc_t, sh_t, tr, tiles_per_group)
    proj = _matmul(h, w_in, 1024, 512, name="in_proj")
    proj3 = proj.reshape(nb, t, IN_WIDTH)
    mix = _mixer_a(proj3, a_ws, a_bs, a_ln_g, a_ln_b, mixer_rows, emit_v)
    ya = mix[0].reshape(n, A_WIDTH)
    v_rows = mix[1] if emit_v else None
    xb_tm = proj3[:, :, 2 * A_WIDTH:2 * A_WIDTH + B_WIDTH].transpose(1, 0, 2).reshape(n, B_WIDTH)
    ys_tm, h_re, h_im = _s5(xb_tm, nb, disc, s5_d, h0_re, h0_im, scan_steps)
    yb_tm = _glu(ys_tm, w_glu, b_glu, 1024, 512)
    yb = yb_tm.reshape(t, nb, B_WIDTH).transpose(1, 0, 2).reshape(n, B_WIDTH)
    merged = _branch_merge(ya, yb, w_branch, proj, 512, 512)
    m = _matmul(merged, w_out, 1024, 512, name="out_proj")

    x1, h2, h2t = _resid_prenorm(x2, m, g_post_tok, gt_t, g_pre_ch, sc_c, sh_c, tr, tiles_per_group)
    q = _matmul(h2, peer_w_query, 1024, 512, name="peer_query")
    topk = _peer_topk(q, peer_sub_keys, min(128, n))
    ft = _peer_dense(h2t, u_bf, vt_bf, topk, min(512, n), 512)
    y = _final_resid(x1, ft, g_post_ch, gt_c, tr, tiles_per_group)
    return (y.reshape(nb, t, d), h_re.reshape(nb, B_GROUPS, B_STATE), h_im.reshape(nb, B_GROUPS, B_STATE), v_rows)


def kernel(x_prompt, x_sample, c_prompt, c_sample, state_ssm_re, state_ssm_im, w_ada, b_ada, g_pre_tok,
           g_post_tok, g_pre_ch, g_post_ch, w_in, a_ws, a_bs, a_ln_g, a_ln_b, s5_lam_re, s5_lam_im,
           s5_log_dt, s5_b_re, s5_b_im, s5_c_re, s5_c_im, s5_d, w_glu, b_glu, w_branch, w_out,
           peer_w_query, peer_sub_keys, peer_u, peer_v):
    depth = w_ada.shape[0]
    nb_p, t_p, _ = x_prompt.shape
    nb_s, t_s, _ = x_sample.shape
    y_p, y_s = x_prompt, x_sample
    re_p, im_p, re_s, im_s, v_s = [], [], [], [], []
    pad = (-(nb_p + nb_s)) % 8
    for l in range(depth):
        c_all = jnp.concatenate([c_prompt, c_sample, jnp.zeros((pad, D_MODEL), F32)], axis=0)
        mod = _ada(c_all, w_ada[l], b_ada[l])
        disc = _s5_discretize(s5_lam_re[l], s5_lam_im[l], s5_log_dt[l], s5_b_re[l], s5_b_im[l],
                              s5_c_re[l], s5_c_im[l])
        lp = (g_pre_tok[l], g_post_tok[l], g_pre_ch[l], g_post_ch[l], w_in[l], a_ws[l], a_bs[l],
              a_ln_g[l], a_ln_b[l], disc, s5_d[l], w_glu[l], b_glu[l], w_branch[l], w_out[l],
              peer_w_query[l], peer_sub_keys[l])
        peer_w = (_to_bf16(peer_u[l], transpose=False), _to_bf16(peer_v[l], transpose=True))
        zeros = jnp.zeros((nb_p, S5_STATES), F32)
        y_p, hr, hi, _ = _encoder_layer(
            y_p, mod[:nb_p], zeros, zeros, lp, peer_w,
            tr=256, scan_steps=32, mixer_rows=A_CHUNK, per_token_mod=False, emit_v=False)
        re_p.append(hr)
        im_p.append(hi)
        y_s, hr, hi, v_rows = _encoder_layer(
            y_s, mod[nb_p:nb_p + nb_s], state_ssm_re[l].reshape(nb_s, S5_STATES),
            state_ssm_im[l].reshape(nb_s, S5_STATES), lp, peer_w,
            tr=nb_s * t_s, scan_steps=t_s, mixer_rows=t_s, per_token_mod=True, emit_v=True)
        re_s.append(hr)
        im_s.append(hi)
        v_s.append(v_rows)
    return (y_p, y_s, jnp.stack(re_p), jnp.stack(im_p), jnp.stack(re_s), jnp.stack(im_s), jnp.stack(v_s))
```

```python
import functools

import jax
import jax.numpy as jnp
from jax import lax
from jax.experimental import pallas as pl
from jax.experimental.pallas import tpu as pltpu

F32 = jnp.float32
BF16 = jnp.bfloat16

D_MODEL = 4096
CHUNK = 64
A_CHUNK = 128
A_GROUPS = 16
A_GROUP_DIM = 128
A_WIDTH = A_GROUPS * A_GROUP_DIM
B_GROUP_DIM = 16
B_GROUPS = 128
B_WIDTH = B_GROUPS * B_GROUP_DIM
B_STATE = 64
S5_STATES = B_GROUPS * B_STATE
S5_GROUPS_PER_BLOCK = 8
S5_BLOCKS = B_GROUPS // S5_GROUPS_PER_BLOCK
S5_BLOCK_STATES = S5_GROUPS_PER_BLOCK * B_STATE
IN_WIDTH = 2 * A_WIDTH + B_WIDTH + 2 * D_MODEL
PEER_HEADS = 8
PEER_SUBDIM = 128
PEER_NKEYS = 128
PEER_EXPERTS = PEER_NKEYS * PEER_NKEYS
PEER_TOPK = 16
PEER_UNRANKED = 1e9
PEER_TE = 512
PEER_GATE_ROWS = 32
PEER_K_CHUNK = 512
PEER_M_CHUNK = 256
EPS = 1e-6

V7X_LANES = 128
V7X_SUBLANES = 8
V7X_BF16_ROWS = 16
V7X_VMEM_BYTES = 64 * 1024 * 1024
VMEM_LIMIT_BYTES = 56 * 1024 * 1024


def _params(*semantics, flags=None):
    return pltpu.CompilerParams(dimension_semantics=semantics, vmem_limit_bytes=VMEM_LIMIT_BYTES, flags=flags)


def _rms(x):
    return x * lax.rsqrt(jnp.mean(x * x, axis=-1, keepdims=True) + EPS)


def _ada_kernel(c_ref, w_ref, b_ref, o_ref):
    s = jax.nn.silu(c_ref[...]).astype(BF16)
    o_ref[...] = jnp.dot(s, w_ref[...].astype(BF16), preferred_element_type=F32) + b_ref[...]


def _ada(c, w, b):
    rows, d = c.shape
    n = w.shape[1]
    tn = 512
    return pl.pallas_call(
        _ada_kernel,
        grid=(n // tn,),
        in_specs=[pl.BlockSpec((rows, d), lambda j: (0, 0)),
                  pl.BlockSpec((d, tn), lambda j: (0, j)),
                  pl.BlockSpec((1, tn), lambda j: (0, j))],
        out_specs=pl.BlockSpec((rows, tn), lambda j: (0, j)),
        out_shape=jax.ShapeDtypeStruct((rows, n), F32),
        compiler_params=_params("parallel"),
        name="ada_mod",
    )(c, w, b.reshape(1, n))


def _mod_spec(mod, tiles_per_group):
    return pl.BlockSpec((None,) + mod.shape[1:], lambda i: (i // tiles_per_group, 0, 0))


def _prenorm_kernel(x_ref, g_ref, sc_ref, sh_ref, o_ref):
    h = _rms(x_ref[...]) * g_ref[...] * (1.0 + sc_ref[...]) + sh_ref[...]
    o_ref[...] = h.astype(BF16)


def _prenorm(x, g, sc, sh, tr, tiles_per_group):
    n, d = x.shape
    return pl.pallas_call(
        _prenorm_kernel,
        grid=(n // tr,),
        in_specs=[pl.BlockSpec((tr, d), lambda i: (i, 0)),
                  pl.BlockSpec((1, d), lambda i: (0, 0)),
                  _mod_spec(sc, tiles_per_group),
                  _mod_spec(sh, tiles_per_group)],
        out_specs=pl.BlockSpec((tr, d), lambda i: (i, 0)),
        out_shape=jax.ShapeDtypeStruct((n, d), BF16),
        compiler_params=_params("parallel"),
        name="prenorm_tok",
    )(x, g.reshape(1, d), sc, sh)


def _resid_kernel(x_ref, m_ref, gpost_ref, gt_ref, gpre_ref, sc_ref, sh_ref, x1_ref, h_ref, ht_ref):
    x1 = x_ref[...] + gt_ref[...] * (_rms(m_ref[...]) * gpost_ref[...])
    x1_ref[...] = x1
    h = _rms(x1) * gpre_ref[...] * (1.0 + sc_ref[...]) + sh_ref[...]
    h_ref[...] = h.astype(BF16)
    ht_ref[...] = h.T.astype(BF16)


def _resid_prenorm(x, m, gpost, gt, gpre, sc, sh, tr, tiles_per_group):
    n, d = x.shape
    row = pl.BlockSpec((tr, d), lambda i: (i, 0))
    vec = pl.BlockSpec((1, d), lambda i: (0, 0))
    return pl.pallas_call(
        _resid_kernel,
        grid=(n // tr,),
        in_specs=[row, row, vec, _mod_spec(gt, tiles_per_group), vec,
                  _mod_spec(sc, tiles_per_group), _mod_spec(sh, tiles_per_group)],
        out_specs=[row, row, pl.BlockSpec((d, tr), lambda i: (0, i))],
        out_shape=[jax.ShapeDtypeStruct((n, d), F32),
                   jax.ShapeDtypeStruct((n, d), BF16),
                   jax.ShapeDtypeStruct((d, n), BF16)],
        compiler_params=_params("parallel"),
        name="resid_prenorm_ch",
    )(x, m, gpost.reshape(1, d), gt, gpre.reshape(1, d), sc, sh)


def _final_kernel(x_ref, ft_ref, gpost_ref, gt_ref, o_ref):
    f = ft_ref[...].T
    o_ref[...] = x_ref[...] + gt_ref[...] * (_rms(f) * gpost_ref[...])


def _final_resid(x, ft, gpost, gt, tr, tiles_per_group):
    n, d = x.shape
    row = pl.BlockSpec((tr, d), lambda i: (i, 0))
    return pl.pallas_call(
        _final_kernel,
        grid=(n // tr,),
        in_specs=[row, pl.BlockSpec((d, tr), lambda i: (0, i)),
                  pl.BlockSpec((1, d), lambda i: (0, 0)), _mod_spec(gt, tiles_per_group)],
        out_specs=row,
        out_shape=jax.ShapeDtypeStruct((n, d), F32),
        compiler_params=_params("parallel"),
        name="final_resid",
    )(x, ft, gpost.reshape(1, d), gt)


def _mm_kernel(a_ref, w_ref, o_ref):
    o_ref[...] = jnp.dot(a_ref[...].astype(BF16), w_ref[...].astype(BF16),
                         preferred_element_type=F32).astype(o_ref.dtype)


def _matmul(a, w, tm, tn, out_dtype=F32, name="matmul"):
    m, k = a.shape
    n = w.shape[1]
    tm = min(tm, m)
    return pl.pallas_call(
        _mm_kernel,
        grid=(m // tm, n // tn),
        in_specs=[pl.BlockSpec((tm, k), lambda i, j: (i, 0)),
                  pl.BlockSpec((k, tn), lambda i, j: (0, j))],
        out_specs=pl.BlockSpec((tm, tn), lambda i, j: (i, j)),
        out_shape=jax.ShapeDtypeStruct((m, n), out_dtype),
        compiler_params=_params("parallel", "parallel"),
        name=name,
    )(a, w)


def _glu_kernel(y_ref, w_ref, b_ref, yj_ref, o_ref):
    acc = jnp.dot(y_ref[...].astype(BF16), w_ref[...].astype(BF16), preferred_element_type=F32)
    o_ref[...] = (yj_ref[...] * jax.nn.sigmoid(acc + b_ref[...])).astype(BF16)


def _glu(y, w, b, tm, tn):
    m, k = y.shape
    n = w.shape[1]
    tm = min(tm, m)
    return pl.pallas_call(
        _glu_kernel,
        grid=(m // tm, n // tn),
        in_specs=[pl.BlockSpec((tm, k), lambda i, j: (i, 0)),
                  pl.BlockSpec((k, tn), lambda i, j: (0, j)),
                  pl.BlockSpec((1, tn), lambda i, j: (0, j)),
                  pl.BlockSpec((tm, tn), lambda i, j: (i, j))],
        out_specs=pl.BlockSpec((tm, tn), lambda i, j: (i, j)),
        out_shape=jax.ShapeDtypeStruct((m, n), BF16),
        compiler_params=_params("parallel", "parallel"),
        name="s5_glu",
    )(y, w, b.reshape(1, n), y)


def _branch_kernel(ya_ref, yb_ref, wa_ref, wb_ref, ga_ref, gb_ref, o_ref):
    a = jnp.dot(ya_ref[...], wa_ref[...].astype(BF16), preferred_element_type=F32)
    b = jnp.dot(yb_ref[...], wb_ref[...].astype(BF16), preferred_element_type=F32)
    o_ref[...] = (jax.nn.sigmoid(ga_ref[...]) * a + jax.nn.sigmoid(gb_ref[...]) * b).astype(BF16)


def _branch_merge(ya, yb, w_branch, proj, tm, tn):
    m = ya.shape[0]
    n = w_branch.shape[1]
    tm = min(tm, m)
    ga_blk = (2 * A_WIDTH + B_WIDTH) // tn
    gb_blk = ga_blk + D_MODEL // tn
    return pl.pallas_call(
        _branch_kernel,
        grid=(n // tn, m // tm),
        in_specs=[pl.BlockSpec((tm, A_WIDTH), lambda j, i: (i, 0)),
                  pl.BlockSpec((tm, B_WIDTH), lambda j, i: (i, 0)),
                  pl.BlockSpec((A_WIDTH, tn), lambda j, i: (0, j)),
                  pl.BlockSpec((B_WIDTH, tn), lambda j, i: (A_WIDTH // B_WIDTH, j)),
                  pl.BlockSpec((tm, tn), lambda j, i: (i, ga_blk + j)),
                  pl.BlockSpec((tm, tn), lambda j, i: (i, gb_blk + j))],
        out_specs=pl.BlockSpec((tm, tn), lambda j, i: (i, j)),
        out_shape=jax.ShapeDtypeStruct((m, n), BF16),
        compiler_params=_params("parallel", "parallel"),
        name="branch_merge",
    )(ya, yb, w_branch, w_branch, proj, proj)


def _mixer_a_kernel(za_ref, ws_ref, bias_ref, lng_ref, lnb_ref, ya_ref, *v_refs, rows):
    z = jax.nn.gelu(za_ref[...])
    u = z[:, :A_WIDTH]
    v = z[:, A_WIDTH:]
    d = v - jnp.mean(v, axis=-1, keepdims=True)
    vn = d * lax.rsqrt(jnp.mean(d * d, axis=-1, keepdims=True) + EPS) * lng_ref[...] + lnb_ref[...]
    if v_refs:
        v_refs[0][...] = vn
    if rows < A_CHUNK:
        vn = jnp.concatenate([vn, jnp.zeros((A_CHUNK - rows, A_WIDTH), F32)], axis=0)
    vb = vn.astype(BF16)
    blk_i = lax.broadcasted_iota(jnp.int32, (A_CHUNK, A_CHUNK), 0) // CHUNK
    blk_j = lax.broadcasted_iota(jnp.int32, (A_CHUNK, A_CHUNK), 1) // CHUNK
    causal = blk_i >= blk_j
    for g in range(A_GROUPS):
        cols = slice(g * A_GROUP_DIM, (g + 1) * A_GROUP_DIM)
        w = jnp.where(causal, ws_ref[g], 0.0).astype(BF16)
        mixed = jnp.dot(w, vb[:, cols], preferred_element_type=F32) + bias_ref[:, cols]
        ya_ref[:, cols] = (u[:, cols] * mixed[:rows]).astype(BF16)


def _mixer_a(proj3, ws, bs, ln_g, ln_b, rows, emit_v):
    nb, t, _ = proj3.shape
    bias = jnp.repeat(bs.T, A_GROUP_DIM, axis=1)
    out_shape = [jax.ShapeDtypeStruct((nb, t, A_WIDTH), BF16)]
    out_specs = [pl.BlockSpec((None, rows, A_WIDTH), lambda b, c: (b, c, 0))]
    if emit_v:
        out_shape.append(jax.ShapeDtypeStruct((nb, t, A_WIDTH), F32))
        out_specs.append(pl.BlockSpec((None, rows, A_WIDTH), lambda b, c: (b, c, 0)))
    return pl.pallas_call(
        functools.partial(_mixer_a_kernel, rows=rows),
        grid=(nb, t // rows),
        in_specs=[pl.BlockSpec((None, rows, 2 * A_WIDTH), lambda b, c: (b, c, 0)),
                  pl.BlockSpec((A_GROUPS, A_CHUNK, A_CHUNK), lambda b, c: (0, 0, 0)),
                  pl.BlockSpec((A_CHUNK, A_WIDTH), lambda b, c: (0, 0)),
                  pl.BlockSpec((1, A_WIDTH), lambda b, c: (0, 0)),
                  pl.BlockSpec((1, A_WIDTH), lambda b, c: (0, 0))],
        out_specs=out_specs,
        out_shape=out_shape,
        compiler_params=_params("parallel", "parallel"),
        name="mixer_a",
    )(proj3, ws, bias, ln_g.reshape(1, A_WIDTH), ln_b.reshape(1, A_WIDTH))


def _s5_discretize(lam_re, lam_im, log_dt, b_re, b_im, c_re, c_im):
    dt = jnp.exp(log_dt.astype(F32))[:, None]
    lr, li = lam_re.astype(F32), lam_im.astype(F32)
    mag = jnp.exp(lr * dt)
    ab_re, ab_im = mag * jnp.cos(li * dt), mag * jnp.sin(li * dt)
    den = lr * lr + li * li
    num_re = ab_re - 1.0
    coef_re = (num_re * lr + ab_im * li) / den
    coef_im = (ab_im * lr - num_re * li) / den
    br, bi = b_re.astype(F32), b_im.astype(F32)
    bb_re = coef_re[..., None] * br - coef_im[..., None] * bi
    bb_im = coef_re[..., None] * bi + coef_im[..., None] * br
    eye = jnp.eye(S5_GROUPS_PER_BLOCK, dtype=F32)

    def pack_in(bb):
        t = bb.reshape(S5_BLOCKS, S5_GROUPS_PER_BLOCK, B_STATE, B_GROUP_DIM).transpose(0, 1, 3, 2)
        return jnp.einsum("kgcp,gm->kgcmp", t, eye).reshape(S5_BLOCKS, V7X_LANES, S5_BLOCK_STATES)

    def pack_out(cc):
        t = cc.reshape(S5_BLOCKS, S5_GROUPS_PER_BLOCK, B_GROUP_DIM, B_STATE).transpose(0, 1, 3, 2)
        return jnp.einsum("kgpc,gm->kgpmc", t, eye).reshape(S5_BLOCKS, S5_BLOCK_STATES, V7X_LANES)

    w_in = jnp.concatenate([pack_in(bb_re), pack_in(bb_im)], axis=-1).astype(BF16)
    w_out = jnp.concatenate([pack_out(c_re.astype(F32)), -pack_out(c_im.astype(F32))], axis=1).astype(BF16)
    return ab_re.reshape(1, S5_STATES), ab_im.reshape(1, S5_STATES), w_in, w_out


def _s5_kernel(xb_ref, win_ref, wout_ref, are_ref, aim_ref, d_ref, h0r_ref, h0i_ref,
               ys_ref, hr_ref, hi_ref, sre, sim, *, nb):
    @pl.when(pl.program_id(0) == 0)
    def _():
        hr_ref[...] = h0r_ref[...]
        hi_ref[...] = h0i_ref[...]

    x = xb_ref[...]
    xbf = x.astype(BF16)
    for k in range(S5_BLOCKS):
        drive = jnp.dot(xbf[:, k * V7X_LANES:(k + 1) * V7X_LANES], win_ref[k], preferred_element_type=F32)
        sre[:, k * S5_BLOCK_STATES:(k + 1) * S5_BLOCK_STATES] = drive[:, :S5_BLOCK_STATES]
        sim[:, k * S5_BLOCK_STATES:(k + 1) * S5_BLOCK_STATES] = drive[:, S5_BLOCK_STATES:]

    a_re = are_ref[...]
    a_im = aim_ref[...]
    sub = lax.broadcasted_iota(jnp.int32, (V7X_SUBLANES, S5_STATES), 0)
    for i in range(x.shape[0] // V7X_SUBLANES):
        rows = slice(i * V7X_SUBLANES, (i + 1) * V7X_SUBLANES)
        prev = slice((i - 1) * V7X_SUBLANES, i * V7X_SUBLANES)
        c_re = hr_ref[...] if i == 0 else sre[prev, :]
        c_im = hi_ref[...] if i == 0 else sim[prev, :]
        d_re = sre[rows, :]
        d_im = sim[rows, :]
        o_re = o_im = None
        for s in range(V7X_SUBLANES // nb):
            if nb < V7X_SUBLANES:
                c_re = pltpu.roll(c_re, nb, 0)
                c_im = pltpu.roll(c_im, nb, 0)
            n_re = a_re * c_re - a_im * c_im + d_re
            n_im = a_re * c_im + a_im * c_re + d_im
            o_re = n_re if s == 0 else jnp.where(sub >= s * nb, n_re, o_re)
            o_im = n_im if s == 0 else jnp.where(sub >= s * nb, n_im, o_im)
            c_re, c_im = n_re, n_im
        sre[rows, :] = o_re
        sim[rows, :] = o_im
    last = slice(x.shape[0] - V7X_SUBLANES, x.shape[0])
    hr_ref[...] = sre[last, :]
    hi_ref[...] = sim[last, :]

    for k in range(S5_BLOCKS):
        st = slice(k * S5_BLOCK_STATES, (k + 1) * S5_BLOCK_STATES)
        ch = slice(k * V7X_LANES, (k + 1) * V7X_LANES)
        y = (jnp.dot(sre[:, st].astype(BF16), wout_ref[k, :S5_BLOCK_STATES, :], preferred_element_type=F32)
             + jnp.dot(sim[:, st].astype(BF16), wout_ref[k, S5_BLOCK_STATES:, :], preferred_element_type=F32)
             + d_ref[:, ch] * x[:, ch])
        ys_ref[:, ch] = jax.nn.gelu(y)


def _s5(xb_tm, nb, disc, d_skip, h0_re, h0_im, steps):
    rows = xb_tm.shape[0]
    a_re, a_im, w_in, w_out = disc
    const2 = lambda c: (0, 0)
    const3 = lambda c: (0, 0, 0)
    blk = steps * nb
    pad = jnp.zeros((V7X_SUBLANES - nb, S5_STATES), F32)
    state = pl.BlockSpec((V7X_SUBLANES, S5_STATES), const2)
    ys, h_re, h_im = pl.pallas_call(
        functools.partial(_s5_kernel, nb=nb),
        grid=(rows // blk,),
        in_specs=[pl.BlockSpec((blk, B_WIDTH), lambda c: (c, 0)),
                  pl.BlockSpec(w_in.shape, const3),
                  pl.BlockSpec(w_out.shape, const3),
                  pl.BlockSpec((1, S5_STATES), const2),
                  pl.BlockSpec((1, S5_STATES), const2),
                  pl.BlockSpec((1, B_WIDTH), const2),
                  state, state],
        out_specs=[pl.BlockSpec((blk, B_WIDTH), lambda c: (c, 0)), state, state],
        out_shape=[jax.ShapeDtypeStruct((rows, B_WIDTH), F32),
                   jax.ShapeDtypeStruct((V7X_SUBLANES, S5_STATES), F32),
                   jax.ShapeDtypeStruct((V7X_SUBLANES, S5_STATES), F32)],
        scratch_shapes=[pltpu.VMEM((blk, S5_STATES), F32),
                        pltpu.VMEM((blk, S5_STATES), F32)],
        compiler_params=_params("arbitrary"),
        name="s5_scan",
    )(xb_tm, w_in, w_out, a_re, a_im, d_skip.reshape(1, B_WIDTH),
      jnp.concatenate([pad, h0_re], axis=0), jnp.concatenate([pad, h0_im], axis=0))
    return ys, h_re[V7X_SUBLANES - nb:], h_im[V7X_SUBLANES - nb:]


def _cast_kernel(x_ref, o_ref):
    o_ref[...] = x_ref[...].astype(BF16)


def _cast_t_kernel(x_ref, o_ref):
    o_ref[...] = x_ref[...].T.astype(BF16)


def _to_bf16(x, transpose):
    r, c = x.shape
    tr, tc = PEER_TE, 2048
    if transpose:
        return pl.pallas_call(
            _cast_t_kernel, grid=(r // tr, c // tc),
            in_specs=[pl.BlockSpec((tr, tc), lambda i, j: (i, j))],
            out_specs=pl.BlockSpec((None, tc, tr), lambda i, j: (i, j, 0)),
            out_shape=jax.ShapeDtypeStruct((r // tr, c, tr), BF16),
            compiler_params=_params("parallel", "parallel"), name="cast_transpose")(x)
    return pl.pallas_call(
        _cast_kernel, grid=(r // tr, c // tc),
        in_specs=[pl.BlockSpec((tr, tc), lambda i, j: (i, j))],
        out_specs=pl.BlockSpec((tr, tc), lambda i, j: (i, j)),
        out_shape=jax.ShapeDtypeStruct((r, c), BF16),
        compiler_params=_params("parallel", "parallel"), name="cast_bf16")(x)


_STAIR_LIMIT = [PEER_TOPK // (a + 1) for a in range(PEER_TOPK)]


def _extract_top(work, order, count, unranked=None):
    big = jnp.int32(1 << 30)
    vals, picks = [], []
    rank = None if unranked is None else jnp.full(work.shape, unranked, F32)
    for r in range(count):
        m = jnp.max(work, axis=0, keepdims=True)
        pick = jnp.min(jnp.where(work == m, order, big), axis=0, keepdims=True)
        hit = order == pick
        work = jnp.where(hit, -jnp.inf, work)
        if rank is not None:
            rank = jnp.where(hit, float(r), rank)
        vals.append(m)
        picks.append(pick)
    return vals, picks, rank


def _stack_rows(rows, row_idx):
    out = jnp.broadcast_to(rows[0], row_idx.shape)
    for r in range(1, len(rows)):
        out = jnp.where(row_idx == r, rows[r], out)
    return out


def _topk_kernel(q_ref, keys_ref, e1n_ref, lim_ref, e2_ref, r2_ref, *, tm):
    key_idx = lax.broadcasted_iota(jnp.int32, (PEER_NKEYS, tm), 0)
    iota16 = lax.broadcasted_iota(jnp.int32, (PEER_TOPK, tm), 0)
    iota8 = iota16[:8]
    flat = [iota16] + [a * PEER_TOPK + iota8 for a in range(1, 8)] + [(iota8 + 8) * PEER_TOPK]
    flat = jnp.concatenate(flat, axis=0)
    for h in range(PEER_HEADS):
        scores, tops, ranks = [], [], []
        for s in range(2):
            col = (2 * h + s) * PEER_SUBDIM
            qs = q_ref[:, col:col + PEER_SUBDIM].astype(BF16)
            ks = keys_ref[h, s].astype(BF16)
            sc = lax.dot_general(ks, qs, (((1,), (1,)), ((), ())), preferred_element_type=F32)
            vals, _, rank = _extract_top(sc, key_idx, PEER_TOPK, unranked=(-1.0, PEER_UNRANKED)[s])
            scores.append(sc)
            tops.append(vals)
            ranks.append(rank)
        v1, v2 = tops
        v2_16 = _stack_rows(v2, iota16)
        v2_8 = v2_16[:8]
        cand = [v1[0] + v2_16]
        for a in range(1, 8):
            cand.append(jnp.where(iota8 < _STAIR_LIMIT[a], v1[a] + v2_8, -jnp.inf))
        cand.append(_stack_rows(v1[8:], iota8) + v2[0])
        cand = jnp.concatenate(cand, axis=0)
        cv, picks, _ = _extract_top(cand, flat, PEER_TOPK)
        z = jnp.ones_like(cv[0])
        for r in range(1, PEER_TOPK):
            z = z + jnp.exp(cv[r] - cv[0])
        limit = jnp.full((PEER_TOPK, tm), -1, jnp.int32)
        for pick in picks:
            row, col = pick >> 4, pick & (PEER_TOPK - 1)
            limit = jnp.maximum(limit, jnp.where(iota16 == row, col, -1))
        limit = limit.astype(F32)
        lim = jnp.full((PEER_NKEYS, tm), -1.0, F32)
        for a in range(PEER_TOPK):
            lim = jnp.where(ranks[0] == float(a), limit[a:a + 1, :], lim)
        e1n_ref[h] = jnp.exp(scores[0] - v1[0]) / z
        lim_ref[h] = lim
        e2_ref[h] = jnp.exp(scores[1] - v2[0])
        r2_ref[h] = ranks[1]


def _peer_topk(q, keys, tm):
    n = q.shape[0]
    per_key = pl.BlockSpec((PEER_HEADS, PEER_NKEYS, tm), lambda i: (0, 0, i))
    key_shape = jax.ShapeDtypeStruct((PEER_HEADS, PEER_NKEYS, n), F32)
    return pl.pallas_call(
        functools.partial(_topk_kernel, tm=tm),
        grid=(n // tm,),
        in_specs=[pl.BlockSpec((tm, q.shape[1]), lambda i: (i, 0)),
                  pl.BlockSpec(keys.shape, lambda i: (0, 0, 0, 0))],
        out_specs=[per_key] * 4,
        out_shape=[key_shape] * 4,
        compiler_params=_params("parallel"),
        name="peer_topk",
    )(q, keys)


def _peer_kernel(ht_ref, u_ref, vt_ref, e1n_ref, lim_ref, e2_ref, r2_ref,
                 o_ref, act_a, act_b, p_a, p_b, row_bc, *, te, nblocks):
    j = pl.program_id(1)
    slabs = te // PEER_NKEYS

    tm = o_ref.shape[1]
    tile = (V7X_SUBLANES, tm)

    @pl.when(j == 0)
    def _():
        o_ref[...] = jnp.zeros_like(o_ref)
        p_a[...] = jnp.zeros_like(p_a)
        act_b[...] = jnp.zeros_like(act_b)

    def step(act_new, act_old, p_new, p_old):
        first = jnp.clip(j - 1, 0, nblocks - 1) * slabs
        d = ht_ref.shape[0]

        def gate(il, r):
            i1 = first + il
            per_head = []
            for h in range(PEER_HEADS):
                if r == 0:
                    rows_i1 = [jnp.broadcast_to(ref[h, pl.ds(i1, 1), :], tile) for ref in (e1n_ref, lim_ref)]
                    for a, val in enumerate(rows_i1):
                        row_bc[a, il, h] = val
                else:
                    rows_i1 = [row_bc[a, il, h] for a in range(2)]
                per_head.append(rows_i1)
            for q in range(PEER_GATE_ROWS // V7X_BF16_ROWS):
                k0 = r * PEER_GATE_ROWS + q * V7X_BF16_ROWS
                halves = []
                for k8 in range(k0, k0 + V7X_BF16_ROWS, V7X_SUBLANES):
                    keys = slice(k8, k8 + V7X_SUBLANES)
                    w = jnp.zeros(tile, F32)
                    for h in range(PEER_HEADS):
                        e1, lim = per_head[h]
                        w = w + jnp.where(r2_ref[h, keys, :] <= lim, e2_ref[h, keys, :] * e1, 0.0)
                    halves.append(w)
                rows = slice(il * PEER_NKEYS + k0, il * PEER_NKEYS + k0 + V7X_BF16_ROWS)
                gates_q = jnp.concatenate(halves, axis=0)
                p_new[rows, :] = (gates_q * jax.nn.gelu(act_old[rows, :])).astype(BF16)

        def pre_act(kc):
            ks = slice(kc * PEER_K_CHUNK, (kc + 1) * PEER_K_CHUNK)
            part = jnp.dot(u_ref[:, ks], ht_ref[ks, :], preferred_element_type=F32)
            if kc == 0:
                act_new[...] = part
            else:
                act_new[...] += part

        def accumulate(mc):
            ms = slice(mc * PEER_M_CHUNK, (mc + 1) * PEER_M_CHUNK)
            o_ref[ms, :] += jnp.dot(vt_ref[ms, :], p_old[...], preferred_element_type=F32)

        gates = [(il, r) for il in range(slabs) for r in range(PEER_NKEYS // PEER_GATE_ROWS)]
        n_k, n_m = d // PEER_K_CHUNK, d // PEER_M_CHUNK
        mxu = []
        for kc in range(n_k):
            mxu.append((functools.partial(pre_act, kc), PEER_K_CHUNK * te))
            mxu += [(functools.partial(accumulate, kc * (n_m // n_k) + m), PEER_M_CHUNK * te)
                    for m in range(n_m // n_k)]
        total = sum(cost for _, cost in mxu)
        issued = done = 0
        for piece, cost in mxu:
            piece()
            issued += cost
            upto = (len(gates) * issued) // total
            for il, r in gates[done:upto]:
                gate(il, r)
            done = upto

    @pl.when(j % 2 == 0)
    def _():
        step(act_a, act_b, p_b, p_a)

    @pl.when(j % 2 == 1)
    def _():
        step(act_b, act_a, p_a, p_b)


def _peer_dense(ht, u_bf, vt_bf, topk, tm):
    d, n = ht.shape
    nblocks, _, te = vt_bf.shape
    per_tile = dict(pipeline_mode=pl.Buffered(1))
    per_key = pl.BlockSpec((PEER_HEADS, PEER_NKEYS, tm), lambda i, j: (0, 0, i), **per_tile)
    return pl.pallas_call(
        functools.partial(_peer_kernel, te=te, nblocks=nblocks),
        grid=(n // tm, nblocks + 2),
        in_specs=[pl.BlockSpec((d, tm), lambda i, j: (0, i), **per_tile),
                  pl.BlockSpec((te, d), lambda i, j: (jnp.minimum(j, nblocks - 1), 0)),
                  pl.BlockSpec((None, d, te), lambda i, j: (jnp.clip(j - 2, 0, nblocks - 1), 0, 0))]
                 + [per_key] * 4,
        out_specs=pl.BlockSpec((d, tm), lambda i, j: (0, i)),
        out_shape=jax.ShapeDtypeStruct((d, n), F32),
        scratch_shapes=[pltpu.VMEM((te, tm), F32), pltpu.VMEM((te, tm), F32),
                        pltpu.VMEM((te, tm), BF16), pltpu.VMEM((te, tm), BF16),
                        pltpu.VMEM((2, te // PEER_NKEYS, PEER_HEADS, V7X_SUBLANES, tm), F32)],
        compiler_params=_params("parallel", "arbitrary"),
        name="peer_dense",
    )(ht, u_bf, vt_bf, *topk)


def _split_mod(mod, tokens_per_stream, per_token):
    if per_token:
        mod = jnp.repeat(mod, tokens_per_stream, axis=0)[None]
    else:
        mod = mod[:, None, :]
    return jnp.split(mod, 6, axis=-1)


def _encoder_layer(x, mod, h0_re, h0_im, lp, peer_w, *, tr, scan_steps, mixer_rows, per_token_mod, emit_v):
    (g_pre_tok, g_post_tok, g_pre_ch, g_post_ch, w_in, a_ws, a_bs, a_ln_g, a_ln_b, disc, s5_d,
     w_glu, b_glu, w_branch, w_out, peer_w_query, peer_sub_keys) = lp
    u_bf, vt_bf = peer_w
    nb, t, d = x.shape
    n = nb * t
    tiles_per_group = 1 if per_token_mod else t // tr
    sh_t, sc_t, gt_t, sh_c, sc_c, gt_c = _split_mod(mod, t, per_token_mod)
    x2 = x.reshape(n, d)

    h = _prenorm(x2, g_pre_tok, sc_t, sh_t, tr, tiles_per_group)
    proj = _matmul(h, w_in, 1024, 512, name="in_proj")
    proj3 = proj.reshape(nb, t, IN_WIDTH)
    mix = _mixer_a(proj3, a_ws, a_bs, a_ln_g, a_ln_b, mixer_rows, emit_v)
    ya = mix[0].reshape(n, A_WIDTH)
    v_rows = mix[1] if emit_v else None
    xb_tm = proj3[:, :, 2 * A_WIDTH:2 * A_WIDTH + B_WIDTH].transpose(1, 0, 2).reshape(n, B_WIDTH)
    ys_tm, h_re, h_im = _s5(xb_tm, nb, disc, s5_d, h0_re, h0_im, scan_steps)
    yb_tm = _glu(ys_tm, w_glu, b_glu, 1024, 512)
    yb = yb_tm.reshape(t, nb, B_WIDTH).transpose(1, 0, 2).reshape(n, B_WIDTH)
    merged = _branch_merge(ya, yb, w_branch, proj, 512, 512)
    m = _matmul(merged, w_out, 1024, 512, name="out_proj")

    x1, h2, h2t = _resid_prenorm(x2, m, g_post_tok, gt_t, g_pre_ch, sc_c, sh_c, tr, tiles_per_group)
    q = _matmul(h2, peer_w_query, 1024, 512, name="peer_query")
    topk = _peer_topk(q, peer_sub_keys, min(128, n))
    ft = _peer_dense(h2t, u_bf, vt_bf, topk, min(512, n))
    y = _final_resid(x1, ft, g_post_ch, gt_c, tr, tiles_per_group)
    return (y.reshape(nb, t, d), h_re.reshape(nb, B_GROUPS, B_STATE), h_im.reshape(nb, B_GROUPS, B_STATE), v_rows)


def kernel(x_prompt, x_sample, c_prompt, c_sample, state_ssm_re, state_ssm_im, w_ada, b_ada, g_pre_tok,
           g_post_tok, g_pre_ch, g_post_ch, w_in, a_ws, a_bs, a_ln_g, a_ln_b, s5_lam_re, s5_lam_im,
           s5_log_dt, s5_b_re, s5_b_im, s5_c_re, s5_c_im, s5_d, w_glu, b_glu, w_branch, w_out,
           peer_w_query, peer_sub_keys, peer_u, peer_v):
    depth = w_ada.shape[0]
    nb_p, t_p, _ = x_prompt.shape
    nb_s, t_s, _ = x_sample.shape
    y_p, y_s = x_prompt, x_sample
    re_p, im_p, re_s, im_s, v_s = [], [], [], [], []
    pad = (-(nb_p + nb_s)) % 8
    for l in range(depth):
        c_all = jnp.concatenate([c_prompt, c_sample, jnp.zeros((pad, D_MODEL), F32)], axis=0)
        mod = _ada(c_all, w_ada[l], b_ada[l])
        disc = _s5_discretize(s5_lam_re[l], s5_lam_im[l], s5_log_dt[l], s5_b_re[l], s5_b_im[l],
                              s5_c_re[l], s5_c_im[l])
        lp = (g_pre_tok[l], g_post_tok[l], g_pre_ch[l], g_post_ch[l], w_in[l], a_ws[l], a_bs[l],
              a_ln_g[l], a_ln_b[l], disc, s5_d[l], w_glu[l], b_glu[l], w_branch[l], w_out[l],
              peer_w_query[l], peer_sub_keys[l])
        peer_w = (_to_bf16(peer_u[l], transpose=False), _to_bf16(peer_v[l], transpose=True))
        zeros = jnp.zeros((nb_p, S5_STATES), F32)
        y_p, hr, hi, _ = _encoder_layer(
            y_p, mod[:nb_p], zeros, zeros, lp, peer_w,
            tr=256, scan_steps=32, mixer_rows=A_CHUNK, per_token_mod=False, emit_v=False)
        re_p.append(hr)
        im_p.append(hi)
        y_s, hr, hi, v_rows = _encoder_layer(
            y_s, mod[nb_p:nb_p + nb_s], state_ssm_re[l].reshape(nb_s, S5_STATES),
            state_ssm_im[l].reshape(nb_s, S5_STATES), lp, peer_w,
            tr=nb_s * t_s, scan_steps=t_s, mixer_rows=t_s, per_token_mod=True, emit_v=True)
        re_s.append(hr)
        im_s.append(hi)
        v_s.append(v_rows)
    return (y_p, y_s, jnp.stack(re_p), jnp.stack(im_p), jnp.stack(re_s), jnp.stack(im_s), jnp.stack(v_s))
```

```python
import functools

import jax
import jax.numpy as jnp
from jax import lax
from jax.experimental import pallas as pl
from jax.experimental.pallas import tpu as pltpu

F32 = jnp.float32
BF16 = jnp.bfloat16

D_MODEL = 4096
CHUNK = 64
A_CHUNK = 128
A_GROUPS = 16
A_GROUP_DIM = 128
A_WIDTH = A_GROUPS * A_GROUP_DIM
B_GROUP_DIM = 16
B_GROUPS = 128
B_WIDTH = B_GROUPS * B_GROUP_DIM
B_STATE = 64
S5_STATES = B_GROUPS * B_STATE
S5_GROUPS_PER_BLOCK = 8
S5_BLOCKS = B_GROUPS // S5_GROUPS_PER_BLOCK
S5_BLOCK_STATES = S5_GROUPS_PER_BLOCK * B_STATE
IN_WIDTH = 2 * A_WIDTH + B_WIDTH + 2 * D_MODEL
PEER_HEADS = 8
PEER_SUBDIM = 128
PEER_NKEYS = 128
PEER_EXPERTS = PEER_NKEYS * PEER_NKEYS
PEER_TOPK = 16
PEER_UNRANKED = 1e9
PEER_TE = 512
PEER_GATE_ROWS = 32
PEER_K_CHUNK = 512
PEER_M_CHUNK = 256
EPS = 1e-6

V7X_LANES = 128
V7X_SUBLANES = 8
V7X_BF16_ROWS = 16
V7X_VMEM_BYTES = 64 * 1024 * 1024
VMEM_LIMIT_BYTES = 56 * 1024 * 1024


def _params(*semantics, flags=None):
    return pltpu.CompilerParams(dimension_semantics=semantics, vmem_limit_bytes=VMEM_LIMIT_BYTES, flags=flags)


def _rms(x):
    return x * lax.rsqrt(jnp.mean(x * x, axis=-1, keepdims=True) + EPS)


def _ada_kernel(c_ref, w_ref, b_ref, o_ref):
    s = jax.nn.silu(c_ref[...]).astype(BF16)
    o_ref[...] = jnp.dot(s, w_ref[...].astype(BF16), preferred_element_type=F32) + b_ref[...]


def _ada(c, w, b):
    rows, d = c.shape
    n = w.shape[1]
    tn = 512
    return pl.pallas_call(
        _ada_kernel,
        grid=(n // tn,),
        in_specs=[pl.BlockSpec((rows, d), lambda j: (0, 0)),
                  pl.BlockSpec((d, tn), lambda j: (0, j)),
                  pl.BlockSpec((1, tn), lambda j: (0, j))],
        out_specs=pl.BlockSpec((rows, tn), lambda j: (0, j)),
        out_shape=jax.ShapeDtypeStruct((rows, n), F32),
        compiler_params=_params("parallel"),
        name="ada_mod",
    )(c, w, b.reshape(1, n))


def _mod_spec(mod, tiles_per_group):
    return pl.BlockSpec((None,) + mod.shape[1:], lambda i: (i // tiles_per_group, 0, 0))


def _prenorm_kernel(x_ref, g_ref, sc_ref, sh_ref, o_ref):
    h = _rms(x_ref[...]) * g_ref[...] * (1.0 + sc_ref[...]) + sh_ref[...]
    o_ref[...] = h.astype(BF16)


def _prenorm(x, g, sc, sh, tr, tiles_per_group):
    n, d = x.shape
    return pl.pallas_call(
        _prenorm_kernel,
        grid=(n // tr,),
        in_specs=[pl.BlockSpec((tr, d), lambda i: (i, 0)),
                  pl.BlockSpec((1, d), lambda i: (0, 0)),
                  _mod_spec(sc, tiles_per_group),
                  _mod_spec(sh, tiles_per_group)],
        out_specs=pl.BlockSpec((tr, d), lambda i: (i, 0)),
        out_shape=jax.ShapeDtypeStruct((n, d), BF16),
        compiler_params=_params("parallel"),
        name="prenorm_tok",
    )(x, g.reshape(1, d), sc, sh)


def _resid_kernel(x_ref, m_ref, gpost_ref, gt_ref, gpre_ref, sc_ref, sh_ref, x1_ref, h_ref, ht_ref):
    x1 = x_ref[...] + gt_ref[...] * (_rms(m_ref[...]) * gpost_ref[...])
    x1_ref[...] = x1
    h = _rms(x1) * gpre_ref[...] * (1.0 + sc_ref[...]) + sh_ref[...]
    h_ref[...] = h.astype(BF16)
    ht_ref[...] = h.T.astype(BF16)


def _resid_prenorm(x, m, gpost, gt, gpre, sc, sh, tr, tiles_per_group):
    n, d = x.shape
    row = pl.BlockSpec((tr, d), lambda i: (i, 0))
    vec = pl.BlockSpec((1, d), lambda i: (0, 0))
    return pl.pallas_call(
        _resid_kernel,
        grid=(n // tr,),
        in_specs=[row, row, vec, _mod_spec(gt, tiles_per_group), vec,
                  _mod_spec(sc, tiles_per_group), _mod_spec(sh, tiles_per_group)],
        out_specs=[row, row, pl.BlockSpec((d, tr), lambda i: (0, i))],
        out_shape=[jax.ShapeDtypeStruct((n, d), F32),
                   jax.ShapeDtypeStruct((n, d), BF16),
                   jax.ShapeDtypeStruct((d, n), BF16)],
        compiler_params=_params("parallel"),
        name="resid_prenorm_ch",
    )(x, m, gpost.reshape(1, d), gt, gpre.reshape(1, d), sc, sh)


def _final_kernel(x_ref, ft_ref, gpost_ref, gt_ref, o_ref):
    f = ft_ref[...].T
    o_ref[...] = x_ref[...] + gt_ref[...] * (_rms(f) * gpost_ref[...])


def _final_resid(x, ft, gpost, gt, tr, tiles_per_group):
    n, d = x.shape
    row = pl.BlockSpec((tr, d), lambda i: (i, 0))
    return pl.pallas_call(
        _final_kernel,
        grid=(n // tr,),
        in_specs=[row, pl.BlockSpec((d, tr), lambda i: (0, i)),
                  pl.BlockSpec((1, d), lambda i: (0, 0)), _mod_spec(gt, tiles_per_group)],
        out_specs=row,
        out_shape=jax.ShapeDtypeStruct((n, d), F32),
        compiler_params=_params("parallel"),
        name="final_resid",
    )(x, ft, gpost.reshape(1, d), gt)


def _mm_kernel(a_ref, w_ref, o_ref):
    o_ref[...] = jnp.dot(a_ref[...].astype(BF16), w_ref[...].astype(BF16),
                         preferred_element_type=F32).astype(o_ref.dtype)


def _matmul(a, w, tm, tn, out_dtype=F32, name="matmul"):
    m, k = a.shape
    n = w.shape[1]
    tm = min(tm, m)
    return pl.pallas_call(
        _mm_kernel,
        grid=(m // tm, n // tn),
        in_specs=[pl.BlockSpec((tm, k), lambda i, j: (i, 0)),
                  pl.BlockSpec((k, tn), lambda i, j: (0, j))],
        out_specs=pl.BlockSpec((tm, tn), lambda i, j: (i, j)),
        out_shape=jax.ShapeDtypeStruct((m, n), out_dtype),
        compiler_params=_params("parallel", "parallel"),
        name=name,
    )(a, w)


def _glu_kernel(y_ref, w_ref, b_ref, yj_ref, o_ref):
    acc = jnp.dot(y_ref[...].astype(BF16), w_ref[...].astype(BF16), preferred_element_type=F32)
    o_ref[...] = (yj_ref[...] * jax.nn.sigmoid(acc + b_ref[...])).astype(BF16)


def _glu(y, w, b, tm, tn):
    m, k = y.shape
    n = w.shape[1]
    tm = min(tm, m)
    return pl.pallas_call(
        _glu_kernel,
        grid=(m // tm, n // tn),
        in_specs=[pl.BlockSpec((tm, k), lambda i, j: (i, 0)),
                  pl.BlockSpec((k, tn), lambda i, j: (0, j)),
                  pl.BlockSpec((1, tn), lambda i, j: (0, j)),
                  pl.BlockSpec((tm, tn), lambda i, j: (i, j))],
        out_specs=pl.BlockSpec((tm, tn), lambda i, j: (i, j)),
        out_shape=jax.ShapeDtypeStruct((m, n), BF16),
        compiler_params=_params("parallel", "parallel"),
        name="s5_glu",
    )(y, w, b.reshape(1, n), y)


def _branch_kernel(ya_ref, yb_ref, wa_ref, wb_ref, ga_ref, gb_ref, o_ref):
    a = jnp.dot(ya_ref[...], wa_ref[...].astype(BF16), preferred_element_type=F32)
    b = jnp.dot(yb_ref[...], wb_ref[...].astype(BF16), preferred_element_type=F32)
    o_ref[...] = (jax.nn.sigmoid(ga_ref[...]) * a + jax.nn.sigmoid(gb_ref[...]) * b).astype(BF16)


def _branch_merge(ya, yb, w_branch, proj, tm, tn):
    m = ya.shape[0]
    n = w_branch.shape[1]
    tm = min(tm, m)
    ga_blk = (2 * A_WIDTH + B_WIDTH) // tn
    gb_blk = ga_blk + D_MODEL // tn
    return pl.pallas_call(
        _branch_kernel,
        grid=(n // tn, m // tm),
        in_specs=[pl.BlockSpec((tm, A_WIDTH), lambda j, i: (i, 0)),
                  pl.BlockSpec((tm, B_WIDTH), lambda j, i: (i, 0)),
                  pl.BlockSpec((A_WIDTH, tn), lambda j, i: (0, j)),
                  pl.BlockSpec((B_WIDTH, tn), lambda j, i: (A_WIDTH // B_WIDTH, j)),
                  pl.BlockSpec((tm, tn), lambda j, i: (i, ga_blk + j)),
                  pl.BlockSpec((tm, tn), lambda j, i: (i, gb_blk + j))],
        out_specs=pl.BlockSpec((tm, tn), lambda j, i: (i, j)),
        out_shape=jax.ShapeDtypeStruct((m, n), BF16),
        compiler_params=_params("parallel", "parallel"),
        name="branch_merge",
    )(ya, yb, w_branch, w_branch, proj, proj)


def _mixer_a_kernel(za_ref, ws_ref, bias_ref, lng_ref, lnb_ref, ya_ref, *v_refs, rows):
    z = jax.nn.gelu(za_ref[...])
    u = z[:, :A_WIDTH]
    v = z[:, A_WIDTH:]
    d = v - jnp.mean(v, axis=-1, keepdims=True)
    vn = d * lax.rsqrt(jnp.mean(d * d, axis=-1, keepdims=True) + EPS) * lng_ref[...] + lnb_ref[...]
    if v_refs:
        v_refs[0][...] = vn
    if rows < A_CHUNK:
        vn = jnp.concatenate([vn, jnp.zeros((A_CHUNK - rows, A_WIDTH), F32)], axis=0)
    vb = vn.astype(BF16)
    blk_i = lax.broadcasted_iota(jnp.int32, (A_CHUNK, A_CHUNK), 0) // CHUNK
    blk_j = lax.broadcasted_iota(jnp.int32, (A_CHUNK, A_CHUNK), 1) // CHUNK
    causal = blk_i >= blk_j
    for g in range(A_GROUPS):
        cols = slice(g * A_GROUP_DIM, (g + 1) * A_GROUP_DIM)
        w = jnp.where(causal, ws_ref[g], 0.0).astype(BF16)
        mixed = jnp.dot(w, vb[:, cols], preferred_element_type=F32) + bias_ref[:, cols]
        ya_ref[:, cols] = (u[:, cols] * mixed[:rows]).astype(BF16)


def _mixer_a(proj3, ws, bs, ln_g, ln_b, rows, emit_v):
    nb, t, _ = proj3.shape
    bias = jnp.repeat(bs.T, A_GROUP_DIM, axis=1)
    out_shape = [jax.ShapeDtypeStruct((nb, t, A_WIDTH), BF16)]
    out_specs = [pl.BlockSpec((None, rows, A_WIDTH), lambda b, c: (b, c, 0))]
    if emit_v:
        out_shape.append(jax.ShapeDtypeStruct((nb, t, A_WIDTH), F32))
        out_specs.append(pl.BlockSpec((None, rows, A_WIDTH), lambda b, c: (b, c, 0)))
    return pl.pallas_call(
        functools.partial(_mixer_a_kernel, rows=rows),
        grid=(nb, t // rows),
        in_specs=[pl.BlockSpec((None, rows, 2 * A_WIDTH), lambda b, c: (b, c, 0)),
                  pl.BlockSpec((A_GROUPS, A_CHUNK, A_CHUNK), lambda b, c: (0, 0, 0)),
                  pl.BlockSpec((A_CHUNK, A_WIDTH), lambda b, c: (0, 0)),
                  pl.BlockSpec((1, A_WIDTH), lambda b, c: (0, 0)),
                  pl.BlockSpec((1, A_WIDTH), lambda b, c: (0, 0))],
        out_specs=out_specs,
        out_shape=out_shape,
        compiler_params=_params("parallel", "parallel"),
        name="mixer_a",
    )(proj3, ws, bias, ln_g.reshape(1, A_WIDTH), ln_b.reshape(1, A_WIDTH))


def _s5_discretize(lam_re, lam_im, log_dt, b_re, b_im, c_re, c_im):
    dt = jnp.exp(log_dt.astype(F32))[:, None]
    lr, li = lam_re.astype(F32), lam_im.astype(F32)
    mag = jnp.exp(lr * dt)
    ab_re, ab_im = mag * jnp.cos(li * dt), mag * jnp.sin(li * dt)
    den = lr * lr + li * li
    num_re = ab_re - 1.0
    coef_re = (num_re * lr + ab_im * li) / den
    coef_im = (ab_im * lr - num_re * li) / den
    br, bi = b_re.astype(F32), b_im.astype(F32)
    bb_re = coef_re[..., None] * br - coef_im[..., None] * bi
    bb_im = coef_re[..., None] * bi + coef_im[..., None] * br
    eye = jnp.eye(S5_GROUPS_PER_BLOCK, dtype=F32)

    def pack_in(bb):
        t = bb.reshape(S5_BLOCKS, S5_GROUPS_PER_BLOCK, B_STATE, B_GROUP_DIM).transpose(0, 1, 3, 2)
        return jnp.einsum("kgcp,gm->kgcmp", t, eye).reshape(S5_BLOCKS, V7X_LANES, S5_BLOCK_STATES)

    def pack_out(cc):
        t = cc.reshape(S5_BLOCKS, S5_GROUPS_PER_BLOCK, B_GROUP_DIM, B_STATE).transpose(0, 1, 3, 2)
        return jnp.einsum("kgpc,gm->kgpmc", t, eye).reshape(S5_BLOCKS, S5_BLOCK_STATES, V7X_LANES)

    w_in = jnp.concatenate([pack_in(bb_re), pack_in(bb_im)], axis=-1).astype(BF16)
    w_out = jnp.concatenate([pack_out(c_re.astype(F32)), -pack_out(c_im.astype(F32))], axis=1).astype(BF16)
    return ab_re.reshape(1, S5_STATES), ab_im.reshape(1, S5_STATES), w_in, w_out


def _s5_kernel(xb_ref, win_ref, wout_ref, are_ref, aim_ref, d_ref, h0r_ref, h0i_ref,
               ys_ref, hr_ref, hi_ref, sre, sim, *, nb):
    @pl.when(pl.program_id(0) == 0)
    def _():
        hr_ref[...] = h0r_ref[...]
        hi_ref[...] = h0i_ref[...]

    x = xb_ref[...]
    xbf = x.astype(BF16)
    for k in range(S5_BLOCKS):
        drive = jnp.dot(xbf[:, k * V7X_LANES:(k + 1) * V7X_LANES], win_ref[k], preferred_element_type=F32)
        sre[:, k * S5_BLOCK_STATES:(k + 1) * S5_BLOCK_STATES] = drive[:, :S5_BLOCK_STATES]
        sim[:, k * S5_BLOCK_STATES:(k + 1) * S5_BLOCK_STATES] = drive[:, S5_BLOCK_STATES:]

    a_re = are_ref[...]
    a_im = aim_ref[...]
    sub = lax.broadcasted_iota(jnp.int32, (V7X_SUBLANES, S5_STATES), 0)
    for i in range(x.shape[0] // V7X_SUBLANES):
        rows = slice(i * V7X_SUBLANES, (i + 1) * V7X_SUBLANES)
        prev = slice((i - 1) * V7X_SUBLANES, i * V7X_SUBLANES)
        c_re = hr_ref[...] if i == 0 else sre[prev, :]
        c_im = hi_ref[...] if i == 0 else sim[prev, :]
        d_re = sre[rows, :]
        d_im = sim[rows, :]
        o_re = o_im = None
        for s in range(V7X_SUBLANES // nb):
            if nb < V7X_SUBLANES:
                c_re = pltpu.roll(c_re, nb, 0)
                c_im = pltpu.roll(c_im, nb, 0)
            n_re = a_re * c_re - a_im * c_im + d_re
            n_im = a_re * c_im + a_im * c_re + d_im
            o_re = n_re if s == 0 else jnp.where(sub >= s * nb, n_re, o_re)
            o_im = n_im if s == 0 else jnp.where(sub >= s * nb, n_im, o_im)
            c_re, c_im = n_re, n_im
        sre[rows, :] = o_re
        sim[rows, :] = o_im
    last = slice(x.shape[0] - V7X_SUBLANES, x.shape[0])
    hr_ref[...] = sre[last, :]
    hi_ref[...] = sim[last, :]

    for k in range(S5_BLOCKS):
        st = slice(k * S5_BLOCK_STATES, (k + 1) * S5_BLOCK_STATES)
        ch = slice(k * V7X_LANES, (k + 1) * V7X_LANES)
        y = (jnp.dot(sre[:, st].astype(BF16), wout_ref[k, :S5_BLOCK_STATES, :], preferred_element_type=F32)
             + jnp.dot(sim[:, st].astype(BF16), wout_ref[k, S5_BLOCK_STATES:, :], preferred_element_type=F32)
             + d_ref[:, ch] * x[:, ch])
        ys_ref[:, ch] = jax.nn.gelu(y)


def _s5(xb_tm, nb, disc, d_skip, h0_re, h0_im, steps):
    rows = xb_tm.shape[0]
    a_re, a_im, w_in, w_out = disc
    const2 = lambda c: (0, 0)
    const3 = lambda c: (0, 0, 0)
    blk = steps * nb
    pad = jnp.zeros((V7X_SUBLANES - nb, S5_STATES), F32)
    state = pl.BlockSpec((V7X_SUBLANES, S5_STATES), const2)
    ys, h_re, h_im = pl.pallas_call(
        functools.partial(_s5_kernel, nb=nb),
        grid=(rows // blk,),
        in_specs=[pl.BlockSpec((blk, B_WIDTH), lambda c: (c, 0)),
                  pl.BlockSpec(w_in.shape, const3),
                  pl.BlockSpec(w_out.shape, const3),
                  pl.BlockSpec((1, S5_STATES), const2),
                  pl.BlockSpec((1, S5_STATES), const2),
                  pl.BlockSpec((1, B_WIDTH), const2),
                  state, state],
        out_specs=[pl.BlockSpec((blk, B_WIDTH), lambda c: (c, 0)), state, state],
        out_shape=[jax.ShapeDtypeStruct((rows, B_WIDTH), F32),
                   jax.ShapeDtypeStruct((V7X_SUBLANES, S5_STATES), F32),
                   jax.ShapeDtypeStruct((V7X_SUBLANES, S5_STATES), F32)],
        scratch_shapes=[pltpu.VMEM((blk, S5_STATES), F32),
                        pltpu.VMEM((blk, S5_STATES), F32)],
        compiler_params=_params("arbitrary"),
        name="s5_scan",
    )(xb_tm, w_in, w_out, a_re, a_im, d_skip.reshape(1, B_WIDTH),
      jnp.concatenate([pad, h0_re], axis=0), jnp.concatenate([pad, h0_im], axis=0))
    return ys, h_re[V7X_SUBLANES - nb:], h_im[V7X_SUBLANES - nb:]


def _cast_kernel(x_ref, o_ref):
    o_ref[...] = x_ref[...].astype(BF16)


def _cast_t_kernel(x_ref, o_ref):
    o_ref[...] = x_ref[...].T.astype(BF16)


def _to_bf16(x, transpose):
    r, c = x.shape
    tr, tc = PEER_TE, 2048
    if transpose:
        return pl.pallas_call(
            _cast_t_kernel, grid=(r // tr, c // tc),
            in_specs=[pl.BlockSpec((tr, tc), lambda i, j: (i, j))],
            out_specs=pl.BlockSpec((None, tc, tr), lambda i, j: (i, j, 0)),
            out_shape=jax.ShapeDtypeStruct((r // tr, c, tr), BF16),
            compiler_params=_params("parallel", "parallel"), name="cast_transpose")(x)
    return pl.pallas_call(
        _cast_kernel, grid=(r // tr, c // tc),
        in_specs=[pl.BlockSpec((tr, tc), lambda i, j: (i, j))],
        out_specs=pl.BlockSpec((tr, tc), lambda i, j: (i, j)),
        out_shape=jax.ShapeDtypeStruct((r, c), BF16),
        compiler_params=_params("parallel", "parallel"), name="cast_bf16")(x)


_STAIR_LIMIT = [PEER_TOPK // (a + 1) for a in range(PEER_TOPK)]


def _extract_top(work, order, count, unranked=None):
    big = jnp.int32(1 << 30)
    vals, picks = [], []
    rank = None if unranked is None else jnp.full(work.shape, unranked, F32)
    for r in range(count):
        m = jnp.max(work, axis=0, keepdims=True)
        pick = jnp.min(jnp.where(work == m, order, big), axis=0, keepdims=True)
        hit = order == pick
        work = jnp.where(hit, -jnp.inf, work)
        if rank is not None:
            rank = jnp.where(hit, float(r), rank)
        vals.append(m)
        picks.append(pick)
    return vals, picks, rank


def _stack_rows(rows, row_idx):
    out = jnp.broadcast_to(rows[0], row_idx.shape)
    for r in range(1, len(rows)):
        out = jnp.where(row_idx == r, rows[r], out)
    return out


def _topk_kernel(q_ref, keys_ref, e1n_ref, lim_ref, e2_ref, r2_ref, *, tm):
    key_idx = lax.broadcasted_iota(jnp.int32, (PEER_NKEYS, tm), 0)
    iota16 = lax.broadcasted_iota(jnp.int32, (PEER_TOPK, tm), 0)
    iota8 = iota16[:8]
    flat = [iota16] + [a * PEER_TOPK + iota8 for a in range(1, 8)] + [(iota8 + 8) * PEER_TOPK]
    flat = jnp.concatenate(flat, axis=0)
    for h in range(PEER_HEADS):
        scores, tops, ranks = [], [], []
        for s in range(2):
            col = (2 * h + s) * PEER_SUBDIM
            qs = q_ref[:, col:col + PEER_SUBDIM].astype(BF16)
            ks = keys_ref[h, s].astype(BF16)
            sc = lax.dot_general(ks, qs, (((1,), (1,)), ((), ())), preferred_element_type=F32)
            vals, _, rank = _extract_top(sc, key_idx, PEER_TOPK, unranked=(-1.0, PEER_UNRANKED)[s])
            scores.append(sc)
            tops.append(vals)
            ranks.append(rank)
        v1, v2 = tops
        v2_16 = _stack_rows(v2, iota16)
        v2_8 = v2_16[:8]
        cand = [v1[0] + v2_16]
        for a in range(1, 8):
            cand.append(jnp.where(iota8 < _STAIR_LIMIT[a], v1[a] + v2_8, -jnp.inf))
        cand.append(_stack_rows(v1[8:], iota8) + v2[0])
        cand = jnp.concatenate(cand, axis=0)
        cv, picks, _ = _extract_top(cand, flat, PEER_TOPK)
        z = jnp.ones_like(cv[0])
        for r in range(1, PEER_TOPK):
            z = z + jnp.exp(cv[r] - cv[0])
        limit = jnp.full((PEER_TOPK, tm), -1, jnp.int32)
        for pick in picks:
            row, col = pick >> 4, pick & (PEER_TOPK - 1)
            limit = jnp.maximum(limit, jnp.where(iota16 == row, col, -1))
        limit = limit.astype(F32)
        lim = jnp.full((PEER_NKEYS, tm), -1.0, F32)
        for a in range(PEER_TOPK):
            lim = jnp.where(ranks[0] == float(a), limit[a:a + 1, :], lim)
        e1n_ref[h] = jnp.exp(scores[0] - v1[0]) / z
        lim_ref[h] = lim
        e2_ref[h] = jnp.exp(scores[1] - v2[0]).astype(BF16)
        r2_ref[h] = ranks[1].astype(BF16)


def _peer_topk(q, keys, tm):
    n = q.shape[0]
    per_key = pl.BlockSpec((PEER_HEADS, PEER_NKEYS, tm), lambda i: (0, 0, i))
    shape = (PEER_HEADS, PEER_NKEYS, n)
    return pl.pallas_call(
        functools.partial(_topk_kernel, tm=tm),
        grid=(n // tm,),
        in_specs=[pl.BlockSpec((tm, q.shape[1]), lambda i: (i, 0)),
                  pl.BlockSpec(keys.shape, lambda i: (0, 0, 0, 0))],
        out_specs=[per_key] * 4,
        out_shape=[jax.ShapeDtypeStruct(shape, F32), jax.ShapeDtypeStruct(shape, F32),
                   jax.ShapeDtypeStruct(shape, BF16), jax.ShapeDtypeStruct(shape, BF16)],
        compiler_params=_params("parallel"),
        name="peer_topk",
    )(q, keys)


def _peer_kernel(ht_ref, u_ref, vt_ref, e1n_ref, lim_ref, e2_ref, r2_ref,
                 o_ref, act_a, act_b, p_a, p_b, row_bc, *, te, nblocks):
    j = pl.program_id(1)
    slabs = te // PEER_NKEYS

    tm = o_ref.shape[1]
    tile = (V7X_BF16_ROWS, tm)

    @pl.when(j == 0)
    def _():
        o_ref[...] = jnp.zeros_like(o_ref)
        p_a[...] = jnp.zeros_like(p_a)
        act_b[...] = jnp.zeros_like(act_b)

    def step(act_new, act_old, p_new, p_old):
        first = jnp.clip(j - 1, 0, nblocks - 1) * slabs
        d = ht_ref.shape[0]

        def gate(il, r):
            i1 = first + il
            per_head = []
            for h in range(PEER_HEADS):
                if r == 0:
                    rows_i1 = [jnp.broadcast_to(ref[h, pl.ds(i1, 1), :], tile).astype(BF16)
                               for ref in (e1n_ref, lim_ref)]
                    for a, val in enumerate(rows_i1):
                        row_bc[a, il, h] = val
                else:
                    rows_i1 = [row_bc[a, il, h] for a in range(2)]
                per_head.append(rows_i1)
            for q in range(PEER_GATE_ROWS // V7X_BF16_ROWS):
                k0 = r * PEER_GATE_ROWS + q * V7X_BF16_ROWS
                keys = slice(k0, k0 + V7X_BF16_ROWS)
                w = jnp.zeros(tile, BF16)
                for h in range(PEER_HEADS):
                    e1, lim = per_head[h]
                    w = w + jnp.where(r2_ref[h, keys, :] <= lim, e2_ref[h, keys, :] * e1, jnp.zeros((), BF16))
                rows = slice(il * PEER_NKEYS + k0, il * PEER_NKEYS + k0 + V7X_BF16_ROWS)
                p_new[rows, :] = w * jax.nn.gelu(act_old[rows, :]).astype(BF16)

        def pre_act(kc):
            ks = slice(kc * PEER_K_CHUNK, (kc + 1) * PEER_K_CHUNK)
            part = jnp.dot(u_ref[:, ks], ht_ref[ks, :], preferred_element_type=F32)
            if kc == 0:
                act_new[...] = part
            else:
                act_new[...] += part

        def accumulate(mc):
            ms = slice(mc * PEER_M_CHUNK, (mc + 1) * PEER_M_CHUNK)
            o_ref[ms, :] += jnp.dot(vt_ref[ms, :], p_old[...], preferred_element_type=F32)

        gates = [(il, r) for il in range(slabs) for r in range(PEER_NKEYS // PEER_GATE_ROWS)]
        n_k, n_m = d // PEER_K_CHUNK, d // PEER_M_CHUNK
        mxu = []
        for kc in range(n_k):
            mxu.append((functools.partial(pre_act, kc), PEER_K_CHUNK * te))
            mxu += [(functools.partial(accumulate, kc * (n_m // n_k) + m), PEER_M_CHUNK * te)
                    for m in range(n_m // n_k)]
        total = sum(cost for _, cost in mxu)
        issued = done = 0
        for piece, cost in mxu:
            piece()
            issued += cost
            upto = (len(gates) * issued) // total
            for il, r in gates[done:upto]:
                gate(il, r)
            done = upto

    @pl.when(j % 2 == 0)
    def _():
        step(act_a, act_b, p_b, p_a)

    @pl.when(j % 2 == 1)
    def _():
        step(act_b, act_a, p_a, p_b)


def _peer_dense(ht, u_bf, vt_bf, topk, tm):
    d, n = ht.shape
    nblocks, _, te = vt_bf.shape
    per_tile = dict(pipeline_mode=pl.Buffered(1))
    per_key = pl.BlockSpec((PEER_HEADS, PEER_NKEYS, tm), lambda i, j: (0, 0, i), **per_tile)
    return pl.pallas_call(
        functools.partial(_peer_kernel, te=te, nblocks=nblocks),
        grid=(n // tm, nblocks + 2),
        in_specs=[pl.BlockSpec((d, tm), lambda i, j: (0, i), **per_tile),
                  pl.BlockSpec((te, d), lambda i, j: (jnp.minimum(j, nblocks - 1), 0)),
                  pl.BlockSpec((None, d, te), lambda i, j: (jnp.clip(j - 2, 0, nblocks - 1), 0, 0))]
                 + [per_key] * 4,
        out_specs=pl.BlockSpec((d, tm), lambda i, j: (0, i)),
        out_shape=jax.ShapeDtypeStruct((d, n), F32),
        scratch_shapes=[pltpu.VMEM((te, tm), F32), pltpu.VMEM((te, tm), F32),
                        pltpu.VMEM((te, tm), BF16), pltpu.VMEM((te, tm), BF16),
                        pltpu.VMEM((2, te // PEER_NKEYS, PEER_HEADS, V7X_BF16_ROWS, tm), BF16)],
        compiler_params=_params("parallel", "arbitrary"),
        name="peer_dense",
    )(ht, u_bf, vt_bf, *topk)


def _split_mod(mod, tokens_per_stream, per_token):
    if per_token:
        mod = jnp.repeat(mod, tokens_per_stream, axis=0)[None]
    else:
        mod = mod[:, None, :]
    return jnp.split(mod, 6, axis=-1)


def _encoder_layer(x, mod, h0_re, h0_im, lp, peer_w, *, tr, scan_steps, mixer_rows, per_token_mod, emit_v):
    (g_pre_tok, g_post_tok, g_pre_ch, g_post_ch, w_in, a_ws, a_bs, a_ln_g, a_ln_b, disc, s5_d,
     w_glu, b_glu, w_branch, w_out, peer_w_query, peer_sub_keys) = lp
    u_bf, vt_bf = peer_w
    nb, t, d = x.shape
    n = nb * t
    tiles_per_group = 1 if per_token_mod else t // tr
    sh_t, sc_t, gt_t, sh_c, sc_c, gt_c = _split_mod(mod, t, per_token_mod)
    x2 = x.reshape(n, d)

    h = _prenorm(x2, g_pre_tok, sc_t, sh_t, tr, tiles_per_group)
    proj = _matmul(h, w_in, 1024, 512, name="in_proj")
    proj3 = proj.reshape(nb, t, IN_WIDTH)
    mix = _mixer_a(proj3, a_ws, a_bs, a_ln_g, a_ln_b, mixer_rows, emit_v)
    ya = mix[0].reshape(n, A_WIDTH)
    v_rows = mix[1] if emit_v else None
    xb_tm = proj3[:, :, 2 * A_WIDTH:2 * A_WIDTH + B_WIDTH].transpose(1, 0, 2).reshape(n, B_WIDTH)
    ys_tm, h_re, h_im = _s5(xb_tm, nb, disc, s5_d, h0_re, h0_im, scan_steps)
    yb_tm = _glu(ys_tm, w_glu, b_glu, 1024, 512)
    yb = yb_tm.reshape(t, nb, B_WIDTH).transpose(1, 0, 2).reshape(n, B_WIDTH)
    merged = _branch_merge(ya, yb, w_branch, proj, 1024, 512)
    m = _matmul(merged, w_out, 1024, 512, name="out_proj")

    x1, h2, h2t = _resid_prenorm(x2, m, g_post_tok, gt_t, g_pre_ch, sc_c, sh_c, tr, tiles_per_group)
    q = _matmul(h2, peer_w_query, 1024, 512, name="peer_query")
    topk = _peer_topk(q, peer_sub_keys, min(128, n))
    ft = _peer_dense(h2t, u_bf, vt_bf, topk, min(512, n))
    y = _final_resid(x1, ft, g_post_ch, gt_c, tr, tiles_per_group)
    return (y.reshape(nb, t, d), h_re.reshape(nb, B_GROUPS, B_STATE), h_im.reshape(nb, B_GROUPS, B_STATE), v_rows)


def kernel(x_prompt, x_sample, c_prompt, c_sample, state_ssm_re, state_ssm_im, w_ada, b_ada, g_pre_tok,
           g_post_tok, g_pre_ch, g_post_ch, w_in, a_ws, a_bs, a_ln_g, a_ln_b, s5_lam_re, s5_lam_im,
           s5_log_dt, s5_b_re, s5_b_im, s5_c_re, s5_c_im, s5_d, w_glu, b_glu, w_branch, w_out,
           peer_w_query, peer_sub_keys, peer_u, peer_v):
    depth = w_ada.shape[0]
    nb_p, t_p, _ = x_prompt.shape
    nb_s, t_s, _ = x_sample.shape
    y_p, y_s = x_prompt, x_sample
    re_p, im_p, re_s, im_s, v_s = [], [], [], [], []
    pad = (-(nb_p + nb_s)) % 8
    for l in range(depth):
        c_all = jnp.concatenate([c_prompt, c_sample, jnp.zeros((pad, D_MODEL), F32)], axis=0)
        mod = _ada(c_all, w_ada[l], b_ada[l])
        disc = _s5_discretize(s5_lam_re[l], s5_lam_im[l], s5_log_dt[l], s5_b_re[l], s5_b_im[l],
                              s5_c_re[l], s5_c_im[l])
        lp = (g_pre_tok[l], g_post_tok[l], g_pre_ch[l], g_post_ch[l], w_in[l], a_ws[l], a_bs[l],
              a_ln_g[l], a_ln_b[l], disc, s5_d[l], w_glu[l], b_glu[l], w_branch[l], w_out[l],
              peer_w_query[l], peer_sub_keys[l])
        peer_w = (_to_bf16(peer_u[l], transpose=False), _to_bf16(peer_v[l], transpose=True))
        zeros = jnp.zeros((nb_p, S5_STATES), F32)
        y_p, hr, hi, _ = _encoder_layer(
            y_p, mod[:nb_p], zeros, zeros, lp, peer_w,
            tr=256, scan_steps=32, mixer_rows=A_CHUNK, per_token_mod=False, emit_v=False)
        re_p.append(hr)
        im_p.append(hi)
        y_s, hr, hi, v_rows = _encoder_layer(
            y_s, mod[nb_p:nb_p + nb_s], state_ssm_re[l].reshape(nb_s, S5_STATES),
            state_ssm_im[l].reshape(nb_s, S5_STATES), lp, peer_w,
            tr=nb_s * t_s, scan_steps=t_s, mixer_rows=t_s, per_token_mod=True, emit_v=True)
        re_s.append(hr)
        im_s.append(hi)
        v_s.append(v_rows)
    return (y_p, y_s, jnp.stack(re_p), jnp.stack(im_p), jnp.stack(re_s), jnp.stack(im_s), jnp.stack(v_s))
```

```python
import functools

import jax
import jax.numpy as jnp
from jax import lax
from jax.experimental import pallas as pl
from jax.experimental.pallas import tpu as pltpu

F32 = jnp.float32
BF16 = jnp.bfloat16

D_MODEL = 4096
CHUNK = 64
A_CHUNK = 128
A_GROUPS = 16
A_GROUP_DIM = 128
A_WIDTH = A_GROUPS * A_GROUP_DIM
B_GROUP_DIM = 16
B_GROUPS = 128
B_WIDTH = B_GROUPS * B_GROUP_DIM
B_STATE = 64
S5_STATES = B_GROUPS * B_STATE
S5_GROUPS_PER_BLOCK = 8
S5_BLOCKS = B_GROUPS // S5_GROUPS_PER_BLOCK
S5_BLOCK_STATES = S5_GROUPS_PER_BLOCK * B_STATE
IN_WIDTH = 2 * A_WIDTH + B_WIDTH + 2 * D_MODEL
PEER_HEADS = 8
PEER_SUBDIM = 128
PEER_NKEYS = 128
PEER_EXPERTS = PEER_NKEYS * PEER_NKEYS
PEER_TOPK = 16
PEER_UNRANKED = 1e9
PEER_TE = 512
PEER_GATE_ROWS = 16
PEER_K_CHUNK = 256
PEER_M_CHUNK = 128
EPS = 1e-6

V7X_LANES = 128
V7X_SUBLANES = 8
V7X_BF16_ROWS = 16
V7X_VMEM_BYTES = 64 * 1024 * 1024
VMEM_LIMIT_BYTES = 56 * 1024 * 1024


def _params(*semantics, flags=None):
    return pltpu.CompilerParams(dimension_semantics=semantics, vmem_limit_bytes=VMEM_LIMIT_BYTES, flags=flags)


def _rms(x):
    return x * lax.rsqrt(jnp.mean(x * x, axis=-1, keepdims=True) + EPS)


def _ada_kernel(c_ref, w_ref, b_ref, o_ref):
    s = jax.nn.silu(c_ref[...]).astype(BF16)
    o_ref[...] = jnp.dot(s, w_ref[...].astype(BF16), preferred_element_type=F32) + b_ref[...]


def _ada(c, w, b):
    rows, d = c.shape
    n = w.shape[1]
    tn = 512
    return pl.pallas_call(
        _ada_kernel,
        grid=(n // tn,),
        in_specs=[pl.BlockSpec((rows, d), lambda j: (0, 0)),
                  pl.BlockSpec((d, tn), lambda j: (0, j)),
                  pl.BlockSpec((1, tn), lambda j: (0, j))],
        out_specs=pl.BlockSpec((rows, tn), lambda j: (0, j)),
        out_shape=jax.ShapeDtypeStruct((rows, n), F32),
        compiler_params=_params("parallel"),
        name="ada_mod",
    )(c, w, b.reshape(1, n))


def _mod_spec(mod, tiles_per_group):
    return pl.BlockSpec((None,) + mod.shape[1:], lambda i: (i // tiles_per_group, 0, 0))


def _prenorm_kernel(x_ref, g_ref, sc_ref, sh_ref, o_ref):
    h = _rms(x_ref[...]) * g_ref[...] * (1.0 + sc_ref[...]) + sh_ref[...]
    o_ref[...] = h.astype(BF16)


def _prenorm(x, g, sc, sh, tr, tiles_per_group):
    n, d = x.shape
    return pl.pallas_call(
        _prenorm_kernel,
        grid=(n // tr,),
        in_specs=[pl.BlockSpec((tr, d), lambda i: (i, 0)),
                  pl.BlockSpec((1, d), lambda i: (0, 0)),
                  _mod_spec(sc, tiles_per_group),
                  _mod_spec(sh, tiles_per_group)],
        out_specs=pl.BlockSpec((tr, d), lambda i: (i, 0)),
        out_shape=jax.ShapeDtypeStruct((n, d), BF16),
        compiler_params=_params("parallel"),
        name="prenorm_tok",
    )(x, g.reshape(1, d), sc, sh)


def _resid_kernel(x_ref, m_ref, gpost_ref, gt_ref, gpre_ref, sc_ref, sh_ref, x1_ref, h_ref, ht_ref):
    x1 = x_ref[...] + gt_ref[...] * (_rms(m_ref[...]) * gpost_ref[...])
    x1_ref[...] = x1
    h = _rms(x1) * gpre_ref[...] * (1.0 + sc_ref[...]) + sh_ref[...]
    h_ref[...] = h.astype(BF16)
    ht_ref[...] = h.T.astype(BF16)


def _resid_prenorm(x, m, gpost, gt, gpre, sc, sh, tr, tiles_per_group):
    n, d = x.shape
    row = pl.BlockSpec((tr, d), lambda i: (i, 0))
    vec = pl.BlockSpec((1, d), lambda i: (0, 0))
    return pl.pallas_call(
        _resid_kernel,
        grid=(n // tr,),
        in_specs=[row, row, vec, _mod_spec(gt, tiles_per_group), vec,
                  _mod_spec(sc, tiles_per_group), _mod_spec(sh, tiles_per_group)],
        out_specs=[row, row, pl.BlockSpec((d, tr), lambda i: (0, i))],
        out_shape=[jax.ShapeDtypeStruct((n, d), F32),
                   jax.ShapeDtypeStruct((n, d), BF16),
                   jax.ShapeDtypeStruct((d, n), BF16)],
        compiler_params=_params("parallel"),
        name="resid_prenorm_ch",
    )(x, m, gpost.reshape(1, d), gt, gpre.reshape(1, d), sc, sh)


def _final_kernel(x_ref, ft_ref, gpost_ref, gt_ref, o_ref):
    f = ft_ref[...].T
    o_ref[...] = x_ref[...] + gt_ref[...] * (_rms(f) * gpost_ref[...])


def _final_resid(x, ft, gpost, gt, tr, tiles_per_group):
    n, d = x.shape
    row = pl.BlockSpec((tr, d), lambda i: (i, 0))
    return pl.pallas_call(
        _final_kernel,
        grid=(n // tr,),
        in_specs=[row, pl.BlockSpec((d, tr), lambda i: (0, i)),
                  pl.BlockSpec((1, d), lambda i: (0, 0)), _mod_spec(gt, tiles_per_group)],
        out_specs=row,
        out_shape=jax.ShapeDtypeStruct((n, d), F32),
        compiler_params=_params("parallel"),
        name="final_resid",
    )(x, ft, gpost.reshape(1, d), gt)


def _mm_kernel(a_ref, w_ref, o_ref):
    o_ref[...] = jnp.dot(a_ref[...].astype(BF16), w_ref[...].astype(BF16),
                         preferred_element_type=F32).astype(o_ref.dtype)


def _matmul(a, w, tm, tn, out_dtype=F32, name="matmul"):
    m, k = a.shape
    n = w.shape[1]
    tm = min(tm, m)
    return pl.pallas_call(
        _mm_kernel,
        grid=(m // tm, n // tn),
        in_specs=[pl.BlockSpec((tm, k), lambda i, j: (i, 0)),
                  pl.BlockSpec((k, tn), lambda i, j: (0, j))],
        out_specs=pl.BlockSpec((tm, tn), lambda i, j: (i, j)),
        out_shape=jax.ShapeDtypeStruct((m, n), out_dtype),
        compiler_params=_params("parallel", "parallel"),
        name=name,
    )(a, w)


def _glu_kernel(y_ref, w_ref, b_ref, yj_ref, o_ref):
    acc = jnp.dot(y_ref[...].astype(BF16), w_ref[...].astype(BF16), preferred_element_type=F32)
    o_ref[...] = (yj_ref[...] * jax.nn.sigmoid(acc + b_ref[...])).astype(BF16)


def _glu(y, w, b, tm, tn):
    m, k = y.shape
    n = w.shape[1]
    tm = min(tm, m)
    return pl.pallas_call(
        _glu_kernel,
        grid=(m // tm, n // tn),
        in_specs=[pl.BlockSpec((tm, k), lambda i, j: (i, 0)),
                  pl.BlockSpec((k, tn), lambda i, j: (0, j)),
                  pl.BlockSpec((1, tn), lambda i, j: (0, j)),
                  pl.BlockSpec((tm, tn), lambda i, j: (i, j))],
        out_specs=pl.BlockSpec((tm, tn), lambda i, j: (i, j)),
        out_shape=jax.ShapeDtypeStruct((m, n), BF16),
        compiler_params=_params("parallel", "parallel"),
        name="s5_glu",
    )(y, w, b.reshape(1, n), y)


def _branch_kernel(ya_ref, yb_ref, wa_ref, wb_ref, ga_ref, gb_ref, o_ref):
    a = jnp.dot(ya_ref[...], wa_ref[...].astype(BF16), preferred_element_type=F32)
    b = jnp.dot(yb_ref[...], wb_ref[...].astype(BF16), preferred_element_type=F32)
    o_ref[...] = (jax.nn.sigmoid(ga_ref[...]) * a + jax.nn.sigmoid(gb_ref[...]) * b).astype(BF16)


def _branch_merge(ya, yb, w_branch, proj, tm, tn):
    m = ya.shape[0]
    n = w_branch.shape[1]
    tm = min(tm, m)
    ga_blk = (2 * A_WIDTH + B_WIDTH) // tn
    gb_blk = ga_blk + D_MODEL // tn
    return pl.pallas_call(
        _branch_kernel,
        grid=(n // tn, m // tm),
        in_specs=[pl.BlockSpec((tm, A_WIDTH), lambda j, i: (i, 0)),
                  pl.BlockSpec((tm, B_WIDTH), lambda j, i: (i, 0)),
                  pl.BlockSpec((A_WIDTH, tn), lambda j, i: (0, j)),
                  pl.BlockSpec((B_WIDTH, tn), lambda j, i: (A_WIDTH // B_WIDTH, j)),
                  pl.BlockSpec((tm, tn), lambda j, i: (i, ga_blk + j)),
                  pl.BlockSpec((tm, tn), lambda j, i: (i, gb_blk + j))],
        out_specs=pl.BlockSpec((tm, tn), lambda j, i: (i, j)),
        out_shape=jax.ShapeDtypeStruct((m, n), BF16),
        compiler_params=_params("parallel", "parallel"),
        name="branch_merge",
    )(ya, yb, w_branch, w_branch, proj, proj)


def _mixer_a_kernel(za_ref, ws_ref, bias_ref, lng_ref, lnb_ref, ya_ref, *v_refs, rows):
    z = jax.nn.gelu(za_ref[...])
    u = z[:, :A_WIDTH]
    v = z[:, A_WIDTH:]
    d = v - jnp.mean(v, axis=-1, keepdims=True)
    vn = d * lax.rsqrt(jnp.mean(d * d, axis=-1, keepdims=True) + EPS) * lng_ref[...] + lnb_ref[...]
    if v_refs:
        v_refs[0][...] = vn
    if rows < A_CHUNK:
        vn = jnp.concatenate([vn, jnp.zeros((A_CHUNK - rows, A_WIDTH), F32)], axis=0)
    vb = vn.astype(BF16)
    blk_i = lax.broadcasted_iota(jnp.int32, (A_CHUNK, A_CHUNK), 0) // CHUNK
    blk_j = lax.broadcasted_iota(jnp.int32, (A_CHUNK, A_CHUNK), 1) // CHUNK
    causal = blk_i >= blk_j
    for g in range(A_GROUPS):
        cols = slice(g * A_GROUP_DIM, (g + 1) * A_GROUP_DIM)
        w = jnp.where(causal, ws_ref[g], 0.0).astype(BF16)
        mixed = jnp.dot(w, vb[:, cols], preferred_element_type=F32) + bias_ref[:, cols]
        ya_ref[:, cols] = (u[:, cols] * mixed[:rows]).astype(BF16)


def _mixer_a(proj3, ws, bs, ln_g, ln_b, rows, emit_v):
    nb, t, _ = proj3.shape
    bias = jnp.repeat(bs.T, A_GROUP_DIM, axis=1)
    out_shape = [jax.ShapeDtypeStruct((nb, t, A_WIDTH), BF16)]
    out_specs = [pl.BlockSpec((None, rows, A_WIDTH), lambda b, c: (b, c, 0))]
    if emit_v:
        out_shape.append(jax.ShapeDtypeStruct((nb, t, A_WIDTH), F32))
        out_specs.append(pl.BlockSpec((None, rows, A_WIDTH), lambda b, c: (b, c, 0)))
    return pl.pallas_call(
        functools.partial(_mixer_a_kernel, rows=rows),
        grid=(nb, t // rows),
        in_specs=[pl.BlockSpec((None, rows, 2 * A_WIDTH), lambda b, c: (b, c, 0)),
                  pl.BlockSpec((A_GROUPS, A_CHUNK, A_CHUNK), lambda b, c: (0, 0, 0)),
                  pl.BlockSpec((A_CHUNK, A_WIDTH), lambda b, c: (0, 0)),
                  pl.BlockSpec((1, A_WIDTH), lambda b, c: (0, 0)),
                  pl.BlockSpec((1, A_WIDTH), lambda b, c: (0, 0))],
        out_specs=out_specs,
        out_shape=out_shape,
        compiler_params=_params("parallel", "parallel"),
        name="mixer_a",
    )(proj3, ws, bias, ln_g.reshape(1, A_WIDTH), ln_b.reshape(1, A_WIDTH))


def _s5_discretize(lam_re, lam_im, log_dt, b_re, b_im, c_re, c_im):
    dt = jnp.exp(log_dt.astype(F32))[:, None]
    lr, li = lam_re.astype(F32), lam_im.astype(F32)
    mag = jnp.exp(lr * dt)
    ab_re, ab_im = mag * jnp.cos(li * dt), mag * jnp.sin(li * dt)
    den = lr * lr + li * li
    num_re = ab_re - 1.0
    coef_re = (num_re * lr + ab_im * li) / den
    coef_im = (ab_im * lr - num_re * li) / den
    br, bi = b_re.astype(F32), b_im.astype(F32)
    bb_re = coef_re[..., None] * br - coef_im[..., None] * bi
    bb_im = coef_re[..., None] * bi + coef_im[..., None] * br
    eye = jnp.eye(S5_GROUPS_PER_BLOCK, dtype=F32)

    def pack_in(bb):
        t = bb.reshape(S5_BLOCKS, S5_GROUPS_PER_BLOCK, B_STATE, B_GROUP_DIM).transpose(0, 1, 3, 2)
        return jnp.einsum("kgcp,gm->kgcmp", t, eye).reshape(S5_BLOCKS, V7X_LANES, S5_BLOCK_STATES)

    def pack_out(cc):
        t = cc.reshape(S5_BLOCKS, S5_GROUPS_PER_BLOCK, B_GROUP_DIM, B_STATE).transpose(0, 1, 3, 2)
        return jnp.einsum("kgpc,gm->kgpmc", t, eye).reshape(S5_BLOCKS, S5_BLOCK_STATES, V7X_LANES)

    w_in = jnp.concatenate([pack_in(bb_re), pack_in(bb_im)], axis=-1).astype(BF16)
    w_out = jnp.concatenate([pack_out(c_re.astype(F32)), -pack_out(c_im.astype(F32))], axis=1).astype(BF16)
    return ab_re.reshape(1, S5_STATES), ab_im.reshape(1, S5_STATES), w_in, w_out


def _s5_kernel(xb_ref, win_ref, wout_ref, are_ref, aim_ref, d_ref, h0r_ref, h0i_ref,
               ys_ref, hr_ref, hi_ref, sre, sim, *, nb):
    @pl.when(pl.program_id(0) == 0)
    def _():
        hr_ref[...] = h0r_ref[...]
        hi_ref[...] = h0i_ref[...]

    x = xb_ref[...]
    xbf = x.astype(BF16)
    for k in range(S5_BLOCKS):
        drive = jnp.dot(xbf[:, k * V7X_LANES:(k + 1) * V7X_LANES], win_ref[k], preferred_element_type=F32)
        sre[:, k * S5_BLOCK_STATES:(k + 1) * S5_BLOCK_STATES] = drive[:, :S5_BLOCK_STATES]
        sim[:, k * S5_BLOCK_STATES:(k + 1) * S5_BLOCK_STATES] = drive[:, S5_BLOCK_STATES:]

    a_re = are_ref[...]
    a_im = aim_ref[...]
    sub = lax.broadcasted_iota(jnp.int32, (V7X_SUBLANES, S5_STATES), 0)
    for i in range(x.shape[0] // V7X_SUBLANES):
        rows = slice(i * V7X_SUBLANES, (i + 1) * V7X_SUBLANES)
        prev = slice((i - 1) * V7X_SUBLANES, i * V7X_SUBLANES)
        c_re = hr_ref[...] if i == 0 else sre[prev, :]
        c_im = hi_ref[...] if i == 0 else sim[prev, :]
        d_re = sre[rows, :]
        d_im = sim[rows, :]
        o_re = o_im = None
        for s in range(V7X_SUBLANES // nb):
            if nb < V7X_SUBLANES:
                c_re = pltpu.roll(c_re, nb, 0)
                c_im = pltpu.roll(c_im, nb, 0)
            n_re = a_re * c_re - a_im * c_im + d_re
            n_im = a_re * c_im + a_im * c_re + d_im
            o_re = n_re if s == 0 else jnp.where(sub >= s * nb, n_re, o_re)
            o_im = n_im if s == 0 else jnp.where(sub >= s * nb, n_im, o_im)
            c_re, c_im = n_re, n_im
        sre[rows, :] = o_re
        sim[rows, :] = o_im
    last = slice(x.shape[0] - V7X_SUBLANES, x.shape[0])
    hr_ref[...] = sre[last, :]
    hi_ref[...] = sim[last, :]

    for k in range(S5_BLOCKS):
        st = slice(k * S5_BLOCK_STATES, (k + 1) * S5_BLOCK_STATES)
        ch = slice(k * V7X_LANES, (k + 1) * V7X_LANES)
        y = (jnp.dot(sre[:, st].astype(BF16), wout_ref[k, :S5_BLOCK_STATES, :], preferred_element_type=F32)
             + jnp.dot(sim[:, st].astype(BF16), wout_ref[k, S5_BLOCK_STATES:, :], preferred_element_type=F32)
             + d_ref[:, ch] * x[:, ch])
        ys_ref[:, ch] = jax.nn.gelu(y)


def _s5(xb_tm, nb, disc, d_skip, h0_re, h0_im, steps):
    rows = xb_tm.shape[0]
    a_re, a_im, w_in, w_out = disc
    const2 = lambda c: (0, 0)
    const3 = lambda c: (0, 0, 0)
    blk = steps * nb
    pad = jnp.zeros((V7X_SUBLANES - nb, S5_STATES), F32)
    state = pl.BlockSpec((V7X_SUBLANES, S5_STATES), const2)
    ys, h_re, h_im = pl.pallas_call(
        functools.partial(_s5_kernel, nb=nb),
        grid=(rows // blk,),
        in_specs=[pl.BlockSpec((blk, B_WIDTH), lambda c: (c, 0)),
                  pl.BlockSpec(w_in.shape, const3),
                  pl.BlockSpec(w_out.shape, const3),
                  pl.BlockSpec((1, S5_STATES), const2),
                  pl.BlockSpec((1, S5_STATES), const2),
                  pl.BlockSpec((1, B_WIDTH), const2),
                  state, state],
        out_specs=[pl.BlockSpec((blk, B_WIDTH), lambda c: (c, 0)), state, state],
        out_shape=[jax.ShapeDtypeStruct((rows, B_WIDTH), F32),
                   jax.ShapeDtypeStruct((V7X_SUBLANES, S5_STATES), F32),
                   jax.ShapeDtypeStruct((V7X_SUBLANES, S5_STATES), F32)],
        scratch_shapes=[pltpu.VMEM((blk, S5_STATES), F32),
                        pltpu.VMEM((blk, S5_STATES), F32)],
        compiler_params=_params("arbitrary"),
        name="s5_scan",
    )(xb_tm, w_in, w_out, a_re, a_im, d_skip.reshape(1, B_WIDTH),
      jnp.concatenate([pad, h0_re], axis=0), jnp.concatenate([pad, h0_im], axis=0))
    return ys, h_re[V7X_SUBLANES - nb:], h_im[V7X_SUBLANES - nb:]


def _cast_kernel(x_ref, o_ref):
    o_ref[...] = x_ref[...].astype(BF16)


def _cast_t_kernel(x_ref, o_ref):
    o_ref[...] = x_ref[...].T.astype(BF16)


def _to_bf16(x, transpose):
    r, c = x.shape
    tr, tc = PEER_TE, 2048
    if transpose:
        return pl.pallas_call(
            _cast_t_kernel, grid=(r // tr, c // tc),
            in_specs=[pl.BlockSpec((tr, tc), lambda i, j: (i, j))],
            out_specs=pl.BlockSpec((None, tc, tr), lambda i, j: (i, j, 0)),
            out_shape=jax.ShapeDtypeStruct((r // tr, c, tr), BF16),
            compiler_params=_params("parallel", "parallel"), name="cast_transpose")(x)
    return pl.pallas_call(
        _cast_kernel, grid=(r // tr, c // tc),
        in_specs=[pl.BlockSpec((tr, tc), lambda i, j: (i, j))],
        out_specs=pl.BlockSpec((tr, tc), lambda i, j: (i, j)),
        out_shape=jax.ShapeDtypeStruct((r, c), BF16),
        compiler_params=_params("parallel", "parallel"), name="cast_bf16")(x)


_STAIR_LIMIT = [PEER_TOPK // (a + 1) for a in range(PEER_TOPK)]


def _extract_top(work, order, count, unranked=None):
    big = jnp.int32(1 << 30)
    vals, picks = [], []
    rank = None if unranked is None else jnp.full(work.shape, unranked, F32)
    for r in range(count):
        m = jnp.max(work, axis=0, keepdims=True)
        pick = jnp.min(jnp.where(work == m, order, big), axis=0, keepdims=True)
        hit = order == pick
        work = jnp.where(hit, -jnp.inf, work)
        if rank is not None:
            rank = jnp.where(hit, float(r), rank)
        vals.append(m)
        picks.append(pick)
    return vals, picks, rank


def _stack_rows(rows, row_idx):
    out = jnp.broadcast_to(rows[0], row_idx.shape)
    for r in range(1, len(rows)):
        out = jnp.where(row_idx == r, rows[r], out)
    return out


def _topk_kernel(q_ref, keys_ref, e1n_ref, lim_ref, e2_ref, r2_ref, *, tm):
    key_idx = lax.broadcasted_iota(jnp.int32, (PEER_NKEYS, tm), 0)
    iota16 = lax.broadcasted_iota(jnp.int32, (PEER_TOPK, tm), 0)
    iota8 = iota16[:8]
    flat = [iota16] + [a * PEER_TOPK + iota8 for a in range(1, 8)] + [(iota8 + 8) * PEER_TOPK]
    flat = jnp.concatenate(flat, axis=0)
    for h in range(PEER_HEADS):
        scores, tops, ranks = [], [], []
        for s in range(2):
            col = (2 * h + s) * PEER_SUBDIM
            qs = q_ref[:, col:col + PEER_SUBDIM].astype(BF16)
            ks = keys_ref[h, s].astype(BF16)
            sc = lax.dot_general(ks, qs, (((1,), (1,)), ((), ())), preferred_element_type=F32)
            vals, _, rank = _extract_top(sc, key_idx, PEER_TOPK, unranked=(-1.0, PEER_UNRANKED)[s])
            scores.append(sc)
            tops.append(vals)
            ranks.append(rank)
        v1, v2 = tops
        v2_16 = _stack_rows(v2, iota16)
        v2_8 = v2_16[:8]
        cand = [v1[0] + v2_16]
        for a in range(1, 8):
            cand.append(jnp.where(iota8 < _STAIR_LIMIT[a], v1[a] + v2_8, -jnp.inf))
        cand.append(_stack_rows(v1[8:], iota8) + v2[0])
        cand = jnp.concatenate(cand, axis=0)
        cv, picks, _ = _extract_top(cand, flat, PEER_TOPK)
        z = jnp.ones_like(cv[0])
        for r in range(1, PEER_TOPK):
            z = z + jnp.exp(cv[r] - cv[0])
        limit = jnp.full((PEER_TOPK, tm), -1, jnp.int32)
        for pick in picks:
            row, col = pick >> 4, pick & (PEER_TOPK - 1)
            limit = jnp.maximum(limit, jnp.where(iota16 == row, col, -1))
        limit = limit.astype(F32)
        lim = jnp.full((PEER_NKEYS, tm), -1.0, F32)
        for a in range(PEER_TOPK):
            lim = jnp.where(ranks[0] == float(a), limit[a:a + 1, :], lim)
        e1n_ref[h] = jnp.exp(scores[0] - v1[0]) / z
        lim_ref[h] = lim
        e2_ref[h] = jnp.exp(scores[1] - v2[0]).astype(BF16)
        r2_ref[h] = ranks[1].astype(BF16)


def _peer_topk(q, keys, tm):
    n = q.shape[0]
    per_key = pl.BlockSpec((PEER_HEADS, PEER_NKEYS, tm), lambda i: (0, 0, i))
    shape = (PEER_HEADS, PEER_NKEYS, n)
    return pl.pallas_call(
        functools.partial(_topk_kernel, tm=tm),
        grid=(n // tm,),
        in_specs=[pl.BlockSpec((tm, q.shape[1]), lambda i: (i, 0)),
                  pl.BlockSpec(keys.shape, lambda i: (0, 0, 0, 0))],
        out_specs=[per_key] * 4,
        out_shape=[jax.ShapeDtypeStruct(shape, F32), jax.ShapeDtypeStruct(shape, F32),
                   jax.ShapeDtypeStruct(shape, BF16), jax.ShapeDtypeStruct(shape, BF16)],
        compiler_params=_params("parallel"),
        name="peer_topk",
    )(q, keys)


def _peer_kernel(ht_ref, u_ref, vt_ref, e1n_ref, lim_ref, e2_ref, r2_ref,
                 o_ref, act_a, act_b, p_a, p_b, row_bc, *, te, nblocks, total):
    j = pl.program_id(0)
    slabs = te // PEER_NKEYS

    tm = o_ref.shape[1]
    tile = (V7X_BF16_ROWS, tm)

    @pl.when(j == 0)
    def _():
        p_a[...] = jnp.zeros_like(p_a)
        act_b[...] = jnp.zeros_like(act_b)

    @pl.when(jnp.maximum(j - 2, 0) % nblocks == 0)
    def _():
        o_ref[...] = jnp.zeros_like(o_ref)

    def step(act_new, act_old, p_new, p_old):
        first = (jnp.clip(j - 1, 0, total - 1) % nblocks) * slabs
        d = ht_ref.shape[0]

        def gate(il, r):
            i1 = first + il
            per_head = []
            for h in range(PEER_HEADS):
                if r == 0:
                    rows_i1 = [jnp.broadcast_to(ref[h, pl.ds(i1, 1), :], tile).astype(BF16)
                               for ref in (e1n_ref, lim_ref)]
                    for a, val in enumerate(rows_i1):
                        row_bc[a, il, h] = val
                else:
                    rows_i1 = [row_bc[a, il, h] for a in range(2)]
                per_head.append(rows_i1)
            for q in range(PEER_GATE_ROWS // V7X_BF16_ROWS):
                k0 = r * PEER_GATE_ROWS + q * V7X_BF16_ROWS
                keys = slice(k0, k0 + V7X_BF16_ROWS)
                w = jnp.zeros(tile, BF16)
                for h in range(PEER_HEADS):
                    e1, lim = per_head[h]
                    w = w + jnp.where(r2_ref[h, keys, :] <= lim, e2_ref[h, keys, :] * e1, jnp.zeros((), BF16))
                rows = slice(il * PEER_NKEYS + k0, il * PEER_NKEYS + k0 + V7X_BF16_ROWS)
                p_new[rows, :] = w * jax.nn.gelu(act_old[rows, :]).astype(BF16)

        def pre_act(kc):
            ks = slice(kc * PEER_K_CHUNK, (kc + 1) * PEER_K_CHUNK)
            part = jnp.dot(u_ref[:, ks], ht_ref[ks, :], preferred_element_type=F32)
            if kc == 0:
                act_new[...] = part
            else:
                act_new[...] += part

        def accumulate(mc):
            ms = slice(mc * PEER_M_CHUNK, (mc + 1) * PEER_M_CHUNK)
            o_ref[ms, :] += jnp.dot(vt_ref[ms, :], p_old[...], preferred_element_type=F32)

        gates = [(il, r) for il in range(slabs) for r in range(PEER_NKEYS // PEER_GATE_ROWS)]
        n_k, n_m = d // PEER_K_CHUNK, d // PEER_M_CHUNK
        mxu = []
        for kc in range(n_k):
            mxu.append((functools.partial(pre_act, kc), PEER_K_CHUNK * te))
            mxu += [(functools.partial(accumulate, kc * (n_m // n_k) + m), PEER_M_CHUNK * te)
                    for m in range(n_m // n_k)]
        mxu_total = sum(cost for _, cost in mxu)
        issued = done = 0
        for piece, cost in mxu:
            piece()
            issued += cost
            upto = (len(gates) * issued) // mxu_total
            for il, r in gates[done:upto]:
                gate(il, r)
            done = upto

    @pl.when(j % 2 == 0)
    def _():
        step(act_a, act_b, p_b, p_a)

    @pl.when(j % 2 == 1)
    def _():
        step(act_b, act_a, p_a, p_b)


def _peer_dense(ht, u_bf, vt_bf, topk, tm):
    d, n = ht.shape
    nblocks, _, te = vt_bf.shape
    total = (n // tm) * nblocks
    item = lambda f, lag: jnp.clip(f - lag, 0, total - 1)
    per_tile = dict(pipeline_mode=pl.Buffered(1))
    per_key = pl.BlockSpec((PEER_HEADS, PEER_NKEYS, tm), lambda f: (0, 0, item(f, 1) // nblocks), **per_tile)
    return pl.pallas_call(
        functools.partial(_peer_kernel, te=te, nblocks=nblocks, total=total),
        grid=(total + 2,),
        in_specs=[pl.BlockSpec((d, tm), lambda f: (0, item(f, 0) // nblocks), **per_tile),
                  pl.BlockSpec((te, d), lambda f: (item(f, 0) % nblocks, 0)),
                  pl.BlockSpec((None, d, te), lambda f: (item(f, 2) % nblocks, 0, 0))]
                 + [per_key] * 4,
        out_specs=pl.BlockSpec((d, tm), lambda f: (0, item(f, 2) // nblocks)),
        out_shape=jax.ShapeDtypeStruct((d, n), F32),
        scratch_shapes=[pltpu.VMEM((te, tm), F32), pltpu.VMEM((te, tm), F32),
                        pltpu.VMEM((te, tm), BF16), pltpu.VMEM((te, tm), BF16),
                        pltpu.VMEM((2, te // PEER_NKEYS, PEER_HEADS, V7X_BF16_ROWS, tm), BF16)],
        compiler_params=_params("arbitrary"),
        name="peer_dense",
    )(ht, u_bf, vt_bf, *topk)


def _split_mod(mod, tokens_per_stream, per_token):
    if per_token:
        mod = jnp.repeat(mod, tokens_per_stream, axis=0)[None]
    else:
        mod = mod[:, None, :]
    return jnp.split(mod, 6, axis=-1)


def _encoder_layer(x, mod, h0_re, h0_im, lp, peer_w, *, tr, scan_steps, mixer_rows, per_token_mod, emit_v):
    (g_pre_tok, g_post_tok, g_pre_ch, g_post_ch, w_in, a_ws, a_bs, a_ln_g, a_ln_b, disc, s5_d,
     w_glu, b_glu, w_branch, w_out, peer_w_query, peer_sub_keys) = lp
    u_bf, vt_bf = peer_w
    nb, t, d = x.shape
    n = nb * t
    tiles_per_group = 1 if per_token_mod else t // tr
    sh_t, sc_t, gt_t, sh_c, sc_c, gt_c = _split_mod(mod, t, per_token_mod)
    x2 = x.reshape(n, d)

    h = _prenorm(x2, g_pre_tok, sc_t, sh_t, tr, tiles_per_group)
    proj = _matmul(h, w_in, 1024, 512, name="in_proj")
    proj3 = proj.reshape(nb, t, IN_WIDTH)
    mix = _mixer_a(proj3, a_ws, a_bs, a_ln_g, a_ln_b, mixer_rows, emit_v)
    ya = mix[0].reshape(n, A_WIDTH)
    v_rows = mix[1] if emit_v else None
    xb_tm = proj3[:, :, 2 * A_WIDTH:2 * A_WIDTH + B_WIDTH].transpose(1, 0, 2).reshape(n, B_WIDTH)
    ys_tm, h_re, h_im = _s5(xb_tm, nb, disc, s5_d, h0_re, h0_im, scan_steps)
    yb_tm = _glu(ys_tm, w_glu, b_glu, 1024, 512)
    yb = yb_tm.reshape(t, nb, B_WIDTH).transpose(1, 0, 2).reshape(n, B_WIDTH)
    merged = _branch_merge(ya, yb, w_branch, proj, 1024, 512)
    m = _matmul(merged, w_out, 1024, 512, name="out_proj")

    x1, h2, h2t = _resid_prenorm(x2, m, g_post_tok, gt_t, g_pre_ch, sc_c, sh_c, tr, tiles_per_group)
    q = _matmul(h2, peer_w_query, 1024, 512, name="peer_query")
    topk = _peer_topk(q, peer_sub_keys, min(128, n))
    ft = _peer_dense(h2t, u_bf, vt_bf, topk, min(512, n))
    y = _final_resid(x1, ft, g_post_ch, gt_c, tr, tiles_per_group)
    return (y.reshape(nb, t, d), h_re.reshape(nb, B_GROUPS, B_STATE), h_im.reshape(nb, B_GROUPS, B_STATE), v_rows)


def kernel(x_prompt, x_sample, c_prompt, c_sample, state_ssm_re, state_ssm_im, w_ada, b_ada, g_pre_tok,
           g_post_tok, g_pre_ch, g_post_ch, w_in, a_ws, a_bs, a_ln_g, a_ln_b, s5_lam_re, s5_lam_im,
           s5_log_dt, s5_b_re, s5_b_im, s5_c_re, s5_c_im, s5_d, w_glu, b_glu, w_branch, w_out,
           peer_w_query, peer_sub_keys, peer_u, peer_v):
    depth = w_ada.shape[0]
    nb_p, t_p, _ = x_prompt.shape
    nb_s, t_s, _ = x_sample.shape
    y_p, y_s = x_prompt, x_sample
    re_p, im_p, re_s, im_s, v_s = [], [], [], [], []
    pad = (-(nb_p + nb_s)) % 8
    for l in range(depth):
        c_all = jnp.concatenate([c_prompt, c_sample, jnp.zeros((pad, D_MODEL), F32)], axis=0)
        mod = _ada(c_all, w_ada[l], b_ada[l])
        disc = _s5_discretize(s5_lam_re[l], s5_lam_im[l], s5_log_dt[l], s5_b_re[l], s5_b_im[l],
                              s5_c_re[l], s5_c_im[l])
        lp = (g_pre_tok[l], g_post_tok[l], g_pre_ch[l], g_post_ch[l], w_in[l], a_ws[l], a_bs[l],
              a_ln_g[l], a_ln_b[l], disc, s5_d[l], w_glu[l], b_glu[l], w_branch[l], w_out[l],
              peer_w_query[l], peer_sub_keys[l])
        peer_w = (_to_bf16(peer_u[l], transpose=False), _to_bf16(peer_v[l], transpose=True))
        zeros = jnp.zeros((nb_p, S5_STATES), F32)
        y_p, hr, hi, _ = _encoder_layer(
            y_p, mod[:nb_p], zeros, zeros, lp, peer_w,
            tr=256, scan_steps=32, mixer_rows=A_CHUNK, per_token_mod=False, emit_v=False)
        re_p.append(hr)
        im_p.append(hi)
        y_s, hr, hi, v_rows = _encoder_layer(
            y_s, mod[nb_p:nb_p + nb_s], state_ssm_re[l].reshape(nb_s, S5_STATES),
            state_ssm_im[l].reshape(nb_s, S5_STATES), lp, peer_w,
            tr=nb_s * t_s, scan_steps=t_s, mixer_rows=t_s, per_token_mod=True, emit_v=True)
        re_s.append(hr)
        im_s.append(hi)
        v_s.append(v_rows)
    return (y_p, y_s, jnp.stack(re_p), jnp.stack(im_p), jnp.stack(re_s), jnp.stack(im_s), jnp.stack(v_s))
```

```python
import functools

import jax
import jax.numpy as jnp
from jax import lax
from jax.experimental import pallas as pl
from jax.experimental.pallas import tpu as pltpu

F32 = jnp.float32
BF16 = jnp.bfloat16

D_MODEL = 4096
CHUNK = 64
A_CHUNK = 128
A_GROUPS = 16
A_GROUP_DIM = 128
A_WIDTH = A_GROUPS * A_GROUP_DIM
B_GROUP_DIM = 16
B_GROUPS = 128
B_WIDTH = B_GROUPS * B_GROUP_DIM
B_STATE = 64
S5_STATES = B_GROUPS * B_STATE
S5_GROUPS_PER_BLOCK = 8
S5_BLOCKS = B_GROUPS // S5_GROUPS_PER_BLOCK
S5_BLOCK_STATES = S5_GROUPS_PER_BLOCK * B_STATE
IN_WIDTH = 2 * A_WIDTH + B_WIDTH + 2 * D_MODEL
PEER_HEADS = 8
PEER_SUBDIM = 128
PEER_NKEYS = 128
PEER_EXPERTS = PEER_NKEYS * PEER_NKEYS
PEER_TOPK = 16
PEER_UNRANKED = 1e9
PEER_TE = 512
PEER_GATE_ROWS = 16
PEER_K_CHUNK = 256
PEER_M_CHUNK = 128
EPS = 1e-6

V7X_LANES = 128
V7X_SUBLANES = 8
V7X_BF16_ROWS = 16
V7X_VMEM_BYTES = 64 * 1024 * 1024
VMEM_LIMIT_BYTES = 56 * 1024 * 1024


def _params(*semantics, flags=None):
    return pltpu.CompilerParams(dimension_semantics=semantics, vmem_limit_bytes=VMEM_LIMIT_BYTES, flags=flags)


def _rms(x):
    return x * lax.rsqrt(jnp.mean(x * x, axis=-1, keepdims=True) + EPS)


def _ada_kernel(c_ref, w_ref, b_ref, o_ref):
    s = jax.nn.silu(c_ref[...]).astype(BF16)
    o_ref[...] = jnp.dot(s, w_ref[...].astype(BF16), preferred_element_type=F32) + b_ref[...]


def _ada(c, w, b):
    rows, d = c.shape
    n = w.shape[1]
    tn = 512
    return pl.pallas_call(
        _ada_kernel,
        grid=(n // tn,),
        in_specs=[pl.BlockSpec((rows, d), lambda j: (0, 0)),
                  pl.BlockSpec((d, tn), lambda j: (0, j)),
                  pl.BlockSpec((1, tn), lambda j: (0, j))],
        out_specs=pl.BlockSpec((rows, tn), lambda j: (0, j)),
        out_shape=jax.ShapeDtypeStruct((rows, n), F32),
        compiler_params=_params("parallel"),
        name="ada_mod",
    )(c, w, b.reshape(1, n))


def _mod_spec(mod, tiles_per_group):
    return pl.BlockSpec((None,) + mod.shape[1:], lambda i: (i // tiles_per_group, 0, 0))


def _prenorm_kernel(x_ref, g_ref, sc_ref, sh_ref, o_ref):
    h = _rms(x_ref[...]) * g_ref[...] * (1.0 + sc_ref[...]) + sh_ref[...]
    o_ref[...] = h.astype(BF16)


def _prenorm(x, g, sc, sh, tr, tiles_per_group):
    n, d = x.shape
    return pl.pallas_call(
        _prenorm_kernel,
        grid=(n // tr,),
        in_specs=[pl.BlockSpec((tr, d), lambda i: (i, 0)),
                  pl.BlockSpec((1, d), lambda i: (0, 0)),
                  _mod_spec(sc, tiles_per_group),
                  _mod_spec(sh, tiles_per_group)],
        out_specs=pl.BlockSpec((tr, d), lambda i: (i, 0)),
        out_shape=jax.ShapeDtypeStruct((n, d), BF16),
        compiler_params=_params("parallel"),
        name="prenorm_tok",
    )(x, g.reshape(1, d), sc, sh)


def _resid_kernel(x_ref, m_ref, gpost_ref, gt_ref, gpre_ref, sc_ref, sh_ref, x1_ref, h_ref, ht_ref):
    x1 = x_ref[...] + gt_ref[...] * (_rms(m_ref[...]) * gpost_ref[...])
    x1_ref[...] = x1
    h = _rms(x1) * gpre_ref[...] * (1.0 + sc_ref[...]) + sh_ref[...]
    h_ref[...] = h.astype(BF16)
    ht_ref[...] = h.T.astype(BF16)


def _resid_prenorm(x, m, gpost, gt, gpre, sc, sh, tr, tiles_per_group):
    n, d = x.shape
    row = pl.BlockSpec((tr, d), lambda i: (i, 0))
    vec = pl.BlockSpec((1, d), lambda i: (0, 0))
    return pl.pallas_call(
        _resid_kernel,
        grid=(n // tr,),
        in_specs=[row, row, vec, _mod_spec(gt, tiles_per_group), vec,
                  _mod_spec(sc, tiles_per_group), _mod_spec(sh, tiles_per_group)],
        out_specs=[row, row, pl.BlockSpec((d, tr), lambda i: (0, i))],
        out_shape=[jax.ShapeDtypeStruct((n, d), F32),
                   jax.ShapeDtypeStruct((n, d), BF16),
                   jax.ShapeDtypeStruct((d, n), BF16)],
        compiler_params=_params("parallel"),
        name="resid_prenorm_ch",
    )(x, m, gpost.reshape(1, d), gt, gpre.reshape(1, d), sc, sh)


def _final_kernel(x_ref, ft_ref, gpost_ref, gt_ref, o_ref):
    f = ft_ref[...].T
    o_ref[...] = x_ref[...] + gt_ref[...] * (_rms(f) * gpost_ref[...])


def _final_resid(x, ft, gpost, gt, tr, tiles_per_group):
    n, d = x.shape
    row = pl.BlockSpec((tr, d), lambda i: (i, 0))
    return pl.pallas_call(
        _final_kernel,
        grid=(n // tr,),
        in_specs=[row, pl.BlockSpec((d, tr), lambda i: (0, i)),
                  pl.BlockSpec((1, d), lambda i: (0, 0)), _mod_spec(gt, tiles_per_group)],
        out_specs=row,
        out_shape=jax.ShapeDtypeStruct((n, d), F32),
        compiler_params=_params("parallel"),
        name="final_resid",
    )(x, ft, gpost.reshape(1, d), gt)


def _mm_kernel(a_ref, w_ref, o_ref):
    o_ref[...] = jnp.dot(a_ref[...].astype(BF16), w_ref[...].astype(BF16),
                         preferred_element_type=F32).astype(o_ref.dtype)


def _matmul(a, w, tm, tn, out_dtype=F32, name="matmul"):
    m, k = a.shape
    n = w.shape[1]
    tm = min(tm, m)
    return pl.pallas_call(
        _mm_kernel,
        grid=(m // tm, n // tn),
        in_specs=[pl.BlockSpec((tm, k), lambda i, j: (i, 0)),
                  pl.BlockSpec((k, tn), lambda i, j: (0, j))],
        out_specs=pl.BlockSpec((tm, tn), lambda i, j: (i, j)),
        out_shape=jax.ShapeDtypeStruct((m, n), out_dtype),
        compiler_params=_params("parallel", "parallel"),
        name=name,
    )(a, w)


def _glu_kernel(y_ref, w_ref, b_ref, yj_ref, o_ref):
    acc = jnp.dot(y_ref[...].astype(BF16), w_ref[...].astype(BF16), preferred_element_type=F32)
    o_ref[...] = (yj_ref[...] * jax.nn.sigmoid(acc + b_ref[...])).astype(BF16)


def _glu(y, w, b, tm, tn):
    m, k = y.shape
    n = w.shape[1]
    tm = min(tm, m)
    return pl.pallas_call(
        _glu_kernel,
        grid=(m // tm, n // tn),
        in_specs=[pl.BlockSpec((tm, k), lambda i, j: (i, 0)),
                  pl.BlockSpec((k, tn), lambda i, j: (0, j)),
                  pl.BlockSpec((1, tn), lambda i, j: (0, j)),
                  pl.BlockSpec((tm, tn), lambda i, j: (i, j))],
        out_specs=pl.BlockSpec((tm, tn), lambda i, j: (i, j)),
        out_shape=jax.ShapeDtypeStruct((m, n), BF16),
        compiler_params=_params("parallel", "parallel"),
        name="s5_glu",
    )(y, w, b.reshape(1, n), y)


def _branch_kernel(ya_ref, yb_ref, wa_ref, wb_ref, ga_ref, gb_ref, o_ref):
    a = jnp.dot(ya_ref[...], wa_ref[...].astype(BF16), preferred_element_type=F32)
    b = jnp.dot(yb_ref[...], wb_ref[...].astype(BF16), preferred_element_type=F32)
    o_ref[...] = (jax.nn.sigmoid(ga_ref[...]) * a + jax.nn.sigmoid(gb_ref[...]) * b).astype(BF16)


def _branch_merge(ya, yb, w_branch, proj, tm, tn):
    m = ya.shape[0]
    n = w_branch.shape[1]
    tm = min(tm, m)
    ga_blk = (2 * A_WIDTH + B_WIDTH) // tn
    gb_blk = ga_blk + D_MODEL // tn
    return pl.pallas_call(
        _branch_kernel,
        grid=(n // tn, m // tm),
        in_specs=[pl.BlockSpec((tm, A_WIDTH), lambda j, i: (i, 0)),
                  pl.BlockSpec((tm, B_WIDTH), lambda j, i: (i, 0)),
                  pl.BlockSpec((A_WIDTH, tn), lambda j, i: (0, j)),
                  pl.BlockSpec((B_WIDTH, tn), lambda j, i: (A_WIDTH // B_WIDTH, j)),
                  pl.BlockSpec((tm, tn), lambda j, i: (i, ga_blk + j)),
                  pl.BlockSpec((tm, tn), lambda j, i: (i, gb_blk + j))],
        out_specs=pl.BlockSpec((tm, tn), lambda j, i: (i, j)),
        out_shape=jax.ShapeDtypeStruct((m, n), BF16),
        compiler_params=_params("parallel", "parallel"),
        name="branch_merge",
    )(ya, yb, w_branch, w_branch, proj, proj)


def _mixer_a_kernel(za_ref, ws_ref, bias_ref, lng_ref, lnb_ref, ya_ref, *v_refs, rows):
    z = jax.nn.gelu(za_ref[...])
    u = z[:, :A_WIDTH]
    v = z[:, A_WIDTH:]
    d = v - jnp.mean(v, axis=-1, keepdims=True)
    vn = d * lax.rsqrt(jnp.mean(d * d, axis=-1, keepdims=True) + EPS) * lng_ref[...] + lnb_ref[...]
    if v_refs:
        v_refs[0][...] = vn
    if rows < A_CHUNK:
        vn = jnp.concatenate([vn, jnp.zeros((A_CHUNK - rows, A_WIDTH), F32)], axis=0)
    vb = vn.astype(BF16)
    blk_i = lax.broadcasted_iota(jnp.int32, (A_CHUNK, A_CHUNK), 0) // CHUNK
    blk_j = lax.broadcasted_iota(jnp.int32, (A_CHUNK, A_CHUNK), 1) // CHUNK
    causal = blk_i >= blk_j
    for g in range(A_GROUPS):
        cols = slice(g * A_GROUP_DIM, (g + 1) * A_GROUP_DIM)
        w = jnp.where(causal, ws_ref[g], 0.0).astype(BF16)
        mixed = jnp.dot(w, vb[:, cols], preferred_element_type=F32) + bias_ref[:, cols]
        ya_ref[:, cols] = (u[:, cols] * mixed[:rows]).astype(BF16)


def _mixer_a(proj3, ws, bs, ln_g, ln_b, rows, emit_v):
    nb, t, _ = proj3.shape
    bias = jnp.repeat(bs.T, A_GROUP_DIM, axis=1)
    out_shape = [jax.ShapeDtypeStruct((nb, t, A_WIDTH), BF16)]
    out_specs = [pl.BlockSpec((None, rows, A_WIDTH), lambda b, c: (b, c, 0))]
    if emit_v:
        out_shape.append(jax.ShapeDtypeStruct((nb, t, A_WIDTH), F32))
        out_specs.append(pl.BlockSpec((None, rows, A_WIDTH), lambda b, c: (b, c, 0)))
    return pl.pallas_call(
        functools.partial(_mixer_a_kernel, rows=rows),
        grid=(nb, t // rows),
        in_specs=[pl.BlockSpec((None, rows, 2 * A_WIDTH), lambda b, c: (b, c, 0)),
                  pl.BlockSpec((A_GROUPS, A_CHUNK, A_CHUNK), lambda b, c: (0, 0, 0)),
                  pl.BlockSpec((A_CHUNK, A_WIDTH), lambda b, c: (0, 0)),
                  pl.BlockSpec((1, A_WIDTH), lambda b, c: (0, 0)),
                  pl.BlockSpec((1, A_WIDTH), lambda b, c: (0, 0))],
        out_specs=out_specs,
        out_shape=out_shape,
        compiler_params=_params("parallel", "parallel"),
        name="mixer_a",
    )(proj3, ws, bias, ln_g.reshape(1, A_WIDTH), ln_b.reshape(1, A_WIDTH))


def _s5_discretize(lam_re, lam_im, log_dt, b_re, b_im, c_re, c_im):
    dt = jnp.exp(log_dt.astype(F32))[:, None]
    lr, li = lam_re.astype(F32), lam_im.astype(F32)
    mag = jnp.exp(lr * dt)
    ab_re, ab_im = mag * jnp.cos(li * dt), mag * jnp.sin(li * dt)
    den = lr * lr + li * li
    num_re = ab_re - 1.0
    coef_re = (num_re * lr + ab_im * li) / den
    coef_im = (ab_im * lr - num_re * li) / den
    br, bi = b_re.astype(F32), b_im.astype(F32)
    bb_re = coef_re[..., None] * br - coef_im[..., None] * bi
    bb_im = coef_re[..., None] * bi + coef_im[..., None] * br
    eye = jnp.eye(S5_GROUPS_PER_BLOCK, dtype=F32)

    def pack_in(bb):
        t = bb.reshape(S5_BLOCKS, S5_GROUPS_PER_BLOCK, B_STATE, B_GROUP_DIM).transpose(0, 1, 3, 2)
        return jnp.einsum("kgcp,gm->kgcmp", t, eye).reshape(S5_BLOCKS, V7X_LANES, S5_BLOCK_STATES)

    def pack_out(cc):
        t = cc.reshape(S5_BLOCKS, S5_GROUPS_PER_BLOCK, B_GROUP_DIM, B_STATE).transpose(0, 1, 3, 2)
        return jnp.einsum("kgpc,gm->kgpmc", t, eye).reshape(S5_BLOCKS, S5_BLOCK_STATES, V7X_LANES)

    w_in = jnp.concatenate([pack_in(bb_re), pack_in(bb_im)], axis=-1).astype(BF16)
    w_out = jnp.concatenate([pack_out(c_re.astype(F32)), -pack_out(c_im.astype(F32))], axis=1).astype(BF16)
    return ab_re.reshape(1, S5_STATES), ab_im.reshape(1, S5_STATES), w_in, w_out


def _s5_kernel(xb_ref, perm_ref, unperm_ref, win_ref, wout_ref, are_ref, aim_ref, d_ref, h0r_ref, h0i_ref,
               ys_ref, hr_ref, hi_ref, sre, sim, g_tm, *, nb):
    @pl.when(pl.program_id(0) == 0)
    def _():
        hr_ref[...] = h0r_ref[...]
        hi_ref[...] = h0i_ref[...]

    x_sm = xb_ref[...].reshape(perm_ref.shape[0], B_WIDTH)
    x_hi = x_sm.astype(BF16)
    rest = x_sm - x_hi.astype(F32)
    x_mid = rest.astype(BF16)
    x_lo = (rest - x_mid.astype(F32)).astype(BF16)
    perm = perm_ref[...]
    x_top = jnp.dot(perm, x_hi, preferred_element_type=F32)
    x = (x_top + jnp.dot(perm, x_mid, preferred_element_type=F32)
         + jnp.dot(perm, x_lo, preferred_element_type=F32))
    xbf = x_top.astype(BF16)
    for k in range(S5_BLOCKS):
        drive = jnp.dot(xbf[:, k * V7X_LANES:(k + 1) * V7X_LANES], win_ref[k], preferred_element_type=F32)
        sre[:, k * S5_BLOCK_STATES:(k + 1) * S5_BLOCK_STATES] = drive[:, :S5_BLOCK_STATES]
        sim[:, k * S5_BLOCK_STATES:(k + 1) * S5_BLOCK_STATES] = drive[:, S5_BLOCK_STATES:]

    a_re = are_ref[...]
    a_im = aim_ref[...]
    sub = lax.broadcasted_iota(jnp.int32, (V7X_SUBLANES, S5_STATES), 0)
    for i in range(x.shape[0] // V7X_SUBLANES):
        rows = slice(i * V7X_SUBLANES, (i + 1) * V7X_SUBLANES)
        prev = slice((i - 1) * V7X_SUBLANES, i * V7X_SUBLANES)
        c_re = hr_ref[...] if i == 0 else sre[prev, :]
        c_im = hi_ref[...] if i == 0 else sim[prev, :]
        d_re = sre[rows, :]
        d_im = sim[rows, :]
        o_re = o_im = None
        for s in range(V7X_SUBLANES // nb):
            if nb < V7X_SUBLANES:
                c_re = pltpu.roll(c_re, nb, 0)
                c_im = pltpu.roll(c_im, nb, 0)
            n_re = a_re * c_re - a_im * c_im + d_re
            n_im = a_re * c_im + a_im * c_re + d_im
            o_re = n_re if s == 0 else jnp.where(sub >= s * nb, n_re, o_re)
            o_im = n_im if s == 0 else jnp.where(sub >= s * nb, n_im, o_im)
            c_re, c_im = n_re, n_im
        sre[rows, :] = o_re
        sim[rows, :] = o_im
    last = slice(x.shape[0] - V7X_SUBLANES, x.shape[0])
    hr_ref[...] = sre[last, :]
    hi_ref[...] = sim[last, :]

    for k in range(S5_BLOCKS):
        st = slice(k * S5_BLOCK_STATES, (k + 1) * S5_BLOCK_STATES)
        ch = slice(k * V7X_LANES, (k + 1) * V7X_LANES)
        y = (jnp.dot(sre[:, st].astype(BF16), wout_ref[k, :S5_BLOCK_STATES, :], preferred_element_type=F32)
             + jnp.dot(sim[:, st].astype(BF16), wout_ref[k, S5_BLOCK_STATES:, :], preferred_element_type=F32)
             + d_ref[:, ch] * x[:, ch])
        g_tm[:, ch] = jax.nn.gelu(y).astype(BF16)
    g_sm = jnp.dot(unperm_ref[...], g_tm[...], preferred_element_type=F32).astype(BF16)
    ys_ref[...] = g_sm.reshape(ys_ref.shape)


def _s5(proj3, disc, d_skip, h0_re, h0_im, steps):
    nb, t, _ = proj3.shape
    a_re, a_im, w_in, w_out = disc
    const2 = lambda c: (0, 0)
    const3 = lambda c: (0, 0, 0)
    blk = steps * nb
    src = jnp.arange(blk)
    perm = jax.nn.one_hot((src % nb) * steps + src // nb, blk, dtype=BF16)
    pad = jnp.zeros((V7X_SUBLANES - nb, S5_STATES), F32)
    state = pl.BlockSpec((V7X_SUBLANES, S5_STATES), const2)
    xb_blk = 2 * A_WIDTH // B_WIDTH
    ys, h_re, h_im = pl.pallas_call(
        functools.partial(_s5_kernel, nb=nb),
        grid=(t // steps,),
        in_specs=[pl.BlockSpec((nb, steps, B_WIDTH), lambda c: (0, c, xb_blk)),
                  pl.BlockSpec((blk, blk), const2),
                  pl.BlockSpec((blk, blk), const2),
                  pl.BlockSpec(w_in.shape, const3),
                  pl.BlockSpec(w_out.shape, const3),
                  pl.BlockSpec((1, S5_STATES), const2),
                  pl.BlockSpec((1, S5_STATES), const2),
                  pl.BlockSpec((1, B_WIDTH), const2),
                  state, state],
        out_specs=[pl.BlockSpec((nb, steps, B_WIDTH), lambda c: (0, c, 0)), state, state],
        out_shape=[jax.ShapeDtypeStruct((nb, t, B_WIDTH), BF16),
                   jax.ShapeDtypeStruct((V7X_SUBLANES, S5_STATES), F32),
                   jax.ShapeDtypeStruct((V7X_SUBLANES, S5_STATES), F32)],
        scratch_shapes=[pltpu.VMEM((blk, S5_STATES), F32),
                        pltpu.VMEM((blk, S5_STATES), F32),
                        pltpu.VMEM((blk, B_WIDTH), BF16)],
        compiler_params=_params("arbitrary"),
        name="s5_scan",
    )(proj3, perm, perm.T, w_in, w_out, a_re, a_im, d_skip.reshape(1, B_WIDTH),
      jnp.concatenate([pad, h0_re], axis=0), jnp.concatenate([pad, h0_im], axis=0))
    return ys, h_re[V7X_SUBLANES - nb:], h_im[V7X_SUBLANES - nb:]


def _cast_kernel(x_ref, o_ref):
    o_ref[...] = x_ref[...].astype(BF16)


def _cast_t_kernel(x_ref, o_ref):
    o_ref[...] = x_ref[...].T.astype(BF16)


def _to_bf16(x, transpose):
    r, c = x.shape
    tr, tc = PEER_TE, 2048
    if transpose:
        return pl.pallas_call(
            _cast_t_kernel, grid=(r // tr, c // tc),
            in_specs=[pl.BlockSpec((tr, tc), lambda i, j: (i, j))],
            out_specs=pl.BlockSpec((None, tc, tr), lambda i, j: (i, j, 0)),
            out_shape=jax.ShapeDtypeStruct((r // tr, c, tr), BF16),
            compiler_params=_params("parallel", "parallel"), name="cast_transpose")(x)
    return pl.pallas_call(
        _cast_kernel, grid=(r // tr, c // tc),
        in_specs=[pl.BlockSpec((tr, tc), lambda i, j: (i, j))],
        out_specs=pl.BlockSpec((tr, tc), lambda i, j: (i, j)),
        out_shape=jax.ShapeDtypeStruct((r, c), BF16),
        compiler_params=_params("parallel", "parallel"), name="cast_bf16")(x)


_STAIR_LIMIT = [PEER_TOPK // (a + 1) for a in range(PEER_TOPK)]


def _extract_top(work, order, count, unranked=None):
    big = PEER_UNRANKED
    vals, picks = [], []
    rank = None if unranked is None else jnp.full(work.shape, unranked, F32)
    for r in range(count):
        m = jnp.max(work, axis=0, keepdims=True)
        pick = jnp.min(jnp.where(work == m, order, big), axis=0, keepdims=True)
        hit = order == pick
        work = jnp.where(hit, -jnp.inf, work)
        if rank is not None:
            rank = jnp.where(hit, float(r), rank)
        vals.append(m)
        picks.append(pick)
    return vals, picks, rank


def _stack_rows(rows, row_idx):
    out = jnp.broadcast_to(rows[0], row_idx.shape)
    for r in range(1, len(rows)):
        out = jnp.where(row_idx == r, rows[r], out)
    return out


def _topk_kernel(q_ref, keys_ref, e1n_ref, lim_ref, e2_ref, r2_ref, *, tm):
    key_idx = lax.broadcasted_iota(jnp.int32, (PEER_NKEYS, tm), 0).astype(F32)
    iota16 = lax.broadcasted_iota(jnp.int32, (PEER_TOPK, tm), 0).astype(F32)
    iota8 = iota16[:8]
    flat = [iota16] + [a * PEER_TOPK + iota8 for a in range(1, 8)] + [(iota8 + 8) * PEER_TOPK]
    flat = jnp.concatenate(flat, axis=0)
    for h in range(PEER_HEADS):
        scores, tops, ranks = [], [], []
        for s in range(2):
            col = (2 * h + s) * PEER_SUBDIM
            qs = q_ref[:, col:col + PEER_SUBDIM].astype(BF16)
            ks = keys_ref[h, s].astype(BF16)
            sc = lax.dot_general(ks, qs, (((1,), (1,)), ((), ())), preferred_element_type=F32)
            vals, _, rank = _extract_top(sc, key_idx, PEER_TOPK, unranked=(-1.0, PEER_UNRANKED)[s])
            scores.append(sc)
            tops.append(vals)
            ranks.append(rank)
        v1, v2 = tops
        v2_16 = _stack_rows(v2, iota16)
        v2_8 = v2_16[:8]
        cand = [v1[0] + v2_16]
        for a in range(1, 8):
            cand.append(jnp.where(iota8 < _STAIR_LIMIT[a], v1[a] + v2_8, -jnp.inf))
        cand.append(_stack_rows(v1[8:], iota8) + v2[0])
        cand = jnp.concatenate(cand, axis=0)
        cv, picks, _ = _extract_top(cand, flat, PEER_TOPK)
        z = jnp.ones_like(cv[0])
        for r in range(1, PEER_TOPK):
            z = z + jnp.exp(cv[r] - cv[0])
        limit = jnp.full((PEER_TOPK, tm), -1.0, F32)
        for pick in picks:
            row = jnp.floor(pick * (1.0 / PEER_TOPK))
            col = pick - PEER_TOPK * row
            limit = jnp.maximum(limit, jnp.where(iota16 == row, col, -1.0))
        lim = jnp.full((PEER_NKEYS, tm), -1.0, F32)
        for a in range(PEER_TOPK):
            lim = jnp.where(ranks[0] == float(a), limit[a:a + 1, :], lim)
        e1n_ref[h] = jnp.exp(scores[0] - v1[0]) / z
        lim_ref[h] = lim
        e2_ref[h] = jnp.exp(scores[1] - v2[0]).astype(BF16)
        r2_ref[h] = ranks[1].astype(BF16)


def _peer_topk(q, keys, tm):
    n = q.shape[0]
    per_key = pl.BlockSpec((PEER_HEADS, PEER_NKEYS, tm), lambda i: (0, 0, i))
    shape = (PEER_HEADS, PEER_NKEYS, n)
    return pl.pallas_call(
        functools.partial(_topk_kernel, tm=tm),
        grid=(n // tm,),
        in_specs=[pl.BlockSpec((tm, q.shape[1]), lambda i: (i, 0)),
                  pl.BlockSpec(keys.shape, lambda i: (0, 0, 0, 0))],
        out_specs=[per_key] * 4,
        out_shape=[jax.ShapeDtypeStruct(shape, F32), jax.ShapeDtypeStruct(shape, F32),
                   jax.ShapeDtypeStruct(shape, BF16), jax.ShapeDtypeStruct(shape, BF16)],
        compiler_params=_params("parallel"),
        name="peer_topk",
    )(q, keys)


def _peer_kernel(ht_ref, u_ref, vt_ref, e1n_ref, lim_ref, e2_ref, r2_ref,
                 o_ref, act_a, act_b, p_a, p_b, row_bc, *, te, nblocks, total):
    j = pl.program_id(0)
    slabs = te // PEER_NKEYS

    tm = o_ref.shape[1]
    tile = (V7X_BF16_ROWS, tm)

    @pl.when(j == 0)
    def _():
        p_a[...] = jnp.zeros_like(p_a)
        act_b[...] = jnp.zeros_like(act_b)

    @pl.when(jnp.maximum(j - 2, 0) % nblocks == 0)
    def _():
        o_ref[...] = jnp.zeros_like(o_ref)

    def step(act_new, act_old, p_new, p_old):
        first = (jnp.clip(j - 1, 0, total - 1) % nblocks) * slabs
        d = ht_ref.shape[0]

        def gate(il, r):
            i1 = first + il
            per_head = []
            for h in range(PEER_HEADS):
                if r == 0:
                    rows_i1 = [jnp.broadcast_to(ref[h, pl.ds(i1, 1), :], tile).astype(BF16)
                               for ref in (e1n_ref, lim_ref)]
                    for a, val in enumerate(rows_i1):
                        row_bc[a, il, h] = val
                else:
                    rows_i1 = [row_bc[a, il, h] for a in range(2)]
                per_head.append(rows_i1)
            for q in range(PEER_GATE_ROWS // V7X_BF16_ROWS):
                k0 = r * PEER_GATE_ROWS + q * V7X_BF16_ROWS
                keys = slice(k0, k0 + V7X_BF16_ROWS)
                w = jnp.zeros(tile, BF16)
                for h in range(PEER_HEADS):
                    e1, lim = per_head[h]
                    w = w + jnp.where(r2_ref[h, keys, :] <= lim, e2_ref[h, keys, :] * e1, jnp.zeros((), BF16))
                rows = slice(il * PEER_NKEYS + k0, il * PEER_NKEYS + k0 + V7X_BF16_ROWS)
                p_new[rows, :] = w * jax.nn.gelu(act_old[rows, :]).astype(BF16)

        def pre_act(kc):
            ks = slice(kc * PEER_K_CHUNK, (kc + 1) * PEER_K_CHUNK)
            part = jnp.dot(u_ref[:, ks], ht_ref[ks, :], preferred_element_type=F32)
            if kc == 0:
                act_new[...] = part
            else:
                act_new[...] += part

        def accumulate(mc):
            ms = slice(mc * PEER_M_CHUNK, (mc + 1) * PEER_M_CHUNK)
            o_ref[ms, :] += jnp.dot(vt_ref[ms, :], p_old[...], preferred_element_type=F32)

        gates = [(il, r) for il in range(slabs) for r in range(PEER_NKEYS // PEER_GATE_ROWS)]
        n_k, n_m = d // PEER_K_CHUNK, d // PEER_M_CHUNK
        mxu = []
        for kc in range(n_k):
            mxu.append((functools.partial(pre_act, kc), PEER_K_CHUNK * te))
            mxu += [(functools.partial(accumulate, kc * (n_m // n_k) + m), PEER_M_CHUNK * te)
                    for m in range(n_m // n_k)]
        mxu_total = sum(cost for _, cost in mxu)
        issued = done = 0
        for piece, cost in mxu:
            piece()
            issued += cost
            upto = (len(gates) * issued) // mxu_total
            for il, r in gates[done:upto]:
                gate(il, r)
            done = upto

    @pl.when(j % 2 == 0)
    def _():
        step(act_a, act_b, p_b, p_a)

    @pl.when(j % 2 == 1)
    def _():
        step(act_b, act_a, p_a, p_b)


def _peer_dense(ht, u_bf, vt_bf, topk, tm):
    d, n = ht.shape
    nblocks, _, te = vt_bf.shape
    total = (n // tm) * nblocks
    item = lambda f, lag: jnp.clip(f - lag, 0, total - 1)
    per_tile = dict(pipeline_mode=pl.Buffered(1))
    per_key = pl.BlockSpec((PEER_HEADS, PEER_NKEYS, tm), lambda f: (0, 0, item(f, 1) // nblocks), **per_tile)
    return pl.pallas_call(
        functools.partial(_peer_kernel, te=te, nblocks=nblocks, total=total),
        grid=(total + 2,),
        in_specs=[pl.BlockSpec((d, tm), lambda f: (0, item(f, 0) // nblocks), **per_tile),
                  pl.BlockSpec((te, d), lambda f: (item(f, 0) % nblocks, 0)),
                  pl.BlockSpec((None, d, te), lambda f: (item(f, 2) % nblocks, 0, 0))]
                 + [per_key] * 4,
        out_specs=pl.BlockSpec((d, tm), lambda f: (0, item(f, 2) // nblocks)),
        out_shape=jax.ShapeDtypeStruct((d, n), F32),
        scratch_shapes=[pltpu.VMEM((te, tm), F32), pltpu.VMEM((te, tm), F32),
                        pltpu.VMEM((te, tm), BF16), pltpu.VMEM((te, tm), BF16),
                        pltpu.VMEM((2, te // PEER_NKEYS, PEER_HEADS, V7X_BF16_ROWS, tm), BF16)],
        compiler_params=_params("arbitrary"),
        name="peer_dense",
    )(ht, u_bf, vt_bf, *topk)


def _split_mod(mod, tokens_per_stream, per_token):
    if per_token:
        mod = jnp.repeat(mod, tokens_per_stream, axis=0)[None]
    else:
        mod = mod[:, None, :]
    return jnp.split(mod, 6, axis=-1)


def _encoder_layer(x, mod, h0_re, h0_im, lp, peer_w, *, tr, scan_steps, mixer_rows, per_token_mod, emit_v):
    (g_pre_tok, g_post_tok, g_pre_ch, g_post_ch, w_in, a_ws, a_bs, a_ln_g, a_ln_b, disc, s5_d,
     w_glu, b_glu, w_branch, w_out, peer_w_query, peer_sub_keys) = lp
    u_bf, vt_bf = peer_w
    nb, t, d = x.shape
    n = nb * t
    tiles_per_group = 1 if per_token_mod else t // tr
    sh_t, sc_t, gt_t, sh_c, sc_c, gt_c = _split_mod(mod, t, per_token_mod)
    x2 = x.reshape(n, d)

    h = _prenorm(x2, g_pre_tok, sc_t, sh_t, tr, tiles_per_group)
    proj = _matmul(h, w_in, 1024, 512, name="in_proj")
    proj3 = proj.reshape(nb, t, IN_WIDTH)
    mix = _mixer_a(proj3, a_ws, a_bs, a_ln_g, a_ln_b, mixer_rows, emit_v)
    ya = mix[0].reshape(n, A_WIDTH)
    v_rows = mix[1] if emit_v else None
    ys, h_re, h_im = _s5(proj3, disc, s5_d, h0_re, h0_im, scan_steps)
    yb = _glu(ys.reshape(n, B_WIDTH), w_glu, b_glu, 1024, 512)
    merged = _branch_merge(ya, yb, w_branch, proj, 1024, 512)
    m = _matmul(merged, w_out, 1024, 512, name="out_proj")

    x1, h2, h2t = _resid_prenorm(x2, m, g_post_tok, gt_t, g_pre_ch, sc_c, sh_c, tr, tiles_per_group)
    q = _matmul(h2, peer_w_query, 1024, 512, name="peer_query")
    topk = _peer_topk(q, peer_sub_keys, min(128, n))
    ft = _peer_dense(h2t, u_bf, vt_bf, topk, min(512, n))
    y = _final_resid(x1, ft, g_post_ch, gt_c, tr, tiles_per_group)
    return (y.reshape(nb, t, d), h_re.reshape(nb, B_GROUPS, B_STATE), h_im.reshape(nb, B_GROUPS, B_STATE), v_rows)


def kernel(x_prompt, x_sample, c_prompt, c_sample, state_ssm_re, state_ssm_im, w_ada, b_ada, g_pre_tok,
           g_post_tok, g_pre_ch, g_post_ch, w_in, a_ws, a_bs, a_ln_g, a_ln_b, s5_lam_re, s5_lam_im,
           s5_log_dt, s5_b_re, s5_b_im, s5_c_re, s5_c_im, s5_d, w_glu, b_glu, w_branch, w_out,
           peer_w_query, peer_sub_keys, peer_u, peer_v):
    depth = w_ada.shape[0]
    nb_p, t_p, _ = x_prompt.shape
    nb_s, t_s, _ = x_sample.shape
    y_p, y_s = x_prompt, x_sample
    re_p, im_p, re_s, im_s, v_s = [], [], [], [], []
    pad = (-(nb_p + nb_s)) % 8
    for l in range(depth):
        c_all = jnp.concatenate([c_prompt, c_sample, jnp.zeros((pad, D_MODEL), F32)], axis=0)
        mod = _ada(c_all, w_ada[l], b_ada[l])
        disc = _s5_discretize(s5_lam_re[l], s5_lam_im[l], s5_log_dt[l], s5_b_re[l], s5_b_im[l],
                              s5_c_re[l], s5_c_im[l])
        lp = (g_pre_tok[l], g_post_tok[l], g_pre_ch[l], g_post_ch[l], w_in[l], a_ws[l], a_bs[l],
              a_ln_g[l], a_ln_b[l], disc, s5_d[l], w_glu[l], b_glu[l], w_branch[l], w_out[l],
              peer_w_query[l], peer_sub_keys[l])
        peer_w = (_to_bf16(peer_u[l], transpose=False), _to_bf16(peer_v[l], transpose=True))
        zeros = jnp.zeros((nb_p, S5_STATES), F32)
        y_p, hr, hi, _ = _encoder_layer(
            y_p, mod[:nb_p], zeros, zeros, lp, peer_w,
            tr=256, scan_steps=32, mixer_rows=A_CHUNK, per_token_mod=False, emit_v=False)
        re_p.append(hr)
        im_p.append(hi)
        y_s, hr, hi, v_rows = _encoder_layer(
            y_s, mod[nb_p:nb_p + nb_s], state_ssm_re[l].reshape(nb_s, S5_STATES),
            state_ssm_im[l].reshape(nb_s, S5_STATES), lp, peer_w,
            tr=nb_s * t_s, scan_steps=t_s, mixer_rows=t_s, per_token_mod=True, emit_v=True)
        re_s.append(hr)
        im_s.append(hi)
        v_s.append(v_rows)
    return (y_p, y_s, jnp.stack(re_p), jnp.stack(im_p), jnp.stack(re_s), jnp.stack(im_s), jnp.stack(v_s))
```

```python
import functools

import jax
import jax.numpy as jnp
from jax import lax
from jax.experimental import pallas as pl
from jax.experimental.pallas import tpu as pltpu

F32 = jnp.float32
BF16 = jnp.bfloat16

D_MODEL = 4096
CHUNK = 64
A_CHUNK = 128
A_GROUPS = 16
A_GROUP_DIM = 128
A_WIDTH = A_GROUPS * A_GROUP_DIM
B_GROUP_DIM = 16
B_GROUPS = 128
B_WIDTH = B_GROUPS * B_GROUP_DIM
B_STATE = 64
S5_STATES = B_GROUPS * B_STATE
S5_GROUPS_PER_BLOCK = 8
S5_BLOCKS = B_GROUPS // S5_GROUPS_PER_BLOCK
S5_BLOCK_STATES = S5_GROUPS_PER_BLOCK * B_STATE
IN_WIDTH = 2 * A_WIDTH + B_WIDTH + 2 * D_MODEL
PEER_HEADS = 8
PEER_SUBDIM = 128
PEER_NKEYS = 128
PEER_EXPERTS = PEER_NKEYS * PEER_NKEYS
PEER_TOPK = 16
PEER_UNRANKED = 1e9
PEER_TE = 512
PEER_GATE_ROWS = 16
PEER_K_CHUNK = 256
PEER_M_CHUNK = 128
EPS = 1e-6

V7X_LANES = 128
V7X_SUBLANES = 8
V7X_BF16_ROWS = 16
V7X_VMEM_BYTES = 64 * 1024 * 1024
VMEM_LIMIT_BYTES = 56 * 1024 * 1024


def _params(*semantics, flags=None):
    return pltpu.CompilerParams(dimension_semantics=semantics, vmem_limit_bytes=VMEM_LIMIT_BYTES, flags=flags)


def _rms(x):
    return x * lax.rsqrt(jnp.mean(x * x, axis=-1, keepdims=True) + EPS)


def _ada_kernel(c_ref, w_ref, b_ref, o_ref):
    s = jax.nn.silu(c_ref[...]).astype(BF16)
    o_ref[...] = jnp.dot(s, w_ref[...].astype(BF16), preferred_element_type=F32) + b_ref[...]


def _ada(c, w, b):
    rows, d = c.shape
    n = w.shape[1]
    tn = 512
    return pl.pallas_call(
        _ada_kernel,
        grid=(n // tn,),
        in_specs=[pl.BlockSpec((rows, d), lambda j: (0, 0)),
                  pl.BlockSpec((d, tn), lambda j: (0, j)),
                  pl.BlockSpec((1, tn), lambda j: (0, j))],
        out_specs=pl.BlockSpec((rows, tn), lambda j: (0, j)),
        out_shape=jax.ShapeDtypeStruct((rows, n), F32),
        compiler_params=_params("parallel"),
        name="ada_mod",
    )(c, w, b.reshape(1, n))


def _mod_spec(mod, tiles_per_group):
    return pl.BlockSpec((None,) + mod.shape[1:], lambda i: (i // tiles_per_group, 0, 0))


def _prenorm_kernel(x_ref, g_ref, sc_ref, sh_ref, o_ref):
    h = _rms(x_ref[...]) * g_ref[...] * (1.0 + sc_ref[...]) + sh_ref[...]
    o_ref[...] = h.astype(BF16)


def _prenorm(x, g, sc, sh, tr, tiles_per_group):
    n, d = x.shape
    return pl.pallas_call(
        _prenorm_kernel,
        grid=(n // tr,),
        in_specs=[pl.BlockSpec((tr, d), lambda i: (i, 0)),
                  pl.BlockSpec((1, d), lambda i: (0, 0)),
                  _mod_spec(sc, tiles_per_group),
                  _mod_spec(sh, tiles_per_group)],
        out_specs=pl.BlockSpec((tr, d), lambda i: (i, 0)),
        out_shape=jax.ShapeDtypeStruct((n, d), BF16),
        compiler_params=_params("parallel"),
        name="prenorm_tok",
    )(x, g.reshape(1, d), sc, sh)


def _resid_kernel(x_ref, m_ref, gpost_ref, gt_ref, gpre_ref, sc_ref, sh_ref, x1_ref, h_ref, ht_ref):
    x1 = x_ref[...] + gt_ref[...] * (_rms(m_ref[...]) * gpost_ref[...])
    x1_ref[...] = x1
    h = _rms(x1) * gpre_ref[...] * (1.0 + sc_ref[...]) + sh_ref[...]
    h_ref[...] = h.astype(BF16)
    ht_ref[...] = h.T.astype(BF16)


def _resid_prenorm(x, m, gpost, gt, gpre, sc, sh, tr, tiles_per_group):
    n, d = x.shape
    row = pl.BlockSpec((tr, d), lambda i: (i, 0))
    vec = pl.BlockSpec((1, d), lambda i: (0, 0))
    return pl.pallas_call(
        _resid_kernel,
        grid=(n // tr,),
        in_specs=[row, row, vec, _mod_spec(gt, tiles_per_group), vec,
                  _mod_spec(sc, tiles_per_group), _mod_spec(sh, tiles_per_group)],
        out_specs=[row, row, pl.BlockSpec((d, tr), lambda i: (0, i))],
        out_shape=[jax.ShapeDtypeStruct((n, d), F32),
                   jax.ShapeDtypeStruct((n, d), BF16),
                   jax.ShapeDtypeStruct((d, n), BF16)],
        compiler_params=_params("parallel"),
        name="resid_prenorm_ch",
    )(x, m, gpost.reshape(1, d), gt, gpre.reshape(1, d), sc, sh)


def _final_kernel(x_ref, ft_ref, gpost_ref, gt_ref, o_ref):
    f = ft_ref[...].T
    o_ref[...] = x_ref[...] + gt_ref[...] * (_rms(f) * gpost_ref[...])


def _final_resid(x, ft, gpost, gt, tr, tiles_per_group):
    n, d = x.shape
    row = pl.BlockSpec((tr, d), lambda i: (i, 0))
    return pl.pallas_call(
        _final_kernel,
        grid=(n // tr,),
        in_specs=[row, pl.BlockSpec((d, tr), lambda i: (0, i)),
                  pl.BlockSpec((1, d), lambda i: (0, 0)), _mod_spec(gt, tiles_per_group)],
        out_specs=row,
        out_shape=jax.ShapeDtypeStruct((n, d), F32),
        compiler_params=_params("parallel"),
        name="final_resid",
    )(x, ft, gpost.reshape(1, d), gt)


def _mm_kernel(a_ref, w_ref, o_ref):
    o_ref[...] = jnp.dot(a_ref[...].astype(BF16), w_ref[...].astype(BF16),
                         preferred_element_type=F32).astype(o_ref.dtype)


def _matmul(a, w, tm, tn, out_dtype=F32, name="matmul"):
    m, k = a.shape
    n = w.shape[1]
    tm = min(tm, m)
    return pl.pallas_call(
        _mm_kernel,
        grid=(m // tm, n // tn),
        in_specs=[pl.BlockSpec((tm, k), lambda i, j: (i, 0)),
                  pl.BlockSpec((k, tn), lambda i, j: (0, j))],
        out_specs=pl.BlockSpec((tm, tn), lambda i, j: (i, j)),
        out_shape=jax.ShapeDtypeStruct((m, n), out_dtype),
        compiler_params=_params("parallel", "parallel"),
        name=name,
    )(a, w)


def _glu_kernel(y_ref, w_ref, b_ref, yj_ref, o_ref):
    acc = jnp.dot(y_ref[...].astype(BF16), w_ref[...].astype(BF16), preferred_element_type=F32)
    o_ref[...] = (yj_ref[...] * jax.nn.sigmoid(acc + b_ref[...])).astype(BF16)


def _glu(y, w, b, tm, tn):
    m, k = y.shape
    n = w.shape[1]
    tm = min(tm, m)
    return pl.pallas_call(
        _glu_kernel,
        grid=(m // tm, n // tn),
        in_specs=[pl.BlockSpec((tm, k), lambda i, j: (i, 0)),
                  pl.BlockSpec((k, tn), lambda i, j: (0, j)),
                  pl.BlockSpec((1, tn), lambda i, j: (0, j)),
                  pl.BlockSpec((tm, tn), lambda i, j: (i, j))],
        out_specs=pl.BlockSpec((tm, tn), lambda i, j: (i, j)),
        out_shape=jax.ShapeDtypeStruct((m, n), BF16),
        compiler_params=_params("parallel", "parallel"),
        name="s5_glu",
    )(y, w, b.reshape(1, n), y)


def _branch_kernel(ya_ref, yb_ref, wa_ref, wb_ref, ga_ref, gb_ref, o_ref):
    a = jnp.dot(ya_ref[...], wa_ref[...].astype(BF16), preferred_element_type=F32)
    b = jnp.dot(yb_ref[...], wb_ref[...].astype(BF16), preferred_element_type=F32)
    o_ref[...] = (jax.nn.sigmoid(ga_ref[...]) * a + jax.nn.sigmoid(gb_ref[...]) * b).astype(BF16)


def _branch_merge(ya, yb, w_branch, proj, tm, tn):
    m = ya.shape[0]
    n = w_branch.shape[1]
    tm = min(tm, m)
    ga_blk = (2 * A_WIDTH + B_WIDTH) // tn
    gb_blk = ga_blk + D_MODEL // tn
    return pl.pallas_call(
        _branch_kernel,
        grid=(n // tn, m // tm),
        in_specs=[pl.BlockSpec((tm, A_WIDTH), lambda j, i: (i, 0)),
                  pl.BlockSpec((tm, B_WIDTH), lambda j, i: (i, 0)),
                  pl.BlockSpec((A_WIDTH, tn), lambda j, i: (0, j)),
                  pl.BlockSpec((B_WIDTH, tn), lambda j, i: (A_WIDTH // B_WIDTH, j)),
                  pl.BlockSpec((tm, tn), lambda j, i: (i, ga_blk + j)),
                  pl.BlockSpec((tm, tn), lambda j, i: (i, gb_blk + j))],
        out_specs=pl.BlockSpec((tm, tn), lambda j, i: (i, j)),
        out_shape=jax.ShapeDtypeStruct((m, n), BF16),
        compiler_params=_params("parallel", "parallel"),
        name="branch_merge",
    )(ya, yb, w_branch, w_branch, proj, proj)


def _mixer_a_kernel(za_ref, ws_ref, bias_ref, lng_ref, lnb_ref, ya_ref, *v_refs, rows):
    z = jax.nn.gelu(za_ref[...])
    u = z[:, :A_WIDTH]
    v = z[:, A_WIDTH:]
    d = v - jnp.mean(v, axis=-1, keepdims=True)
    vn = d * lax.rsqrt(jnp.mean(d * d, axis=-1, keepdims=True) + EPS) * lng_ref[...] + lnb_ref[...]
    if v_refs:
        v_refs[0][...] = vn
    if rows < A_CHUNK:
        vn = jnp.concatenate([vn, jnp.zeros((A_CHUNK - rows, A_WIDTH), F32)], axis=0)
    vb = vn.astype(BF16)
    blk_i = lax.broadcasted_iota(jnp.int32, (A_CHUNK, A_CHUNK), 0) // CHUNK
    blk_j = lax.broadcasted_iota(jnp.int32, (A_CHUNK, A_CHUNK), 1) // CHUNK
    causal = blk_i >= blk_j
    for g in range(A_GROUPS):
        cols = slice(g * A_GROUP_DIM, (g + 1) * A_GROUP_DIM)
        w = jnp.where(causal, ws_ref[g], 0.0).astype(BF16)
        mixed = jnp.dot(w, vb[:, cols], preferred_element_type=F32) + bias_ref[:, cols]
        ya_ref[:, cols] = (u[:, cols] * mixed[:rows]).astype(BF16)


def _mixer_a(proj3, ws, bs, ln_g, ln_b, rows, emit_v):
    nb, t, _ = proj3.shape
    bias = jnp.repeat(bs.T, A_GROUP_DIM, axis=1)
    out_shape = [jax.ShapeDtypeStruct((nb, t, A_WIDTH), BF16)]
    out_specs = [pl.BlockSpec((None, rows, A_WIDTH), lambda b, c: (b, c, 0))]
    if emit_v:
        out_shape.append(jax.ShapeDtypeStruct((nb, t, A_WIDTH), F32))
        out_specs.append(pl.BlockSpec((None, rows, A_WIDTH), lambda b, c: (b, c, 0)))
    return pl.pallas_call(
        functools.partial(_mixer_a_kernel, rows=rows),
        grid=(nb, t // rows),
        in_specs=[pl.BlockSpec((None, rows, 2 * A_WIDTH), lambda b, c: (b, c, 0)),
                  pl.BlockSpec((A_GROUPS, A_CHUNK, A_CHUNK), lambda b, c: (0, 0, 0)),
                  pl.BlockSpec((A_CHUNK, A_WIDTH), lambda b, c: (0, 0)),
                  pl.BlockSpec((1, A_WIDTH), lambda b, c: (0, 0)),
                  pl.BlockSpec((1, A_WIDTH), lambda b, c: (0, 0))],
        out_specs=out_specs,
        out_shape=out_shape,
        compiler_params=_params("parallel", "parallel"),
        name="mixer_a",
    )(proj3, ws, bias, ln_g.reshape(1, A_WIDTH), ln_b.reshape(1, A_WIDTH))


def _s5_discretize(lam_re, lam_im, log_dt, b_re, b_im, c_re, c_im):
    dt = jnp.exp(log_dt.astype(F32))[:, None]
    lr, li = lam_re.astype(F32), lam_im.astype(F32)
    mag = jnp.exp(lr * dt)
    ab_re, ab_im = mag * jnp.cos(li * dt), mag * jnp.sin(li * dt)
    den = lr * lr + li * li
    num_re = ab_re - 1.0
    coef_re = (num_re * lr + ab_im * li) / den
    coef_im = (ab_im * lr - num_re * li) / den
    br, bi = b_re.astype(F32), b_im.astype(F32)
    bb_re = coef_re[..., None] * br - coef_im[..., None] * bi
    bb_im = coef_re[..., None] * bi + coef_im[..., None] * br
    eye = jnp.eye(S5_GROUPS_PER_BLOCK, dtype=F32)

    def pack_in(bb):
        t = bb.reshape(S5_BLOCKS, S5_GROUPS_PER_BLOCK, B_STATE, B_GROUP_DIM).transpose(0, 1, 3, 2)
        return jnp.einsum("kgcp,gm->kgcmp", t, eye).reshape(S5_BLOCKS, V7X_LANES, S5_BLOCK_STATES)

    def pack_out(cc):
        t = cc.reshape(S5_BLOCKS, S5_GROUPS_PER_BLOCK, B_GROUP_DIM, B_STATE).transpose(0, 1, 3, 2)
        return jnp.einsum("kgpc,gm->kgpmc", t, eye).reshape(S5_BLOCKS, S5_BLOCK_STATES, V7X_LANES)

    w_in = jnp.concatenate([pack_in(bb_re), pack_in(bb_im)], axis=-1).astype(BF16)
    w_out = jnp.concatenate([pack_out(c_re.astype(F32)), -pack_out(c_im.astype(F32))], axis=1).astype(BF16)
    return ab_re.reshape(1, S5_STATES), ab_im.reshape(1, S5_STATES), w_in, w_out


def _s5_kernel(xb_ref, perm_ref, unperm_ref, win_ref, wout_ref, are_ref, aim_ref, d_ref, h0r_ref, h0i_ref,
               ys_ref, hr_ref, hi_ref, sre, sim, g_tm, *, nb):
    @pl.when(pl.program_id(0) == 0)
    def _():
        hr_ref[...] = h0r_ref[...]
        hi_ref[...] = h0i_ref[...]

    x_sm = xb_ref[...].reshape(perm_ref.shape[0], B_WIDTH)
    x_hi = x_sm.astype(BF16)
    rest = x_sm - x_hi.astype(F32)
    x_mid = rest.astype(BF16)
    x_lo = (rest - x_mid.astype(F32)).astype(BF16)
    perm = perm_ref[...]
    x_top = jnp.dot(perm, x_hi, preferred_element_type=F32)
    x = (x_top + jnp.dot(perm, x_mid, preferred_element_type=F32)
         + jnp.dot(perm, x_lo, preferred_element_type=F32))
    xbf = x_top.astype(BF16)
    for k in range(S5_BLOCKS):
        drive = jnp.dot(xbf[:, k * V7X_LANES:(k + 1) * V7X_LANES], win_ref[k], preferred_element_type=F32)
        sre[:, k * S5_BLOCK_STATES:(k + 1) * S5_BLOCK_STATES] = drive[:, :S5_BLOCK_STATES]
        sim[:, k * S5_BLOCK_STATES:(k + 1) * S5_BLOCK_STATES] = drive[:, S5_BLOCK_STATES:]

    a_re = are_ref[...]
    a_im = aim_ref[...]
    sub = lax.broadcasted_iota(jnp.int32, (V7X_SUBLANES, S5_STATES), 0)
    for i in range(x.shape[0] // V7X_SUBLANES):
        rows = slice(i * V7X_SUBLANES, (i + 1) * V7X_SUBLANES)
        prev = slice((i - 1) * V7X_SUBLANES, i * V7X_SUBLANES)
        c_re = hr_ref[...] if i == 0 else sre[prev, :]
        c_im = hi_ref[...] if i == 0 else sim[prev, :]
        d_re = sre[rows, :]
        d_im = sim[rows, :]
        o_re = o_im = None
        for s in range(V7X_SUBLANES // nb):
            if nb < V7X_SUBLANES:
                c_re = pltpu.roll(c_re, nb, 0)
                c_im = pltpu.roll(c_im, nb, 0)
            n_re = a_re * c_re - a_im * c_im + d_re
            n_im = a_re * c_im + a_im * c_re + d_im
            o_re = n_re if s == 0 else jnp.where(sub >= s * nb, n_re, o_re)
            o_im = n_im if s == 0 else jnp.where(sub >= s * nb, n_im, o_im)
            c_re, c_im = n_re, n_im
        sre[rows, :] = o_re
        sim[rows, :] = o_im
    last = slice(x.shape[0] - V7X_SUBLANES, x.shape[0])
    hr_ref[...] = sre[last, :]
    hi_ref[...] = sim[last, :]

    for k in range(S5_BLOCKS):
        st = slice(k * S5_BLOCK_STATES, (k + 1) * S5_BLOCK_STATES)
        ch = slice(k * V7X_LANES, (k + 1) * V7X_LANES)
        y = (jnp.dot(sre[:, st].astype(BF16), wout_ref[k, :S5_BLOCK_STATES, :], preferred_element_type=F32)
             + jnp.dot(sim[:, st].astype(BF16), wout_ref[k, S5_BLOCK_STATES:, :], preferred_element_type=F32)
             + d_ref[:, ch] * x[:, ch])
        g_tm[:, ch] = jax.nn.gelu(y).astype(BF16)
    g_sm = jnp.dot(unperm_ref[...], g_tm[...], preferred_element_type=F32).astype(BF16)
    ys_ref[...] = g_sm.reshape(ys_ref.shape)


def _s5(proj3, disc, d_skip, h0_re, h0_im, steps):
    nb, t, _ = proj3.shape
    a_re, a_im, w_in, w_out = disc
    const2 = lambda c: (0, 0)
    const3 = lambda c: (0, 0, 0)
    blk = steps * nb
    src = jnp.arange(blk)
    perm = jax.nn.one_hot((src % nb) * steps + src // nb, blk, dtype=BF16)
    pad = jnp.zeros((V7X_SUBLANES - nb, S5_STATES), F32)
    state = pl.BlockSpec((V7X_SUBLANES, S5_STATES), const2)
    xb_blk = 2 * A_WIDTH // B_WIDTH
    ys, h_re, h_im = pl.pallas_call(
        functools.partial(_s5_kernel, nb=nb),
        grid=(t // steps,),
        in_specs=[pl.BlockSpec((nb, steps, B_WIDTH), lambda c: (0, c, xb_blk)),
                  pl.BlockSpec((blk, blk), const2),
                  pl.BlockSpec((blk, blk), const2),
                  pl.BlockSpec(w_in.shape, const3),
                  pl.BlockSpec(w_out.shape, const3),
                  pl.BlockSpec((1, S5_STATES), const2),
                  pl.BlockSpec((1, S5_STATES), const2),
                  pl.BlockSpec((1, B_WIDTH), const2),
                  state, state],
        out_specs=[pl.BlockSpec((nb, steps, B_WIDTH), lambda c: (0, c, 0)), state, state],
        out_shape=[jax.ShapeDtypeStruct((nb, t, B_WIDTH), BF16),
                   jax.ShapeDtypeStruct((V7X_SUBLANES, S5_STATES), F32),
                   jax.ShapeDtypeStruct((V7X_SUBLANES, S5_STATES), F32)],
        scratch_shapes=[pltpu.VMEM((blk, S5_STATES), F32),
                        pltpu.VMEM((blk, S5_STATES), F32),
                        pltpu.VMEM((blk, B_WIDTH), BF16)],
        compiler_params=_params("arbitrary"),
        name="s5_scan",
    )(proj3, perm, perm.T, w_in, w_out, a_re, a_im, d_skip.reshape(1, B_WIDTH),
      jnp.concatenate([pad, h0_re], axis=0), jnp.concatenate([pad, h0_im], axis=0))
    return ys, h_re[V7X_SUBLANES - nb:], h_im[V7X_SUBLANES - nb:]


def _cast_kernel(x_ref, o_ref):
    o_ref[...] = x_ref[...].astype(BF16)


def _cast_t_kernel(x_ref, o_ref):
    o_ref[...] = x_ref[...].T.astype(BF16)


def _to_bf16(x, transpose):
    r, c = x.shape
    tr, tc = PEER_TE, 2048
    if transpose:
        return pl.pallas_call(
            _cast_t_kernel, grid=(r // tr, c // tc),
            in_specs=[pl.BlockSpec((tr, tc), lambda i, j: (i, j))],
            out_specs=pl.BlockSpec((None, tc, tr), lambda i, j: (i, j, 0)),
            out_shape=jax.ShapeDtypeStruct((r // tr, c, tr), BF16),
            compiler_params=_params("parallel", "parallel"), name="cast_transpose")(x)
    return pl.pallas_call(
        _cast_kernel, grid=(r // tr, c // tc),
        in_specs=[pl.BlockSpec((tr, tc), lambda i, j: (i, j))],
        out_specs=pl.BlockSpec((tr, tc), lambda i, j: (i, j)),
        out_shape=jax.ShapeDtypeStruct((r, c), BF16),
        compiler_params=_params("parallel", "parallel"), name="cast_bf16")(x)


_STAIR_LIMIT = [PEER_TOPK // (a + 1) for a in range(PEER_TOPK)]


def _extract_top(work, order, count, unranked=None):
    big = PEER_UNRANKED
    vals, picks = [], []
    rank = None if unranked is None else jnp.full(work.shape, unranked, F32)
    for r in range(count):
        m = jnp.max(work, axis=0, keepdims=True)
        pick = jnp.min(jnp.where(work == m, order, big), axis=0, keepdims=True)
        hit = order == pick
        work = jnp.where(hit, -jnp.inf, work)
        if rank is not None:
            rank = jnp.where(hit, float(r), rank)
        vals.append(m)
        picks.append(pick)
    return vals, picks, rank


def _extract_top_untied(work, count, unranked):
    vals = []
    rank = jnp.full(work.shape, unranked, F32)
    for r in range(count):
        m = jnp.max(work, axis=0, keepdims=True)
        hit = work == m
        work = jnp.where(hit, -jnp.inf, work)
        rank = jnp.where(hit, float(r), rank)
        vals.append(m)
    removed = jnp.sum(jnp.where(work == -jnp.inf, 1.0, 0.0))
    return vals, rank, removed


def _stack_rows(rows, row_idx):
    out = jnp.broadcast_to(rows[0], row_idx.shape)
    for r in range(1, len(rows)):
        out = jnp.where(row_idx == r, rows[r], out)
    return out


def _topk_kernel(q_ref, keys_ref, e1n_ref, lim_ref, e2_ref, r2_ref, sc_ref, top_ref, rank_ref, *, tm):
    key_idx = lax.broadcasted_iota(jnp.int32, (PEER_NKEYS, tm), 0).astype(F32)
    iota16 = lax.broadcasted_iota(jnp.int32, (PEER_TOPK, tm), 0).astype(F32)
    iota8 = iota16[:8]
    flat = [iota16] + [a * PEER_TOPK + iota8 for a in range(1, 8)] + [(iota8 + 8) * PEER_TOPK]
    flat = jnp.concatenate(flat, axis=0)
    unranked = (-1.0, PEER_UNRANKED)
    halves = [(h, s) for h in range(PEER_HEADS) for s in range(2)]
    removed = 0.0
    for h, s in halves:
        col = (2 * h + s) * PEER_SUBDIM
        qs = q_ref[:, col:col + PEER_SUBDIM].astype(BF16)
        ks = keys_ref[h, s].astype(BF16)
        sc = lax.dot_general(ks, qs, (((1,), (1,)), ((), ())), preferred_element_type=F32)
        vals, rank, gone = _extract_top_untied(sc, PEER_TOPK, unranked[s])
        sc_ref[h, s] = sc
        top_ref[h, s] = _stack_rows(vals, iota16)
        rank_ref[h, s] = rank
        removed = removed + gone

    @pl.when(removed != float(len(halves) * PEER_TOPK * tm))
    def _():
        for h, s in halves:
            vals, _, rank = _extract_top(sc_ref[h, s], key_idx, PEER_TOPK, unranked=unranked[s])
            top_ref[h, s] = _stack_rows(vals, iota16)
            rank_ref[h, s] = rank

    for h in range(PEER_HEADS):
        scores = [sc_ref[h, 0], sc_ref[h, 1]]
        ranks = [rank_ref[h, 0], rank_ref[h, 1]]
        top1, v2_16 = top_ref[h, 0], top_ref[h, 1]
        v1 = [top1[r:r + 1, :] for r in range(PEER_TOPK)]
        v2 = [v2_16[0:1, :]]
        v2_8 = v2_16[:8]
        cand = [v1[0] + v2_16]
        for a in range(1, 8):
            cand.append(jnp.where(iota8 < _STAIR_LIMIT[a], v1[a] + v2_8, -jnp.inf))
        cand.append(top1[8:] + v2[0])
        cand = jnp.concatenate(cand, axis=0)
        cv, picks, _ = _extract_top(cand, flat, PEER_TOPK)
        z = jnp.ones_like(cv[0])
        for r in range(1, PEER_TOPK):
            z = z + jnp.exp(cv[r] - cv[0])
        limit = jnp.full((PEER_TOPK, tm), -1.0, F32)
        for pick in picks:
            row = jnp.floor(pick * (1.0 / PEER_TOPK))
            col = pick - PEER_TOPK * row
            limit = jnp.maximum(limit, jnp.where(iota16 == row, col, -1.0))
        lim = jnp.full((PEER_NKEYS, tm), -1.0, F32)
        for a in range(PEER_TOPK):
            lim = jnp.where(ranks[0] == float(a), limit[a:a + 1, :], lim)
        e1n_ref[h] = jnp.exp(scores[0] - v1[0]) / z
        lim_ref[h] = lim
        e2_ref[h] = jnp.exp(scores[1] - v2[0]).astype(BF16)
        r2_ref[h] = ranks[1].astype(BF16)


def _peer_topk(q, keys, tm):
    n = q.shape[0]
    per_key = pl.BlockSpec((PEER_HEADS, PEER_NKEYS, tm), lambda i: (0, 0, i))
    shape = (PEER_HEADS, PEER_NKEYS, n)
    return pl.pallas_call(
        functools.partial(_topk_kernel, tm=tm),
        grid=(n // tm,),
        in_specs=[pl.BlockSpec((tm, q.shape[1]), lambda i: (i, 0)),
                  pl.BlockSpec(keys.shape, lambda i: (0, 0, 0, 0))],
        out_specs=[per_key] * 4,
        out_shape=[jax.ShapeDtypeStruct(shape, F32), jax.ShapeDtypeStruct(shape, F32),
                   jax.ShapeDtypeStruct(shape, BF16), jax.ShapeDtypeStruct(shape, BF16)],
        scratch_shapes=[pltpu.VMEM((PEER_HEADS, 2, PEER_NKEYS, tm), F32),
                        pltpu.VMEM((PEER_HEADS, 2, PEER_TOPK, tm), F32),
                        pltpu.VMEM((PEER_HEADS, 2, PEER_NKEYS, tm), F32)],
        compiler_params=_params("parallel"),
        name="peer_topk",
    )(q, keys)


def _peer_kernel(ht_ref, u_ref, vt_ref, e1n_ref, lim_ref, e2_ref, r2_ref,
                 o_ref, act_a, act_b, p_a, p_b, row_bc, *, te, nblocks, total):
    j = pl.program_id(0)
    slabs = te // PEER_NKEYS

    tm = o_ref.shape[1]
    tile = (V7X_BF16_ROWS, tm)

    @pl.when(j == 0)
    def _():
        p_a[...] = jnp.zeros_like(p_a)
        act_b[...] = jnp.zeros_like(act_b)

    @pl.when(jnp.maximum(j - 2, 0) % nblocks == 0)
    def _():
        o_ref[...] = jnp.zeros_like(o_ref)

    def step(act_new, act_old, p_new, p_old):
        first = (jnp.clip(j - 1, 0, total - 1) % nblocks) * slabs
        d = ht_ref.shape[0]

        def gate(il, r):
            i1 = first + il
            per_head = []
            for h in range(PEER_HEADS):
                if r == 0:
                    rows_i1 = [jnp.broadcast_to(ref[h, pl.ds(i1, 1), :], tile).astype(BF16)
                               for ref in (e1n_ref, lim_ref)]
                    for a, val in enumerate(rows_i1):
                        row_bc[a, il, h] = val
                else:
                    rows_i1 = [row_bc[a, il, h] for a in range(2)]
                per_head.append(rows_i1)
            for q in range(PEER_GATE_ROWS // V7X_BF16_ROWS):
                k0 = r * PEER_GATE_ROWS + q * V7X_BF16_ROWS
                keys = slice(k0, k0 + V7X_BF16_ROWS)
                w = jnp.zeros(tile, BF16)
                for h in range(PEER_HEADS):
                    e1, lim = per_head[h]
                    w = w + jnp.where(r2_ref[h, keys, :] <= lim, e2_ref[h, keys, :] * e1, jnp.zeros((), BF16))
                rows = slice(il * PEER_NKEYS + k0, il * PEER_NKEYS + k0 + V7X_BF16_ROWS)
                p_new[rows, :] = w * jax.nn.gelu(act_old[rows, :]).astype(BF16)

        def pre_act(kc):
            ks = slice(kc * PEER_K_CHUNK, (kc + 1) * PEER_K_CHUNK)
            part = jnp.dot(u_ref[:, ks], ht_ref[ks, :], preferred_element_type=F32)
            if kc == 0:
                act_new[...] = part
            else:
                act_new[...] += part

        def accumulate(mc):
            ms = slice(mc * PEER_M_CHUNK, (mc + 1) * PEER_M_CHUNK)
            o_ref[ms, :] += jnp.dot(vt_ref[ms, :], p_old[...], preferred_element_type=F32)

        gates = [(il, r) for il in range(slabs) for r in range(PEER_NKEYS // PEER_GATE_ROWS)]
        n_k, n_m = d // PEER_K_CHUNK, d // PEER_M_CHUNK
        mxu = []
        for kc in range(n_k):
            mxu.append((functools.partial(pre_act, kc), PEER_K_CHUNK * te))
            mxu += [(functools.partial(accumulate, kc * (n_m // n_k) + m), PEER_M_CHUNK * te)
                    for m in range(n_m // n_k)]
        mxu_total = sum(cost for _, cost in mxu)
        issued = done = 0
        for piece, cost in mxu:
            piece()
            issued += cost
            upto = (len(gates) * issued) // mxu_total
            for il, r in gates[done:upto]:
                gate(il, r)
            done = upto

    @pl.when(j % 2 == 0)
    def _():
        step(act_a, act_b, p_b, p_a)

    @pl.when(j % 2 == 1)
    def _():
        step(act_b, act_a, p_a, p_b)


def _peer_dense(ht, u_bf, vt_bf, topk, tm):
    d, n = ht.shape
    nblocks, _, te = vt_bf.shape
    total = (n // tm) * nblocks
    item = lambda f, lag: jnp.clip(f - lag, 0, total - 1)
    per_tile = dict(pipeline_mode=pl.Buffered(1))
    per_key = pl.BlockSpec((PEER_HEADS, PEER_NKEYS, tm), lambda f: (0, 0, item(f, 1) // nblocks), **per_tile)
    return pl.pallas_call(
        functools.partial(_peer_kernel, te=te, nblocks=nblocks, total=total),
        grid=(total + 2,),
        in_specs=[pl.BlockSpec((d, tm), lambda f: (0, item(f, 0) // nblocks), **per_tile),
                  pl.BlockSpec((te, d), lambda f: (item(f, 0) % nblocks, 0)),
                  pl.BlockSpec((None, d, te), lambda f: (item(f, 2) % nblocks, 0, 0))]
                 + [per_key] * 4,
        out_specs=pl.BlockSpec((d, tm), lambda f: (0, item(f, 2) // nblocks)),
        out_shape=jax.ShapeDtypeStruct((d, n), F32),
        scratch_shapes=[pltpu.VMEM((te, tm), F32), pltpu.VMEM((te, tm), F32),
                        pltpu.VMEM((te, tm), BF16), pltpu.VMEM((te, tm), BF16),
                        pltpu.VMEM((2, te // PEER_NKEYS, PEER_HEADS, V7X_BF16_ROWS, tm), BF16)],
        compiler_params=_params("arbitrary"),
        name="peer_dense",
    )(ht, u_bf, vt_bf, *topk)


def _split_mod(mod, tokens_per_stream, per_token):
    if per_token:
        mod = jnp.repeat(mod, tokens_per_stream, axis=0)[None]
    else:
        mod = mod[:, None, :]
    return jnp.split(mod, 6, axis=-1)


def _encoder_layer(x, mod, h0_re, h0_im, lp, peer_w, *, tr, scan_steps, mixer_rows, per_token_mod, emit_v):
    (g_pre_tok, g_post_tok, g_pre_ch, g_post_ch, w_in, a_ws, a_bs, a_ln_g, a_ln_b, disc, s5_d,
     w_glu, b_glu, w_branch, w_out, peer_w_query, peer_sub_keys) = lp
    u_bf, vt_bf = peer_w
    nb, t, d = x.shape
    n = nb * t
    tiles_per_group = 1 if per_token_mod else t // tr
    sh_t, sc_t, gt_t, sh_c, sc_c, gt_c = _split_mod(mod, t, per_token_mod)
    x2 = x.reshape(n, d)

    h = _prenorm(x2, g_pre_tok, sc_t, sh_t, tr, tiles_per_group)
    proj = _matmul(h, w_in, 1024, 512, name="in_proj")
    proj3 = proj.reshape(nb, t, IN_WIDTH)
    mix = _mixer_a(proj3, a_ws, a_bs, a_ln_g, a_ln_b, mixer_rows, emit_v)
    ya = mix[0].reshape(n, A_WIDTH)
    v_rows = mix[1] if emit_v else None
    ys, h_re, h_im = _s5(proj3, disc, s5_d, h0_re, h0_im, scan_steps)
    yb = _glu(ys.reshape(n, B_WIDTH), w_glu, b_glu, 1024, 512)
    merged = _branch_merge(ya, yb, w_branch, proj, 1024, 512)
    m = _matmul(merged, w_out, 1024, 512, name="out_proj")

    x1, h2, h2t = _resid_prenorm(x2, m, g_post_tok, gt_t, g_pre_ch, sc_c, sh_c, tr, tiles_per_group)
    q = _matmul(h2, peer_w_query, 1024, 512, name="peer_query")
    topk = _peer_topk(q, peer_sub_keys, min(128, n))
    ft = _peer_dense(h2t, u_bf, vt_bf, topk, min(512, n))
    y = _final_resid(x1, ft, g_post_ch, gt_c, tr, tiles_per_group)
    return (y.reshape(nb, t, d), h_re.reshape(nb, B_GROUPS, B_STATE), h_im.reshape(nb, B_GROUPS, B_STATE), v_rows)


def kernel(x_prompt, x_sample, c_prompt, c_sample, state_ssm_re, state_ssm_im, w_ada, b_ada, g_pre_tok,
           g_post_tok, g_pre_ch, g_post_ch, w_in, a_ws, a_bs, a_ln_g, a_ln_b, s5_lam_re, s5_lam_im,
           s5_log_dt, s5_b_re, s5_b_im, s5_c_re, s5_c_im, s5_d, w_glu, b_glu, w_branch, w_out,
           peer_w_query, peer_sub_keys, peer_u, peer_v):
    depth = w_ada.shape[0]
    nb_p, t_p, _ = x_prompt.shape
    nb_s, t_s, _ = x_sample.shape
    y_p, y_s = x_prompt, x_sample
    re_p, im_p, re_s, im_s, v_s = [], [], [], [], []
    pad = (-(nb_p + nb_s)) % 8
    for l in range(depth):
        c_all = jnp.concatenate([c_prompt, c_sample, jnp.zeros((pad, D_MODEL), F32)], axis=0)
        mod = _ada(c_all, w_ada[l], b_ada[l])
        disc = _s5_discretize(s5_lam_re[l], s5_lam_im[l], s5_log_dt[l], s5_b_re[l], s5_b_im[l],
                              s5_c_re[l], s5_c_im[l])
        lp = (g_pre_tok[l], g_post_tok[l], g_pre_ch[l], g_post_ch[l], w_in[l], a_ws[l], a_bs[l],
              a_ln_g[l], a_ln_b[l], disc, s5_d[l], w_glu[l], b_glu[l], w_branch[l], w_out[l],
              peer_w_query[l], peer_sub_keys[l])
        peer_w = (_to_bf16(peer_u[l], transpose=False), _to_bf16(peer_v[l], transpose=True))
        zeros = jnp.zeros((nb_p, S5_STATES), F32)
        y_p, hr, hi, _ = _encoder_layer(
            y_p, mod[:nb_p], zeros, zeros, lp, peer_w,
            tr=256, scan_steps=32, mixer_rows=A_CHUNK, per_token_mod=False, emit_v=False)
        re_p.append(hr)
        im_p.append(hi)
        y_s, hr, hi, v_rows = _encoder_layer(
            y_s, mod[nb_p:nb_p + nb_s], state_ssm_re[l].reshape(nb_s, S5_STATES),
            state_ssm_im[l].reshape(nb_s, S5_STATES), lp, peer_w,
            tr=nb_s * t_s, scan_steps=t_s, mixer_rows=t_s, per_token_mod=True, emit_v=True)
        re_s.append(hr)
        im_s.append(hi)
        v_s.append(v_rows)
    return (y_p, y_s, jnp.stack(re_p), jnp.stack(im_p), jnp.stack(re_s), jnp.stack(im_s), jnp.stack(v_s))
```

```python
import functools

import jax
import jax.numpy as jnp
from jax import lax
from jax.experimental import pallas as pl
from jax.experimental.pallas import tpu as pltpu

F32 = jnp.float32
BF16 = jnp.bfloat16

D_MODEL = 4096
CHUNK = 64
A_CHUNK = 128
A_GROUPS = 16
A_GROUP_DIM = 128
A_WIDTH = A_GROUPS * A_GROUP_DIM
B_GROUP_DIM = 16
B_GROUPS = 128
B_WIDTH = B_GROUPS * B_GROUP_DIM
B_STATE = 64
S5_STATES = B_GROUPS * B_STATE
S5_GROUPS_PER_BLOCK = 8
S5_BLOCKS = B_GROUPS // S5_GROUPS_PER_BLOCK
S5_BLOCK_STATES = S5_GROUPS_PER_BLOCK * B_STATE
IN_WIDTH = 2 * A_WIDTH + B_WIDTH + 2 * D_MODEL
PEER_HEADS = 8
PEER_SUBDIM = 128
PEER_NKEYS = 128
PEER_EXPERTS = PEER_NKEYS * PEER_NKEYS
PEER_TOPK = 16
PEER_UNRANKED = 1e9
PEER_TE = 512
PEER_GATE_ROWS = 16
PEER_K_CHUNK = 256
PEER_M_CHUNK = 128
EPS = 1e-6

V7X_LANES = 128
V7X_SUBLANES = 8
V7X_BF16_ROWS = 16
V7X_VMEM_BYTES = 64 * 1024 * 1024
VMEM_LIMIT_BYTES = 56 * 1024 * 1024


def _params(*semantics, flags=None):
    return pltpu.CompilerParams(dimension_semantics=semantics, vmem_limit_bytes=VMEM_LIMIT_BYTES, flags=flags)


def _rms(x):
    return x * lax.rsqrt(jnp.mean(x * x, axis=-1, keepdims=True) + EPS)


def _ada_kernel(c_ref, w_ref, b_ref, o_ref):
    s = jax.nn.silu(c_ref[...]).astype(BF16)
    o_ref[...] = jnp.dot(s, w_ref[...].astype(BF16), preferred_element_type=F32) + b_ref[...]


def _ada(c, w, b):
    rows, d = c.shape
    n = w.shape[1]
    tn = 512
    return pl.pallas_call(
        _ada_kernel,
        grid=(n // tn,),
        in_specs=[pl.BlockSpec((rows, d), lambda j: (0, 0)),
                  pl.BlockSpec((d, tn), lambda j: (0, j)),
                  pl.BlockSpec((1, tn), lambda j: (0, j))],
        out_specs=pl.BlockSpec((rows, tn), lambda j: (0, j)),
        out_shape=jax.ShapeDtypeStruct((rows, n), F32),
        compiler_params=_params("parallel"),
        name="ada_mod",
    )(c, w, b.reshape(1, n))


def _mod_spec(mod, tiles_per_group):
    return pl.BlockSpec((None,) + mod.shape[1:], lambda i: (i // tiles_per_group, 0, 0))


def _prenorm_kernel(x_ref, g_ref, sc_ref, sh_ref, o_ref):
    h = _rms(x_ref[...]) * g_ref[...] * (1.0 + sc_ref[...]) + sh_ref[...]
    o_ref[...] = h.astype(BF16)


def _prenorm(x, g, sc, sh, tr, tiles_per_group):
    n, d = x.shape
    return pl.pallas_call(
        _prenorm_kernel,
        grid=(n // tr,),
        in_specs=[pl.BlockSpec((tr, d), lambda i: (i, 0)),
                  pl.BlockSpec((1, d), lambda i: (0, 0)),
                  _mod_spec(sc, tiles_per_group),
                  _mod_spec(sh, tiles_per_group)],
        out_specs=pl.BlockSpec((tr, d), lambda i: (i, 0)),
        out_shape=jax.ShapeDtypeStruct((n, d), BF16),
        compiler_params=_params("parallel"),
        name="prenorm_tok",
    )(x, g.reshape(1, d), sc, sh)


def _resid_kernel(x_ref, m_ref, gpost_ref, gt_ref, gpre_ref, sc_ref, sh_ref, x1_ref, h_ref, ht_ref):
    x1 = x_ref[...] + gt_ref[...] * (_rms(m_ref[...]) * gpost_ref[...])
    x1_ref[...] = x1
    h = _rms(x1) * gpre_ref[...] * (1.0 + sc_ref[...]) + sh_ref[...]
    h_ref[...] = h.astype(BF16)
    ht_ref[...] = h.T.astype(BF16)


def _resid_prenorm(x, m, gpost, gt, gpre, sc, sh, tr, tiles_per_group):
    n, d = x.shape
    row = pl.BlockSpec((tr, d), lambda i: (i, 0))
    vec = pl.BlockSpec((1, d), lambda i: (0, 0))
    return pl.pallas_call(
        _resid_kernel,
        grid=(n // tr,),
        in_specs=[row, row, vec, _mod_spec(gt, tiles_per_group), vec,
                  _mod_spec(sc, tiles_per_group), _mod_spec(sh, tiles_per_group)],
        out_specs=[row, row, pl.BlockSpec((d, tr), lambda i: (0, i))],
        out_shape=[jax.ShapeDtypeStruct((n, d), F32),
                   jax.ShapeDtypeStruct((n, d), BF16),
                   jax.ShapeDtypeStruct((d, n), BF16)],
        compiler_params=_params("parallel"),
        name="resid_prenorm_ch",
    )(x, m, gpost.reshape(1, d), gt, gpre.reshape(1, d), sc, sh)


def _final_kernel(x_ref, ft_ref, gpost_ref, gt_ref, o_ref):
    f = ft_ref[...].T
    o_ref[...] = x_ref[...] + gt_ref[...] * (_rms(f) * gpost_ref[...])


def _final_resid(x, ft, gpost, gt, tr, tiles_per_group):
    n, d = x.shape
    row = pl.BlockSpec((tr, d), lambda i: (i, 0))
    return pl.pallas_call(
        _final_kernel,
        grid=(n // tr,),
        in_specs=[row, pl.BlockSpec((d, tr), lambda i: (0, i)),
                  pl.BlockSpec((1, d), lambda i: (0, 0)), _mod_spec(gt, tiles_per_group)],
        out_specs=row,
        out_shape=jax.ShapeDtypeStruct((n, d), F32),
        compiler_params=_params("parallel"),
        name="final_resid",
    )(x, ft, gpost.reshape(1, d), gt)


def _mm_kernel(a_ref, w_ref, o_ref):
    o_ref[...] = jnp.dot(a_ref[...].astype(BF16), w_ref[...].astype(BF16),
                         preferred_element_type=F32).astype(o_ref.dtype)


def _matmul(a, w, tm, tn, out_dtype=F32, name="matmul"):
    m, k = a.shape
    n = w.shape[1]
    tm = min(tm, m)
    return pl.pallas_call(
        _mm_kernel,
        grid=(m // tm, n // tn),
        in_specs=[pl.BlockSpec((tm, k), lambda i, j: (i, 0)),
                  pl.BlockSpec((k, tn), lambda i, j: (0, j))],
        out_specs=pl.BlockSpec((tm, tn), lambda i, j: (i, j)),
        out_shape=jax.ShapeDtypeStruct((m, n), out_dtype),
        compiler_params=_params("parallel", "parallel"),
        name=name,
    )(a, w)


def _glu_kernel(y_ref, w_ref, b_ref, yj_ref, o_ref):
    acc = jnp.dot(y_ref[...].astype(BF16), w_ref[...].astype(BF16), preferred_element_type=F32)
    o_ref[...] = (yj_ref[...] * jax.nn.sigmoid(acc + b_ref[...])).astype(BF16)


def _glu(y, w, b, tm, tn):
    m, k = y.shape
    n = w.shape[1]
    tm = min(tm, m)
    return pl.pallas_call(
        _glu_kernel,
        grid=(m // tm, n // tn),
        in_specs=[pl.BlockSpec((tm, k), lambda i, j: (i, 0)),
                  pl.BlockSpec((k, tn), lambda i, j: (0, j)),
                  pl.BlockSpec((1, tn), lambda i, j: (0, j)),
                  pl.BlockSpec((tm, tn), lambda i, j: (i, j))],
        out_specs=pl.BlockSpec((tm, tn), lambda i, j: (i, j)),
        out_shape=jax.ShapeDtypeStruct((m, n), BF16),
        compiler_params=_params("parallel", "parallel"),
        name="s5_glu",
    )(y, w, b.reshape(1, n), y)


def _branch_kernel(ya_ref, yb_ref, wa_ref, wb_ref, ga_ref, gb_ref, o_ref):
    a = jnp.dot(ya_ref[...], wa_ref[...].astype(BF16), preferred_element_type=F32)
    b = jnp.dot(yb_ref[...], wb_ref[...].astype(BF16), preferred_element_type=F32)
    o_ref[...] = (jax.nn.sigmoid(ga_ref[...]) * a + jax.nn.sigmoid(gb_ref[...]) * b).astype(BF16)


def _branch_merge(ya, yb, w_branch, proj, tm, tn):
    m = ya.shape[0]
    n = w_branch.shape[1]
    tm = min(tm, m)
    ga_blk = (2 * A_WIDTH + B_WIDTH) // tn
    gb_blk = ga_blk + D_MODEL // tn
    return pl.pallas_call(
        _branch_kernel,
        grid=(n // tn, m // tm),
        in_specs=[pl.BlockSpec((tm, A_WIDTH), lambda j, i: (i, 0)),
                  pl.BlockSpec((tm, B_WIDTH), lambda j, i: (i, 0)),
                  pl.BlockSpec((A_WIDTH, tn), lambda j, i: (0, j)),
                  pl.BlockSpec((B_WIDTH, tn), lambda j, i: (A_WIDTH // B_WIDTH, j)),
                  pl.BlockSpec((tm, tn), lambda j, i: (i, ga_blk + j)),
                  pl.BlockSpec((tm, tn), lambda j, i: (i, gb_blk + j))],
        out_specs=pl.BlockSpec((tm, tn), lambda j, i: (i, j)),
        out_shape=jax.ShapeDtypeStruct((m, n), BF16),
        compiler_params=_params("parallel", "parallel"),
        name="branch_merge",
    )(ya, yb, w_branch, w_branch, proj, proj)


def _mixer_a_kernel(za_ref, ws_ref, bias_ref, lng_ref, lnb_ref, ya_ref, *v_refs, rows):
    z = jax.nn.gelu(za_ref[...])
    u = z[:, :A_WIDTH]
    v = z[:, A_WIDTH:]
    d = v - jnp.mean(v, axis=-1, keepdims=True)
    vn = d * lax.rsqrt(jnp.mean(d * d, axis=-1, keepdims=True) + EPS) * lng_ref[...] + lnb_ref[...]
    if v_refs:
        v_refs[0][...] = vn
    if rows < A_CHUNK:
        vn = jnp.concatenate([vn, jnp.zeros((A_CHUNK - rows, A_WIDTH), F32)], axis=0)
    vb = vn.astype(BF16)
    blk_i = lax.broadcasted_iota(jnp.int32, (A_CHUNK, A_CHUNK), 0) // CHUNK
    blk_j = lax.broadcasted_iota(jnp.int32, (A_CHUNK, A_CHUNK), 1) // CHUNK
    causal = blk_i >= blk_j
    for g in range(A_GROUPS):
        cols = slice(g * A_GROUP_DIM, (g + 1) * A_GROUP_DIM)
        w = jnp.where(causal, ws_ref[g], 0.0).astype(BF16)
        mixed = jnp.dot(w, vb[:, cols], preferred_element_type=F32) + bias_ref[:, cols]
        ya_ref[:, cols] = (u[:, cols] * mixed[:rows]).astype(BF16)


def _mixer_a(proj3, ws, bs, ln_g, ln_b, rows, emit_v):
    nb, t, _ = proj3.shape
    bias = jnp.repeat(bs.T, A_GROUP_DIM, axis=1)
    out_shape = [jax.ShapeDtypeStruct((nb, t, A_WIDTH), BF16)]
    out_specs = [pl.BlockSpec((None, rows, A_WIDTH), lambda b, c: (b, c, 0))]
    if emit_v:
        out_shape.append(jax.ShapeDtypeStruct((nb, t, A_WIDTH), F32))
        out_specs.append(pl.BlockSpec((None, rows, A_WIDTH), lambda b, c: (b, c, 0)))
    return pl.pallas_call(
        functools.partial(_mixer_a_kernel, rows=rows),
        grid=(nb, t // rows),
        in_specs=[pl.BlockSpec((None, rows, 2 * A_WIDTH), lambda b, c: (b, c, 0)),
                  pl.BlockSpec((A_GROUPS, A_CHUNK, A_CHUNK), lambda b, c: (0, 0, 0)),
                  pl.BlockSpec((A_CHUNK, A_WIDTH), lambda b, c: (0, 0)),
                  pl.BlockSpec((1, A_WIDTH), lambda b, c: (0, 0)),
                  pl.BlockSpec((1, A_WIDTH), lambda b, c: (0, 0))],
        out_specs=out_specs,
        out_shape=out_shape,
        compiler_params=_params("parallel", "parallel"),
        name="mixer_a",
    )(proj3, ws, bias, ln_g.reshape(1, A_WIDTH), ln_b.reshape(1, A_WIDTH))


def _s5_discretize(lam_re, lam_im, log_dt, b_re, b_im, c_re, c_im):
    dt = jnp.exp(log_dt.astype(F32))[:, None]
    lr, li = lam_re.astype(F32), lam_im.astype(F32)
    mag = jnp.exp(lr * dt)
    ab_re, ab_im = mag * jnp.cos(li * dt), mag * jnp.sin(li * dt)
    den = lr * lr + li * li
    num_re = ab_re - 1.0
    coef_re = (num_re * lr + ab_im * li) / den
    coef_im = (ab_im * lr - num_re * li) / den
    br, bi = b_re.astype(F32), b_im.astype(F32)
    bb_re = coef_re[..., None] * br - coef_im[..., None] * bi
    bb_im = coef_re[..., None] * bi + coef_im[..., None] * br
    eye = jnp.eye(S5_GROUPS_PER_BLOCK, dtype=F32)

    def pack_in(bb):
        t = bb.reshape(S5_BLOCKS, S5_GROUPS_PER_BLOCK, B_STATE, B_GROUP_DIM).transpose(0, 1, 3, 2)
        return jnp.einsum("kgcp,gm->kgcmp", t, eye).reshape(S5_BLOCKS, V7X_LANES, S5_BLOCK_STATES)

    def pack_out(cc):
        t = cc.reshape(S5_BLOCKS, S5_GROUPS_PER_BLOCK, B_GROUP_DIM, B_STATE).transpose(0, 1, 3, 2)
        return jnp.einsum("kgpc,gm->kgpmc", t, eye).reshape(S5_BLOCKS, S5_BLOCK_STATES, V7X_LANES)

    w_in = jnp.concatenate([pack_in(bb_re), pack_in(bb_im)], axis=-1).astype(BF16)
    w_out = jnp.concatenate([pack_out(c_re.astype(F32)), -pack_out(c_im.astype(F32))], axis=1).astype(BF16)
    return ab_re.reshape(1, S5_STATES), ab_im.reshape(1, S5_STATES), w_in, w_out


def _s5_kernel(xb_ref, perm_ref, unperm_ref, win_ref, wout_ref, are_ref, aim_ref, d_ref, h0r_ref, h0i_ref,
               ys_ref, hr_ref, hi_ref, sre, sim, g_tm, *, nb):
    @pl.when(pl.program_id(0) == 0)
    def _():
        hr_ref[...] = h0r_ref[...]
        hi_ref[...] = h0i_ref[...]

    x_sm = xb_ref[...].reshape(perm_ref.shape[0], B_WIDTH)
    x_hi = x_sm.astype(BF16)
    rest = x_sm - x_hi.astype(F32)
    x_mid = rest.astype(BF16)
    x_lo = (rest - x_mid.astype(F32)).astype(BF16)
    perm = perm_ref[...]
    x_top = jnp.dot(perm, x_hi, preferred_element_type=F32)
    x = (x_top + jnp.dot(perm, x_mid, preferred_element_type=F32)
         + jnp.dot(perm, x_lo, preferred_element_type=F32))
    xbf = x_top.astype(BF16)
    for k in range(S5_BLOCKS):
        drive = jnp.dot(xbf[:, k * V7X_LANES:(k + 1) * V7X_LANES], win_ref[k], preferred_element_type=F32)
        sre[:, k * S5_BLOCK_STATES:(k + 1) * S5_BLOCK_STATES] = drive[:, :S5_BLOCK_STATES]
        sim[:, k * S5_BLOCK_STATES:(k + 1) * S5_BLOCK_STATES] = drive[:, S5_BLOCK_STATES:]

    a_re = are_ref[...]
    a_im = aim_ref[...]
    sub = lax.broadcasted_iota(jnp.int32, (V7X_SUBLANES, S5_STATES), 0)
    for i in range(x.shape[0] // V7X_SUBLANES):
        rows = slice(i * V7X_SUBLANES, (i + 1) * V7X_SUBLANES)
        prev = slice((i - 1) * V7X_SUBLANES, i * V7X_SUBLANES)
        c_re = hr_ref[...] if i == 0 else sre[prev, :]
        c_im = hi_ref[...] if i == 0 else sim[prev, :]
        d_re = sre[rows, :]
        d_im = sim[rows, :]
        o_re = o_im = None
        for s in range(V7X_SUBLANES // nb):
            if nb < V7X_SUBLANES:
                c_re = pltpu.roll(c_re, nb, 0)
                c_im = pltpu.roll(c_im, nb, 0)
            n_re = a_re * c_re - a_im * c_im + d_re
            n_im = a_re * c_im + a_im * c_re + d_im
            o_re = n_re if s == 0 else jnp.where(sub >= s * nb, n_re, o_re)
            o_im = n_im if s == 0 else jnp.where(sub >= s * nb, n_im, o_im)
            c_re, c_im = n_re, n_im
        sre[rows, :] = o_re
        sim[rows, :] = o_im
    last = slice(x.shape[0] - V7X_SUBLANES, x.shape[0])
    hr_ref[...] = sre[last, :]
    hi_ref[...] = sim[last, :]

    for k in range(S5_BLOCKS):
        st = slice(k * S5_BLOCK_STATES, (k + 1) * S5_BLOCK_STATES)
        ch = slice(k * V7X_LANES, (k + 1) * V7X_LANES)
        y = (jnp.dot(sre[:, st].astype(BF16), wout_ref[k, :S5_BLOCK_STATES, :], preferred_element_type=F32)
             + jnp.dot(sim[:, st].astype(BF16), wout_ref[k, S5_BLOCK_STATES:, :], preferred_element_type=F32)
             + d_ref[:, ch] * x[:, ch])
        g_tm[:, ch] = jax.nn.gelu(y).astype(BF16)
    g_sm = jnp.dot(unperm_ref[...], g_tm[...], preferred_element_type=F32).astype(BF16)
    ys_ref[...] = g_sm.reshape(ys_ref.shape)


def _s5(proj3, disc, d_skip, h0_re, h0_im, steps):
    nb, t, _ = proj3.shape
    a_re, a_im, w_in, w_out = disc
    const2 = lambda c: (0, 0)
    const3 = lambda c: (0, 0, 0)
    blk = steps * nb
    src = jnp.arange(blk)
    perm = jax.nn.one_hot((src % nb) * steps + src // nb, blk, dtype=BF16)
    pad = jnp.zeros((V7X_SUBLANES - nb, S5_STATES), F32)
    state = pl.BlockSpec((V7X_SUBLANES, S5_STATES), const2)
    xb_blk = 2 * A_WIDTH // B_WIDTH
    ys, h_re, h_im = pl.pallas_call(
        functools.partial(_s5_kernel, nb=nb),
        grid=(t // steps,),
        in_specs=[pl.BlockSpec((nb, steps, B_WIDTH), lambda c: (0, c, xb_blk)),
                  pl.BlockSpec((blk, blk), const2),
                  pl.BlockSpec((blk, blk), const2),
                  pl.BlockSpec(w_in.shape, const3),
                  pl.BlockSpec(w_out.shape, const3),
                  pl.BlockSpec((1, S5_STATES), const2),
                  pl.BlockSpec((1, S5_STATES), const2),
                  pl.BlockSpec((1, B_WIDTH), const2),
                  state, state],
        out_specs=[pl.BlockSpec((nb, steps, B_WIDTH), lambda c: (0, c, 0)), state, state],
        out_shape=[jax.ShapeDtypeStruct((nb, t, B_WIDTH), BF16),
                   jax.ShapeDtypeStruct((V7X_SUBLANES, S5_STATES), F32),
                   jax.ShapeDtypeStruct((V7X_SUBLANES, S5_STATES), F32)],
        scratch_shapes=[pltpu.VMEM((blk, S5_STATES), F32),
                        pltpu.VMEM((blk, S5_STATES), F32),
                        pltpu.VMEM((blk, B_WIDTH), BF16)],
        compiler_params=_params("arbitrary"),
        name="s5_scan",
    )(proj3, perm, perm.T, w_in, w_out, a_re, a_im, d_skip.reshape(1, B_WIDTH),
      jnp.concatenate([pad, h0_re], axis=0), jnp.concatenate([pad, h0_im], axis=0))
    return ys, h_re[V7X_SUBLANES - nb:], h_im[V7X_SUBLANES - nb:]


def _cast_kernel(x_ref, o_ref):
    o_ref[...] = x_ref[...].astype(BF16)


def _cast_t_kernel(x_ref, o_ref):
    o_ref[...] = x_ref[...].T.astype(BF16)


def _to_bf16(x, transpose):
    r, c = x.shape
    tr, tc = PEER_TE, 2048
    if transpose:
        return pl.pallas_call(
            _cast_t_kernel, grid=(r // tr, c // tc),
            in_specs=[pl.BlockSpec((tr, tc), lambda i, j: (i, j))],
            out_specs=pl.BlockSpec((None, tc, tr), lambda i, j: (i, j, 0)),
            out_shape=jax.ShapeDtypeStruct((r // tr, c, tr), BF16),
            compiler_params=_params("parallel", "parallel"), name="cast_transpose")(x)
    return pl.pallas_call(
        _cast_kernel, grid=(r // tr, c // tc),
        in_specs=[pl.BlockSpec((tr, tc), lambda i, j: (i, j))],
        out_specs=pl.BlockSpec((tr, tc), lambda i, j: (i, j)),
        out_shape=jax.ShapeDtypeStruct((r, c), BF16),
        compiler_params=_params("parallel", "parallel"), name="cast_bf16")(x)


_STAIR_LIMIT = [PEER_TOPK // (a + 1) for a in range(PEER_TOPK)]


def _extract_top(work, order, count, unranked=None):
    big = PEER_UNRANKED
    vals, picks = [], []
    rank = None if unranked is None else jnp.full(work.shape, unranked, F32)
    for r in range(count):
        m = jnp.max(work, axis=0, keepdims=True)
        pick = jnp.min(jnp.where(work == m, order, big), axis=0, keepdims=True)
        hit = order == pick
        work = jnp.where(hit, -jnp.inf, work)
        if rank is not None:
            rank = jnp.where(hit, float(r), rank)
        vals.append(m)
        picks.append(pick)
    return vals, picks, rank


def _extract_top_untied(work, count, unranked):
    vals = []
    rank = jnp.full(work.shape, unranked, F32)
    for r in range(count):
        m = jnp.max(work, axis=0, keepdims=True)
        hit = work == m
        work = jnp.where(hit, -jnp.inf, work)
        rank = jnp.where(hit, float(r), rank)
        vals.append(m)
    removed = jnp.sum(jnp.where(work == -jnp.inf, 1.0, 0.0))
    return vals, rank, removed


def _stack_rows(rows, row_idx):
    out = jnp.broadcast_to(rows[0], row_idx.shape)
    for r in range(1, len(rows)):
        out = jnp.where(row_idx == r, rows[r], out)
    return out


def _topk_kernel(q_ref, keys_ref, e1n_ref, lim_ref, e2_ref, r2_ref, sc_ref, top_ref, rank_ref, *, tm):
    key_idx = lax.broadcasted_iota(jnp.int32, (PEER_NKEYS, tm), 0).astype(F32)
    iota16 = lax.broadcasted_iota(jnp.int32, (PEER_TOPK, tm), 0).astype(F32)
    iota8 = iota16[:8]
    flat = [iota16] + [a * PEER_TOPK + iota8 for a in range(1, 8)] + [(iota8 + 8) * PEER_TOPK]
    flat = jnp.concatenate(flat, axis=0)
    unranked = (-1.0, PEER_UNRANKED)
    halves = [(h, s) for h in range(PEER_HEADS) for s in range(2)]
    removed = 0.0
    for h, s in halves:
        col = (2 * h + s) * PEER_SUBDIM
        qs = q_ref[:, col:col + PEER_SUBDIM].astype(BF16)
        ks = keys_ref[h, s].astype(BF16)
        sc = lax.dot_general(ks, qs, (((1,), (1,)), ((), ())), preferred_element_type=F32)
        vals, rank, gone = _extract_top_untied(sc, PEER_TOPK, unranked[s])
        sc_ref[h, s] = sc
        top_ref[h, s] = _stack_rows(vals, iota16)
        rank_ref[h, s] = rank
        removed = removed + gone

    @pl.when(removed != float(len(halves) * PEER_TOPK * tm))
    def _():
        for h, s in halves:
            vals, _, rank = _extract_top(sc_ref[h, s], key_idx, PEER_TOPK, unranked=unranked[s])
            top_ref[h, s] = _stack_rows(vals, iota16)
            rank_ref[h, s] = rank

    for h in range(PEER_HEADS):
        scores = [sc_ref[h, 0], sc_ref[h, 1]]
        ranks = [rank_ref[h, 0], rank_ref[h, 1]]
        top1, v2_16 = top_ref[h, 0], top_ref[h, 1]
        v1 = [top1[r:r + 1, :] for r in range(PEER_TOPK)]
        v2 = [v2_16[0:1, :]]
        v2_8 = v2_16[:8]
        cand = [v1[0] + v2_16]
        for a in range(1, 8):
            cand.append(jnp.where(iota8 < _STAIR_LIMIT[a], v1[a] + v2_8, -jnp.inf))
        cand.append(top1[8:] + v2[0])
        cand = jnp.concatenate(cand, axis=0)
        cv, picks, _ = _extract_top(cand, flat, PEER_TOPK)
        z = jnp.ones_like(cv[0])
        for r in range(1, PEER_TOPK):
            z = z + jnp.exp(cv[r] - cv[0])
        limit = jnp.full((PEER_TOPK, tm), -1.0, F32)
        for pick in picks:
            row = jnp.floor(pick * (1.0 / PEER_TOPK))
            col = pick - PEER_TOPK * row
            limit = jnp.maximum(limit, jnp.where(iota16 == row, col, -1.0))
        lim = jnp.full((PEER_NKEYS, tm), -1.0, F32)
        for a in range(PEER_TOPK):
            lim = jnp.where(ranks[0] == float(a), limit[a:a + 1, :], lim)
        e1n_ref[h] = jnp.exp(scores[0] - v1[0]) / z
        lim_ref[h] = lim
        e2_ref[h] = jnp.exp(scores[1] - v2[0]).astype(BF16)
        r2_ref[h] = ranks[1].astype(BF16)


def _peer_topk(q, keys, tm):
    n = q.shape[0]
    per_key = pl.BlockSpec((PEER_HEADS, PEER_NKEYS, tm), lambda i: (0, 0, i))
    shape = (PEER_HEADS, PEER_NKEYS, n)
    return pl.pallas_call(
        functools.partial(_topk_kernel, tm=tm),
        grid=(n // tm,),
        in_specs=[pl.BlockSpec((tm, q.shape[1]), lambda i: (i, 0)),
                  pl.BlockSpec(keys.shape, lambda i: (0, 0, 0, 0))],
        out_specs=[per_key] * 4,
        out_shape=[jax.ShapeDtypeStruct(shape, F32), jax.ShapeDtypeStruct(shape, F32),
                   jax.ShapeDtypeStruct(shape, BF16), jax.ShapeDtypeStruct(shape, BF16)],
        scratch_shapes=[pltpu.VMEM((PEER_HEADS, 2, PEER_NKEYS, tm), F32),
                        pltpu.VMEM((PEER_HEADS, 2, PEER_TOPK, tm), F32),
                        pltpu.VMEM((PEER_HEADS, 2, PEER_NKEYS, tm), F32)],
        compiler_params=_params("parallel"),
        name="peer_topk",
    )(q, keys)


def _peer_kernel(ht_ref, u_ref, vt_ref, e1n_ref, lim_ref, e2_ref, r2_ref,
                 o_ref, act_a, act_b, p_a, p_b, row_bc, *, te, nblocks, total):
    j = pl.program_id(0)
    slabs = te // PEER_NKEYS

    tm = o_ref.shape[1]
    tile = (V7X_BF16_ROWS, tm)

    @pl.when(j == 0)
    def _():
        p_a[...] = jnp.zeros_like(p_a)
        act_b[...] = jnp.zeros_like(act_b)

    @pl.when(jnp.maximum(j - 2, 0) % nblocks == 0)
    def _():
        o_ref[...] = jnp.zeros_like(o_ref)

    def step(act_new, act_old, p_new, p_old):
        first = (jnp.clip(j - 1, 0, total - 1) % nblocks) * slabs
        d = ht_ref.shape[0]

        def gate(il, r):
            i1 = first + il
            per_head = []
            for h in range(PEER_HEADS):
                if r == 0:
                    rows_i1 = [jnp.broadcast_to(ref[h, pl.ds(i1, 1), :], tile).astype(BF16)
                               for ref in (e1n_ref, lim_ref)]
                    for a, val in enumerate(rows_i1):
                        row_bc[a, il, h] = val
                else:
                    rows_i1 = [row_bc[a, il, h] for a in range(2)]
                per_head.append(rows_i1)
            for q in range(PEER_GATE_ROWS // V7X_BF16_ROWS):
                k0 = r * PEER_GATE_ROWS + q * V7X_BF16_ROWS
                keys = slice(k0, k0 + V7X_BF16_ROWS)
                w = jnp.zeros(tile, BF16)
                for h in range(PEER_HEADS):
                    e1, lim = per_head[h]
                    w = w + jnp.where(r2_ref[h, keys, :] <= lim, e2_ref[h, keys, :] * e1, jnp.zeros((), BF16))
                rows = slice(il * PEER_NKEYS + k0, il * PEER_NKEYS + k0 + V7X_BF16_ROWS)
                p_new[rows, :] = w * jax.nn.gelu(act_old[rows, :]).astype(BF16)

        def pre_act(kc):
            ks = slice(kc * PEER_K_CHUNK, (kc + 1) * PEER_K_CHUNK)
            part = jnp.dot(u_ref[:, ks].astype(BF16), ht_ref[ks, :], preferred_element_type=F32)
            if kc == 0:
                act_new[...] = part
            else:
                act_new[...] += part

        def accumulate(mc):
            ms = slice(mc * PEER_M_CHUNK, (mc + 1) * PEER_M_CHUNK)
            o_ref[ms, :] += jnp.dot(vt_ref[ms, :], p_old[...], preferred_element_type=F32)

        gates = [(il, r) for il in range(slabs) for r in range(PEER_NKEYS // PEER_GATE_ROWS)]
        n_k, n_m = d // PEER_K_CHUNK, d // PEER_M_CHUNK
        mxu = []
        for kc in range(n_k):
            mxu.append((functools.partial(pre_act, kc), PEER_K_CHUNK * te))
            mxu += [(functools.partial(accumulate, kc * (n_m // n_k) + m), PEER_M_CHUNK * te)
                    for m in range(n_m // n_k)]
        mxu_total = sum(cost for _, cost in mxu)
        issued = done = 0
        for piece, cost in mxu:
            piece()
            issued += cost
            upto = (len(gates) * issued) // mxu_total
            for il, r in gates[done:upto]:
                gate(il, r)
            done = upto

    @pl.when(j % 2 == 0)
    def _():
        step(act_a, act_b, p_b, p_a)

    @pl.when(j % 2 == 1)
    def _():
        step(act_b, act_a, p_a, p_b)


def _peer_dense(ht, u_bf, vt_bf, topk, tm):
    d, n = ht.shape
    nblocks, _, te = vt_bf.shape
    total = (n // tm) * nblocks
    item = lambda f, lag: jnp.clip(f - lag, 0, total - 1)
    per_tile = dict(pipeline_mode=pl.Buffered(1))
    per_key = pl.BlockSpec((PEER_HEADS, PEER_NKEYS, tm), lambda f: (0, 0, item(f, 1) // nblocks), **per_tile)
    return pl.pallas_call(
        functools.partial(_peer_kernel, te=te, nblocks=nblocks, total=total),
        grid=(total + 2,),
        in_specs=[pl.BlockSpec((d, tm), lambda f: (0, item(f, 0) // nblocks), **per_tile),
                  pl.BlockSpec((te, d), lambda f: (item(f, 0) % nblocks, 0)),
                  pl.BlockSpec((None, d, te), lambda f: (item(f, 2) % nblocks, 0, 0))]
                 + [per_key] * 4,
        out_specs=pl.BlockSpec((d, tm), lambda f: (0, item(f, 2) // nblocks)),
        out_shape=jax.ShapeDtypeStruct((d, n), F32),
        scratch_shapes=[pltpu.VMEM((te, tm), F32), pltpu.VMEM((te, tm), F32),
                        pltpu.VMEM((te, tm), BF16), pltpu.VMEM((te, tm), BF16),
                        pltpu.VMEM((2, te // PEER_NKEYS, PEER_HEADS, V7X_BF16_ROWS, tm), BF16)],
        compiler_params=_params("arbitrary"),
        name="peer_dense",
    )(ht, u_bf, vt_bf, *topk)


def _split_mod(mod, tokens_per_stream, per_token):
    if per_token:
        mod = jnp.repeat(mod, tokens_per_stream, axis=0)[None]
    else:
        mod = mod[:, None, :]
    return jnp.split(mod, 6, axis=-1)


def _encoder_layer(x, mod, h0_re, h0_im, lp, peer_w, *, tr, scan_steps, mixer_rows, per_token_mod, emit_v):
    (g_pre_tok, g_post_tok, g_pre_ch, g_post_ch, w_in, a_ws, a_bs, a_ln_g, a_ln_b, disc, s5_d,
     w_glu, b_glu, w_branch, w_out, peer_w_query, peer_sub_keys) = lp
    u_bf, vt_bf = peer_w
    nb, t, d = x.shape
    n = nb * t
    tiles_per_group = 1 if per_token_mod else t // tr
    sh_t, sc_t, gt_t, sh_c, sc_c, gt_c = _split_mod(mod, t, per_token_mod)
    x2 = x.reshape(n, d)

    h = _prenorm(x2, g_pre_tok, sc_t, sh_t, tr, tiles_per_group)
    proj = _matmul(h, w_in, 1024, 512, name="in_proj")
    proj3 = proj.reshape(nb, t, IN_WIDTH)
    mix = _mixer_a(proj3, a_ws, a_bs, a_ln_g, a_ln_b, mixer_rows, emit_v)
    ya = mix[0].reshape(n, A_WIDTH)
    v_rows = mix[1] if emit_v else None
    ys, h_re, h_im = _s5(proj3, disc, s5_d, h0_re, h0_im, scan_steps)
    yb = _glu(ys.reshape(n, B_WIDTH), w_glu, b_glu, 1024, 512)
    merged = _branch_merge(ya, yb, w_branch, proj, 1024, 512)
    m = _matmul(merged, w_out, 1024, 512, name="out_proj")

    x1, h2, h2t = _resid_prenorm(x2, m, g_post_tok, gt_t, g_pre_ch, sc_c, sh_c, tr, tiles_per_group)
    q = _matmul(h2, peer_w_query, 1024, 512, name="peer_query")
    topk = _peer_topk(q, peer_sub_keys, min(128, n))
    ft = _peer_dense(h2t, u_bf, vt_bf, topk, min(512, n))
    y = _final_resid(x1, ft, g_post_ch, gt_c, tr, tiles_per_group)
    return (y.reshape(nb, t, d), h_re.reshape(nb, B_GROUPS, B_STATE), h_im.reshape(nb, B_GROUPS, B_STATE), v_rows)


def kernel(x_prompt, x_sample, c_prompt, c_sample, state_ssm_re, state_ssm_im, w_ada, b_ada, g_pre_tok,
           g_post_tok, g_pre_ch, g_post_ch, w_in, a_ws, a_bs, a_ln_g, a_ln_b, s5_lam_re, s5_lam_im,
           s5_log_dt, s5_b_re, s5_b_im, s5_c_re, s5_c_im, s5_d, w_glu, b_glu, w_branch, w_out,
           peer_w_query, peer_sub_keys, peer_u, peer_v):
    depth = w_ada.shape[0]
    nb_p, t_p, _ = x_prompt.shape
    nb_s, t_s, _ = x_sample.shape
    y_p, y_s = x_prompt, x_sample
    re_p, im_p, re_s, im_s, v_s = [], [], [], [], []
    pad = (-(nb_p + nb_s)) % 8
    for l in range(depth):
        c_all = jnp.concatenate([c_prompt, c_sample, jnp.zeros((pad, D_MODEL), F32)], axis=0)
        mod = _ada(c_all, w_ada[l], b_ada[l])
        disc = _s5_discretize(s5_lam_re[l], s5_lam_im[l], s5_log_dt[l], s5_b_re[l], s5_b_im[l],
                              s5_c_re[l], s5_c_im[l])
        lp = (g_pre_tok[l], g_post_tok[l], g_pre_ch[l], g_post_ch[l], w_in[l], a_ws[l], a_bs[l],
              a_ln_g[l], a_ln_b[l], disc, s5_d[l], w_glu[l], b_glu[l], w_branch[l], w_out[l],
              peer_w_query[l], peer_sub_keys[l])
        peer_w = (peer_u[l], _to_bf16(peer_v[l], transpose=True))
        zeros = jnp.zeros((nb_p, S5_STATES), F32)
        y_p, hr, hi, _ = _encoder_layer(
            y_p, mod[:nb_p], zeros, zeros, lp, peer_w,
            tr=256, scan_steps=32, mixer_rows=A_CHUNK, per_token_mod=False, emit_v=False)
        re_p.append(hr)
        im_p.append(hi)
        y_s, hr, hi, v_rows = _encoder_layer(
            y_s, mod[nb_p:nb_p + nb_s], state_ssm_re[l].reshape(nb_s, S5_STATES),
            state_ssm_im[l].reshape(nb_s, S5_STATES), lp, peer_w,
            tr=nb_s * t_s, scan_steps=t_s, mixer_rows=t_s, per_token_mod=True, emit_v=True)
        re_s.append(hr)
        im_s.append(hi)
        v_s.append(v_rows)
    return (y_p, y_s, jnp.stack(re_p), jnp.stack(im_p), jnp.stack(re_s), jnp.stack(im_s), jnp.stack(v_s))
```

```python
import functools

import jax
import jax.numpy as jnp
from jax import lax
from jax.experimental import pallas as pl
from jax.experimental.pallas import tpu as pltpu

F32 = jnp.float32
BF16 = jnp.bfloat16

D_MODEL = 4096
CHUNK = 64
A_CHUNK = 128
A_GROUPS = 16
A_GROUP_DIM = 128
A_WIDTH = A_GROUPS * A_GROUP_DIM
B_GROUP_DIM = 16
B_GROUPS = 128
B_WIDTH = B_GROUPS * B_GROUP_DIM
B_STATE = 64
S5_STATES = B_GROUPS * B_STATE
S5_GROUPS_PER_BLOCK = 8
S5_BLOCKS = B_GROUPS // S5_GROUPS_PER_BLOCK
S5_BLOCK_STATES = S5_GROUPS_PER_BLOCK * B_STATE
IN_WIDTH = 2 * A_WIDTH + B_WIDTH + 2 * D_MODEL
PEER_HEADS = 8
PEER_SUBDIM = 128
PEER_NKEYS = 128
PEER_EXPERTS = PEER_NKEYS * PEER_NKEYS
PEER_TOPK = 16
PEER_UNRANKED = 1e9
PEER_TE = 512
PEER_GATE_ROWS = 16
PEER_K_CHUNK = 256
PEER_M_CHUNK = 128
EPS = 1e-6

V7X_LANES = 128
V7X_SUBLANES = 8
V7X_BF16_ROWS = 16
V7X_VMEM_BYTES = 64 * 1024 * 1024
VMEM_LIMIT_BYTES = 56 * 1024 * 1024


def _params(*semantics, flags=None):
    return pltpu.CompilerParams(dimension_semantics=semantics, vmem_limit_bytes=VMEM_LIMIT_BYTES, flags=flags)


def _rms(x):
    return x * lax.rsqrt(jnp.mean(x * x, axis=-1, keepdims=True) + EPS)


def _ada_kernel(c_ref, w_ref, b_ref, o_ref):
    s = jax.nn.silu(c_ref[...]).astype(BF16)
    o_ref[...] = jnp.dot(s, w_ref[...].astype(BF16), preferred_element_type=F32) + b_ref[...]


def _ada(c, w, b):
    rows, d = c.shape
    n = w.shape[1]
    tn = 512
    return pl.pallas_call(
        _ada_kernel,
        grid=(n // tn,),
        in_specs=[pl.BlockSpec((rows, d), lambda j: (0, 0)),
                  pl.BlockSpec((d, tn), lambda j: (0, j)),
                  pl.BlockSpec((1, tn), lambda j: (0, j))],
        out_specs=pl.BlockSpec((rows, tn), lambda j: (0, j)),
        out_shape=jax.ShapeDtypeStruct((rows, n), F32),
        compiler_params=_params("parallel"),
        name="ada_mod",
    )(c, w, b.reshape(1, n))


def _mod_spec(mod, tiles_per_group):
    return pl.BlockSpec((None,) + mod.shape[1:], lambda i: (i // tiles_per_group, 0, 0))


def _prenorm_kernel(x_ref, g_ref, sc_ref, sh_ref, o_ref):
    h = _rms(x_ref[...]) * g_ref[...] * (1.0 + sc_ref[...]) + sh_ref[...]
    o_ref[...] = h.astype(BF16)


def _prenorm(x, g, sc, sh, tr, tiles_per_group):
    n, d = x.shape
    return pl.pallas_call(
        _prenorm_kernel,
        grid=(n // tr,),
        in_specs=[pl.BlockSpec((tr, d), lambda i: (i, 0)),
                  pl.BlockSpec((1, d), lambda i: (0, 0)),
                  _mod_spec(sc, tiles_per_group),
                  _mod_spec(sh, tiles_per_group)],
        out_specs=pl.BlockSpec((tr, d), lambda i: (i, 0)),
        out_shape=jax.ShapeDtypeStruct((n, d), BF16),
        compiler_params=_params("parallel"),
        name="prenorm_tok",
    )(x, g.reshape(1, d), sc, sh)


def _resid_kernel(x_ref, m_ref, gpost_ref, gt_ref, gpre_ref, sc_ref, sh_ref, x1_ref, h_ref, ht_ref):
    x1 = x_ref[...] + gt_ref[...] * (_rms(m_ref[...]) * gpost_ref[...])
    x1_ref[...] = x1
    h = _rms(x1) * gpre_ref[...] * (1.0 + sc_ref[...]) + sh_ref[...]
    h_ref[...] = h.astype(BF16)
    ht_ref[...] = h.T.astype(BF16)


def _resid_prenorm(x, m, gpost, gt, gpre, sc, sh, tr, tiles_per_group):
    n, d = x.shape
    row = pl.BlockSpec((tr, d), lambda i: (i, 0))
    vec = pl.BlockSpec((1, d), lambda i: (0, 0))
    return pl.pallas_call(
        _resid_kernel,
        grid=(n // tr,),
        in_specs=[row, row, vec, _mod_spec(gt, tiles_per_group), vec,
                  _mod_spec(sc, tiles_per_group), _mod_spec(sh, tiles_per_group)],
        out_specs=[row, row, pl.BlockSpec((d, tr), lambda i: (0, i))],
        out_shape=[jax.ShapeDtypeStruct((n, d), F32),
                   jax.ShapeDtypeStruct((n, d), BF16),
                   jax.ShapeDtypeStruct((d, n), BF16)],
        compiler_params=_params("parallel"),
        name="resid_prenorm_ch",
    )(x, m, gpost.reshape(1, d), gt, gpre.reshape(1, d), sc, sh)


def _final_kernel(x_ref, ft_ref, gpost_ref, gt_ref, o_ref):
    f = ft_ref[...].T
    o_ref[...] = x_ref[...] + gt_ref[...] * (_rms(f) * gpost_ref[...])


def _final_resid(x, ft, gpost, gt, tr, tiles_per_group):
    n, d = x.shape
    row = pl.BlockSpec((tr, d), lambda i: (i, 0))
    return pl.pallas_call(
        _final_kernel,
        grid=(n // tr,),
        in_specs=[row, pl.BlockSpec((d, tr), lambda i: (0, i)),
                  pl.BlockSpec((1, d), lambda i: (0, 0)), _mod_spec(gt, tiles_per_group)],
        out_specs=row,
        out_shape=jax.ShapeDtypeStruct((n, d), F32),
        compiler_params=_params("parallel"),
        name="final_resid",
    )(x, ft, gpost.reshape(1, d), gt)


def _mm_kernel(a_ref, w_ref, o_ref):
    o_ref[...] = jnp.dot(a_ref[...].astype(BF16), w_ref[...].astype(BF16),
                         preferred_element_type=F32).astype(o_ref.dtype)


def _matmul(a, w, tm, tn, out_dtype=F32, name="matmul"):
    m, k = a.shape
    n = w.shape[1]
    tm = min(tm, m)
    return pl.pallas_call(
        _mm_kernel,
        grid=(m // tm, n // tn),
        in_specs=[pl.BlockSpec((tm, k), lambda i, j: (i, 0)),
                  pl.BlockSpec((k, tn), lambda i, j: (0, j))],
        out_specs=pl.BlockSpec((tm, tn), lambda i, j: (i, j)),
        out_shape=jax.ShapeDtypeStruct((m, n), out_dtype),
        compiler_params=_params("parallel", "parallel"),
        name=name,
    )(a, w)


def _glu_kernel(y_ref, w_ref, b_ref, yj_ref, o_ref):
    acc = jnp.dot(y_ref[...].astype(BF16), w_ref[...].astype(BF16), preferred_element_type=F32)
    o_ref[...] = (yj_ref[...] * jax.nn.sigmoid(acc + b_ref[...])).astype(BF16)


def _glu(y, w, b, tm, tn):
    m, k = y.shape
    n = w.shape[1]
    tm = min(tm, m)
    return pl.pallas_call(
        _glu_kernel,
        grid=(m // tm, n // tn),
        in_specs=[pl.BlockSpec((tm, k), lambda i, j: (i, 0)),
                  pl.BlockSpec((k, tn), lambda i, j: (0, j)),
                  pl.BlockSpec((1, tn), lambda i, j: (0, j)),
                  pl.BlockSpec((tm, tn), lambda i, j: (i, j))],
        out_specs=pl.BlockSpec((tm, tn), lambda i, j: (i, j)),
        out_shape=jax.ShapeDtypeStruct((m, n), BF16),
        compiler_params=_params("parallel", "parallel"),
        name="s5_glu",
    )(y, w, b.reshape(1, n), y)


def _branch_kernel(ya_ref, yb_ref, wa_ref, wb_ref, ga_ref, gb_ref, o_ref):
    a = jnp.dot(ya_ref[...], wa_ref[...].astype(BF16), preferred_element_type=F32)
    b = jnp.dot(yb_ref[...], wb_ref[...].astype(BF16), preferred_element_type=F32)
    o_ref[...] = (jax.nn.sigmoid(ga_ref[...]) * a + jax.nn.sigmoid(gb_ref[...]) * b).astype(BF16)


def _branch_merge(ya, yb, w_branch, proj, tm, tn):
    m = ya.shape[0]
    n = w_branch.shape[1]
    tm = min(tm, m)
    ga_blk = (2 * A_WIDTH + B_WIDTH) // tn
    gb_blk = ga_blk + D_MODEL // tn
    return pl.pallas_call(
        _branch_kernel,
        grid=(n // tn, m // tm),
        in_specs=[pl.BlockSpec((tm, A_WIDTH), lambda j, i: (i, 0)),
                  pl.BlockSpec((tm, B_WIDTH), lambda j, i: (i, 0)),
                  pl.BlockSpec((A_WIDTH, tn), lambda j, i: (0, j)),
                  pl.BlockSpec((B_WIDTH, tn), lambda j, i: (A_WIDTH // B_WIDTH, j)),
                  pl.BlockSpec((tm, tn), lambda j, i: (i, ga_blk + j)),
                  pl.BlockSpec((tm, tn), lambda j, i: (i, gb_blk + j))],
        out_specs=pl.BlockSpec((tm, tn), lambda j, i: (i, j)),
        out_shape=jax.ShapeDtypeStruct((m, n), BF16),
        compiler_params=_params("parallel", "parallel"),
        name="branch_merge",
    )(ya, yb, w_branch, w_branch, proj, proj)


def _mixer_a_kernel(za_ref, ws_ref, bias_ref, lng_ref, lnb_ref, ya_ref, *v_refs, rows):
    z = jax.nn.gelu(za_ref[...])
    u = z[:, :A_WIDTH]
    v = z[:, A_WIDTH:]
    d = v - jnp.mean(v, axis=-1, keepdims=True)
    vn = d * lax.rsqrt(jnp.mean(d * d, axis=-1, keepdims=True) + EPS) * lng_ref[...] + lnb_ref[...]
    if v_refs:
        v_refs[0][...] = vn
    if rows < A_CHUNK:
        vn = jnp.concatenate([vn, jnp.zeros((A_CHUNK - rows, A_WIDTH), F32)], axis=0)
    vb = vn.astype(BF16)
    blk_i = lax.broadcasted_iota(jnp.int32, (A_CHUNK, A_CHUNK), 0) // CHUNK
    blk_j = lax.broadcasted_iota(jnp.int32, (A_CHUNK, A_CHUNK), 1) // CHUNK
    causal = blk_i >= blk_j
    for g in range(A_GROUPS):
        cols = slice(g * A_GROUP_DIM, (g + 1) * A_GROUP_DIM)
        w = jnp.where(causal, ws_ref[g], 0.0).astype(BF16)
        mixed = jnp.dot(w, vb[:, cols], preferred_element_type=F32) + bias_ref[:, cols]
        ya_ref[:, cols] = (u[:, cols] * mixed[:rows]).astype(BF16)


def _mixer_a(proj3, ws, bs, ln_g, ln_b, rows, emit_v):
    nb, t, _ = proj3.shape
    bias = jnp.repeat(bs.T, A_GROUP_DIM, axis=1)
    out_shape = [jax.ShapeDtypeStruct((nb, t, A_WIDTH), BF16)]
    out_specs = [pl.BlockSpec((None, rows, A_WIDTH), lambda b, c: (b, c, 0))]
    if emit_v:
        out_shape.append(jax.ShapeDtypeStruct((nb, t, A_WIDTH), F32))
        out_specs.append(pl.BlockSpec((None, rows, A_WIDTH), lambda b, c: (b, c, 0)))
    return pl.pallas_call(
        functools.partial(_mixer_a_kernel, rows=rows),
        grid=(nb, t // rows),
        in_specs=[pl.BlockSpec((None, rows, 2 * A_WIDTH), lambda b, c: (b, c, 0)),
                  pl.BlockSpec((A_GROUPS, A_CHUNK, A_CHUNK), lambda b, c: (0, 0, 0)),
                  pl.BlockSpec((A_CHUNK, A_WIDTH), lambda b, c: (0, 0)),
                  pl.BlockSpec((1, A_WIDTH), lambda b, c: (0, 0)),
                  pl.BlockSpec((1, A_WIDTH), lambda b, c: (0, 0))],
        out_specs=out_specs,
        out_shape=out_shape,
        compiler_params=_params("parallel", "parallel"),
        name="mixer_a",
    )(proj3, ws, bias, ln_g.reshape(1, A_WIDTH), ln_b.reshape(1, A_WIDTH))


def _s5_discretize(lam_re, lam_im, log_dt, b_re, b_im, c_re, c_im):
    dt = jnp.exp(log_dt.astype(F32))[:, None]
    lr, li = lam_re.astype(F32), lam_im.astype(F32)
    mag = jnp.exp(lr * dt)
    ab_re, ab_im = mag * jnp.cos(li * dt), mag * jnp.sin(li * dt)
    den = lr * lr + li * li
    num_re = ab_re - 1.0
    coef_re = (num_re * lr + ab_im * li) / den
    coef_im = (ab_im * lr - num_re * li) / den
    br, bi = b_re.astype(F32), b_im.astype(F32)
    bb_re = coef_re[..., None] * br - coef_im[..., None] * bi
    bb_im = coef_re[..., None] * bi + coef_im[..., None] * br
    eye = jnp.eye(S5_GROUPS_PER_BLOCK, dtype=F32)

    def pack_in(bb):
        t = bb.reshape(S5_BLOCKS, S5_GROUPS_PER_BLOCK, B_STATE, B_GROUP_DIM).transpose(0, 1, 3, 2)
        return jnp.einsum("kgcp,gm->kgcmp", t, eye).reshape(S5_BLOCKS, V7X_LANES, S5_BLOCK_STATES)

    def pack_out(cc):
        t = cc.reshape(S5_BLOCKS, S5_GROUPS_PER_BLOCK, B_GROUP_DIM, B_STATE).transpose(0, 1, 3, 2)
        return jnp.einsum("kgpc,gm->kgpmc", t, eye).reshape(S5_BLOCKS, S5_BLOCK_STATES, V7X_LANES)

    w_in = jnp.concatenate([pack_in(bb_re), pack_in(bb_im)], axis=-1).astype(BF16)
    w_out = jnp.concatenate([pack_out(c_re.astype(F32)), -pack_out(c_im.astype(F32))], axis=1).astype(BF16)
    return ab_re.reshape(1, S5_STATES), ab_im.reshape(1, S5_STATES), w_in, w_out


def _s5_kernel(xb_ref, perm_ref, unperm_ref, win_ref, wout_ref, are_ref, aim_ref, d_ref, h0r_ref, h0i_ref,
               ys_ref, hr_ref, hi_ref, sre, sim, g_tm, *, nb):
    @pl.when(pl.program_id(0) == 0)
    def _():
        hr_ref[...] = h0r_ref[...]
        hi_ref[...] = h0i_ref[...]

    x_sm = xb_ref[...].reshape(perm_ref.shape[0], B_WIDTH)
    x_hi = x_sm.astype(BF16)
    rest = x_sm - x_hi.astype(F32)
    x_mid = rest.astype(BF16)
    x_lo = (rest - x_mid.astype(F32)).astype(BF16)
    perm = perm_ref[...]
    x_top = jnp.dot(perm, x_hi, preferred_element_type=F32)
    x = (x_top + jnp.dot(perm, x_mid, preferred_element_type=F32)
         + jnp.dot(perm, x_lo, preferred_element_type=F32))
    xbf = x_top.astype(BF16)
    for k in range(S5_BLOCKS):
        drive = jnp.dot(xbf[:, k * V7X_LANES:(k + 1) * V7X_LANES], win_ref[k], preferred_element_type=F32)
        sre[:, k * S5_BLOCK_STATES:(k + 1) * S5_BLOCK_STATES] = drive[:, :S5_BLOCK_STATES]
        sim[:, k * S5_BLOCK_STATES:(k + 1) * S5_BLOCK_STATES] = drive[:, S5_BLOCK_STATES:]

    a_re = are_ref[...]
    a_im = aim_ref[...]
    sub = lax.broadcasted_iota(jnp.int32, (V7X_SUBLANES, S5_STATES), 0)
    for i in range(x.shape[0] // V7X_SUBLANES):
        rows = slice(i * V7X_SUBLANES, (i + 1) * V7X_SUBLANES)
        prev = slice((i - 1) * V7X_SUBLANES, i * V7X_SUBLANES)
        c_re = hr_ref[...] if i == 0 else sre[prev, :]
        c_im = hi_ref[...] if i == 0 else sim[prev, :]
        d_re = sre[rows, :]
        d_im = sim[rows, :]
        o_re = o_im = None
        for s in range(V7X_SUBLANES // nb):
            if nb < V7X_SUBLANES:
                c_re = pltpu.roll(c_re, nb, 0)
                c_im = pltpu.roll(c_im, nb, 0)
            n_re = a_re * c_re - a_im * c_im + d_re
            n_im = a_re * c_im + a_im * c_re + d_im
            o_re = n_re if s == 0 else jnp.where(sub >= s * nb, n_re, o_re)
            o_im = n_im if s == 0 else jnp.where(sub >= s * nb, n_im, o_im)
            c_re, c_im = n_re, n_im
        sre[rows, :] = o_re
        sim[rows, :] = o_im
    last = slice(x.shape[0] - V7X_SUBLANES, x.shape[0])
    hr_ref[...] = sre[last, :]
    hi_ref[...] = sim[last, :]

    for k in range(S5_BLOCKS):
        st = slice(k * S5_BLOCK_STATES, (k + 1) * S5_BLOCK_STATES)
        ch = slice(k * V7X_LANES, (k + 1) * V7X_LANES)
        y = (jnp.dot(sre[:, st].astype(BF16), wout_ref[k, :S5_BLOCK_STATES, :], preferred_element_type=F32)
             + jnp.dot(sim[:, st].astype(BF16), wout_ref[k, S5_BLOCK_STATES:, :], preferred_element_type=F32)
             + d_ref[:, ch] * x[:, ch])
        g_tm[:, ch] = jax.nn.gelu(y).astype(BF16)
    g_sm = jnp.dot(unperm_ref[...], g_tm[...], preferred_element_type=F32).astype(BF16)
    ys_ref[...] = g_sm.reshape(ys_ref.shape)


def _s5(proj3, disc, d_skip, h0_re, h0_im, steps):
    nb, t, _ = proj3.shape
    a_re, a_im, w_in, w_out = disc
    const2 = lambda c: (0, 0)
    const3 = lambda c: (0, 0, 0)
    blk = steps * nb
    src = jnp.arange(blk)
    perm = jax.nn.one_hot((src % nb) * steps + src // nb, blk, dtype=BF16)
    pad = jnp.zeros((V7X_SUBLANES - nb, S5_STATES), F32)
    state = pl.BlockSpec((V7X_SUBLANES, S5_STATES), const2)
    xb_blk = 2 * A_WIDTH // B_WIDTH
    ys, h_re, h_im = pl.pallas_call(
        functools.partial(_s5_kernel, nb=nb),
        grid=(t // steps,),
        in_specs=[pl.BlockSpec((nb, steps, B_WIDTH), lambda c: (0, c, xb_blk)),
                  pl.BlockSpec((blk, blk), const2),
                  pl.BlockSpec((blk, blk), const2),
                  pl.BlockSpec(w_in.shape, const3),
                  pl.BlockSpec(w_out.shape, const3),
                  pl.BlockSpec((1, S5_STATES), const2),
                  pl.BlockSpec((1, S5_STATES), const2),
                  pl.BlockSpec((1, B_WIDTH), const2),
                  state, state],
        out_specs=[pl.BlockSpec((nb, steps, B_WIDTH), lambda c: (0, c, 0)), state, state],
        out_shape=[jax.ShapeDtypeStruct((nb, t, B_WIDTH), BF16),
                   jax.ShapeDtypeStruct((V7X_SUBLANES, S5_STATES), F32),
                   jax.ShapeDtypeStruct((V7X_SUBLANES, S5_STATES), F32)],
        scratch_shapes=[pltpu.VMEM((blk, S5_STATES), F32),
                        pltpu.VMEM((blk, S5_STATES), F32),
                        pltpu.VMEM((blk, B_WIDTH), BF16)],
        compiler_params=_params("arbitrary"),
        name="s5_scan",
    )(proj3, perm, perm.T, w_in, w_out, a_re, a_im, d_skip.reshape(1, B_WIDTH),
      jnp.concatenate([pad, h0_re], axis=0), jnp.concatenate([pad, h0_im], axis=0))
    return ys, h_re[V7X_SUBLANES - nb:], h_im[V7X_SUBLANES - nb:]


def _cast_t_kernel(x_ref, o_ref):
    o_ref[...] = x_ref[...].T.astype(BF16)


def _transposed_bf16_blocks(x):
    r, c = x.shape
    tr, tc = PEER_TE, 2048
    return pl.pallas_call(
        _cast_t_kernel, grid=(r // tr, c // tc),
        in_specs=[pl.BlockSpec((tr, tc), lambda i, j: (i, j))],
        out_specs=pl.BlockSpec((None, tc, tr), lambda i, j: (i, j, 0)),
        out_shape=jax.ShapeDtypeStruct((r // tr, c, tr), BF16),
        compiler_params=_params("parallel", "parallel"), name="cast_transpose")(x)


_STAIR_LIMIT = [PEER_TOPK // (a + 1) for a in range(PEER_TOPK)]


def _extract_top(work, order, count, unranked=None):
    big = PEER_UNRANKED
    vals, picks = [], []
    rank = None if unranked is None else jnp.full(work.shape, unranked, F32)
    for r in range(count):
        m = jnp.max(work, axis=0, keepdims=True)
        pick = jnp.min(jnp.where(work == m, order, big), axis=0, keepdims=True)
        hit = order == pick
        work = jnp.where(hit, -jnp.inf, work)
        if rank is not None:
            rank = jnp.where(hit, float(r), rank)
        vals.append(m)
        picks.append(pick)
    return vals, picks, rank


def _extract_top_untied(work, count, unranked):
    vals = []
    rank = jnp.full(work.shape, unranked, F32)
    for r in range(count):
        m = jnp.max(work, axis=0, keepdims=True)
        hit = work == m
        work = jnp.where(hit, -jnp.inf, work)
        rank = jnp.where(hit, float(r), rank)
        vals.append(m)
    removed = jnp.sum(jnp.where(work == -jnp.inf, 1.0, 0.0))
    return vals, rank, removed


def _stack_rows(rows, row_idx):
    out = jnp.broadcast_to(rows[0], row_idx.shape)
    for r in range(1, len(rows)):
        out = jnp.where(row_idx == r, rows[r], out)
    return out


def _topk_kernel(q_ref, keys_ref, e1n_ref, lim_ref, e2_ref, r2_ref, sc_ref, top_ref, rank_ref, *, tm):
    key_idx = lax.broadcasted_iota(jnp.int32, (PEER_NKEYS, tm), 0).astype(F32)
    iota16 = lax.broadcasted_iota(jnp.int32, (PEER_TOPK, tm), 0).astype(F32)
    iota8 = iota16[:8]
    flat = [iota16] + [a * PEER_TOPK + iota8 for a in range(1, 8)] + [(iota8 + 8) * PEER_TOPK]
    flat = jnp.concatenate(flat, axis=0)
    unranked = (-1.0, PEER_UNRANKED)
    halves = [(h, s) for h in range(PEER_HEADS) for s in range(2)]
    removed = 0.0
    for h, s in halves:
        col = (2 * h + s) * PEER_SUBDIM
        qs = q_ref[:, col:col + PEER_SUBDIM].astype(BF16)
        ks = keys_ref[h, s].astype(BF16)
        sc = lax.dot_general(ks, qs, (((1,), (1,)), ((), ())), preferred_element_type=F32)
        vals, rank, gone = _extract_top_untied(sc, PEER_TOPK, unranked[s])
        sc_ref[h, s] = sc
        top_ref[h, s] = _stack_rows(vals, iota16)
        rank_ref[h, s] = rank
        removed = removed + gone

    @pl.when(removed != float(len(halves) * PEER_TOPK * tm))
    def _():
        for h, s in halves:
            vals, _, rank = _extract_top(sc_ref[h, s], key_idx, PEER_TOPK, unranked=unranked[s])
            top_ref[h, s] = _stack_rows(vals, iota16)
            rank_ref[h, s] = rank

    for h in range(PEER_HEADS):
        scores = [sc_ref[h, 0], sc_ref[h, 1]]
        ranks = [rank_ref[h, 0], rank_ref[h, 1]]
        top1, v2_16 = top_ref[h, 0], top_ref[h, 1]
        v1 = [top1[r:r + 1, :] for r in range(PEER_TOPK)]
        v2 = [v2_16[0:1, :]]
        v2_8 = v2_16[:8]
        cand = [v1[0] + v2_16]
        for a in range(1, 8):
            cand.append(jnp.where(iota8 < _STAIR_LIMIT[a], v1[a] + v2_8, -jnp.inf))
        cand.append(top1[8:] + v2[0])
        cand = jnp.concatenate(cand, axis=0)
        cv, picks, _ = _extract_top(cand, flat, PEER_TOPK)
        z = jnp.ones_like(cv[0])
        for r in range(1, PEER_TOPK):
            z = z + jnp.exp(cv[r] - cv[0])
        limit = jnp.full((PEER_TOPK, tm), -1.0, F32)
        for pick in picks:
            row = jnp.floor(pick * (1.0 / PEER_TOPK))
            col = pick - PEER_TOPK * row
            limit = jnp.maximum(limit, jnp.where(iota16 == row, col, -1.0))
        lim = jnp.full((PEER_NKEYS, tm), -1.0, F32)
        for a in range(PEER_TOPK):
            lim = jnp.where(ranks[0] == float(a), limit[a:a + 1, :], lim)
        e1n_ref[h] = jnp.exp(scores[0] - v1[0]) / z
        lim_ref[h] = lim
        e2_ref[h] = jnp.exp(scores[1] - v2[0]).astype(BF16)
        r2_ref[h] = ranks[1].astype(BF16)


def _peer_topk(q, keys, tm):
    n = q.shape[0]
    per_key = pl.BlockSpec((PEER_HEADS, PEER_NKEYS, tm), lambda i: (0, 0, i))
    shape = (PEER_HEADS, PEER_NKEYS, n)
    return pl.pallas_call(
        functools.partial(_topk_kernel, tm=tm),
        grid=(n // tm,),
        in_specs=[pl.BlockSpec((tm, q.shape[1]), lambda i: (i, 0)),
                  pl.BlockSpec(keys.shape, lambda i: (0, 0, 0, 0))],
        out_specs=[per_key] * 4,
        out_shape=[jax.ShapeDtypeStruct(shape, F32), jax.ShapeDtypeStruct(shape, F32),
                   jax.ShapeDtypeStruct(shape, BF16), jax.ShapeDtypeStruct(shape, BF16)],
        scratch_shapes=[pltpu.VMEM((PEER_HEADS, 2, PEER_NKEYS, tm), F32),
                        pltpu.VMEM((PEER_HEADS, 2, PEER_TOPK, tm), F32),
                        pltpu.VMEM((PEER_HEADS, 2, PEER_NKEYS, tm), F32)],
        compiler_params=_params("parallel"),
        name="peer_topk",
    )(q, keys)


def _peer_kernel(ht_ref, u_ref, vt_ref, e1n_ref, lim_ref, e2_ref, r2_ref,
                 o_ref, act_a, act_b, p_a, p_b, row_bc, *, te, nblocks, total):
    j = pl.program_id(0)
    slabs = te // PEER_NKEYS

    tm = o_ref.shape[1]
    tile = (V7X_BF16_ROWS, tm)

    @pl.when(j == 0)
    def _():
        p_a[...] = jnp.zeros_like(p_a)
        act_b[...] = jnp.zeros_like(act_b)

    @pl.when(jnp.maximum(j - 2, 0) % nblocks == 0)
    def _():
        o_ref[...] = jnp.zeros_like(o_ref)

    def step(act_new, act_old, p_new, p_old):
        first = (jnp.clip(j - 1, 0, total - 1) % nblocks) * slabs
        d = ht_ref.shape[0]

        def gate(il, r):
            i1 = first + il
            per_head = []
            for h in range(PEER_HEADS):
                if r == 0:
                    rows_i1 = [jnp.broadcast_to(ref[h, pl.ds(i1, 1), :], tile).astype(BF16)
                               for ref in (e1n_ref, lim_ref)]
                    for a, val in enumerate(rows_i1):
                        row_bc[a, il, h] = val
                else:
                    rows_i1 = [row_bc[a, il, h] for a in range(2)]
                per_head.append(rows_i1)
            for q in range(PEER_GATE_ROWS // V7X_BF16_ROWS):
                k0 = r * PEER_GATE_ROWS + q * V7X_BF16_ROWS
                keys = slice(k0, k0 + V7X_BF16_ROWS)
                w = jnp.zeros(tile, BF16)
                for h in range(PEER_HEADS):
                    e1, lim = per_head[h]
                    w = w + jnp.where(r2_ref[h, keys, :] <= lim, e2_ref[h, keys, :] * e1, jnp.zeros((), BF16))
                rows = slice(il * PEER_NKEYS + k0, il * PEER_NKEYS + k0 + V7X_BF16_ROWS)
                p_new[rows, :] = w * jax.nn.gelu(act_old[rows, :]).astype(BF16)

        def pre_act(kc):
            ks = slice(kc * PEER_K_CHUNK, (kc + 1) * PEER_K_CHUNK)
            part = jnp.dot(u_ref[:, ks].astype(BF16), ht_ref[ks, :], preferred_element_type=F32)
            if kc == 0:
                act_new[...] = part
            else:
                act_new[...] += part

        def accumulate(mc):
            ms = slice(mc * PEER_M_CHUNK, (mc + 1) * PEER_M_CHUNK)
            o_ref[ms, :] += jnp.dot(vt_ref[ms, :], p_old[...], preferred_element_type=F32)

        gates = [(il, r) for il in range(slabs) for r in range(PEER_NKEYS // PEER_GATE_ROWS)]
        n_k, n_m = d // PEER_K_CHUNK, d // PEER_M_CHUNK
        mxu = []
        for kc in range(n_k):
            mxu.append((functools.partial(pre_act, kc), PEER_K_CHUNK * te))
            mxu += [(functools.partial(accumulate, kc * (n_m // n_k) + m), PEER_M_CHUNK * te)
                    for m in range(n_m // n_k)]
        mxu_total = sum(cost for _, cost in mxu)
        issued = done = 0
        for piece, cost in mxu:
            piece()
            issued += cost
            upto = (len(gates) * issued) // mxu_total
            for il, r in gates[done:upto]:
                gate(il, r)
            done = upto

    @pl.when(j % 2 == 0)
    def _():
        step(act_a, act_b, p_b, p_a)

    @pl.when(j % 2 == 1)
    def _():
        step(act_b, act_a, p_a, p_b)


def _peer_dense(ht, expert_u, vt_bf, topk, tm):
    d, n = ht.shape
    nblocks, _, te = vt_bf.shape
    total = (n // tm) * nblocks
    item = lambda f, lag: jnp.clip(f - lag, 0, total - 1)
    per_tile = dict(pipeline_mode=pl.Buffered(1))
    per_key = pl.BlockSpec((PEER_HEADS, PEER_NKEYS, tm), lambda f: (0, 0, item(f, 1) // nblocks), **per_tile)
    return pl.pallas_call(
        functools.partial(_peer_kernel, te=te, nblocks=nblocks, total=total),
        grid=(total + 2,),
        in_specs=[pl.BlockSpec((d, tm), lambda f: (0, item(f, 0) // nblocks), **per_tile),
                  pl.BlockSpec((te, d), lambda f: (item(f, 0) % nblocks, 0)),
                  pl.BlockSpec((None, d, te), lambda f: (item(f, 2) % nblocks, 0, 0))]
                 + [per_key] * 4,
        out_specs=pl.BlockSpec((d, tm), lambda f: (0, item(f, 2) // nblocks)),
        out_shape=jax.ShapeDtypeStruct((d, n), F32),
        scratch_shapes=[pltpu.VMEM((te, tm), F32), pltpu.VMEM((te, tm), F32),
                        pltpu.VMEM((te, tm), BF16), pltpu.VMEM((te, tm), BF16),
                        pltpu.VMEM((2, te // PEER_NKEYS, PEER_HEADS, V7X_BF16_ROWS, tm), BF16)],
        compiler_params=_params("arbitrary"),
        name="peer_dense",
    )(ht, expert_u, vt_bf, *topk)


def _split_mod(mod, tokens_per_stream, per_token):
    if per_token:
        mod = jnp.repeat(mod, tokens_per_stream, axis=0)[None]
    else:
        mod = mod[:, None, :]
    return jnp.split(mod, 6, axis=-1)


def _encoder_layer(x, mod, h0_re, h0_im, lp, peer_w, *, tr, scan_steps, mixer_rows, per_token_mod, emit_v):
    (g_pre_tok, g_post_tok, g_pre_ch, g_post_ch, w_in, a_ws, a_bs, a_ln_g, a_ln_b, disc, s5_d,
     w_glu, b_glu, w_branch, w_out, peer_w_query, peer_sub_keys) = lp
    peer_u, vt_bf = peer_w
    nb, t, d = x.shape
    n = nb * t
    tiles_per_group = 1 if per_token_mod else t // tr
    sh_t, sc_t, gt_t, sh_c, sc_c, gt_c = _split_mod(mod, t, per_token_mod)
    x2 = x.reshape(n, d)

    h = _prenorm(x2, g_pre_tok, sc_t, sh_t, tr, tiles_per_group)
    proj = _matmul(h, w_in, 1024, 512, name="in_proj")
    proj3 = proj.reshape(nb, t, IN_WIDTH)
    mix = _mixer_a(proj3, a_ws, a_bs, a_ln_g, a_ln_b, mixer_rows, emit_v)
    ya = mix[0].reshape(n, A_WIDTH)
    v_rows = mix[1] if emit_v else None
    ys, h_re, h_im = _s5(proj3, disc, s5_d, h0_re, h0_im, scan_steps)
    yb = _glu(ys.reshape(n, B_WIDTH), w_glu, b_glu, 1024, 512)
    merged = _branch_merge(ya, yb, w_branch, proj, 1024, 512)
    m = _matmul(merged, w_out, 1024, 512, name="out_proj")

    x1, h2, h2t = _resid_prenorm(x2, m, g_post_tok, gt_t, g_pre_ch, sc_c, sh_c, tr, tiles_per_group)
    q = _matmul(h2, peer_w_query, 1024, 512, name="peer_query")
    topk = _peer_topk(q, peer_sub_keys, min(128, n))
    ft = _peer_dense(h2t, peer_u, vt_bf, topk, min(512, n))
    y = _final_resid(x1, ft, g_post_ch, gt_c, tr, tiles_per_group)
    return (y.reshape(nb, t, d), h_re.reshape(nb, B_GROUPS, B_STATE), h_im.reshape(nb, B_GROUPS, B_STATE), v_rows)


def kernel(x_prompt, x_sample, c_prompt, c_sample, state_ssm_re, state_ssm_im, w_ada, b_ada, g_pre_tok,
           g_post_tok, g_pre_ch, g_post_ch, w_in, a_ws, a_bs, a_ln_g, a_ln_b, s5_lam_re, s5_lam_im,
           s5_log_dt, s5_b_re, s5_b_im, s5_c_re, s5_c_im, s5_d, w_glu, b_glu, w_branch, w_out,
           peer_w_query, peer_sub_keys, peer_u, peer_v):
    depth = w_ada.shape[0]
    nb_p, t_p, _ = x_prompt.shape
    nb_s, t_s, _ = x_sample.shape
    y_p, y_s = x_prompt, x_sample
    re_p, im_p, re_s, im_s, v_s = [], [], [], [], []
    pad = (-(nb_p + nb_s)) % 8
    for l in range(depth):
        c_all = jnp.concatenate([c_prompt, c_sample, jnp.zeros((pad, D_MODEL), F32)], axis=0)
        mod = _ada(c_all, w_ada[l], b_ada[l])
        disc = _s5_discretize(s5_lam_re[l], s5_lam_im[l], s5_log_dt[l], s5_b_re[l], s5_b_im[l],
                              s5_c_re[l], s5_c_im[l])
        lp = (g_pre_tok[l], g_post_tok[l], g_pre_ch[l], g_post_ch[l], w_in[l], a_ws[l], a_bs[l],
              a_ln_g[l], a_ln_b[l], disc, s5_d[l], w_glu[l], b_glu[l], w_branch[l], w_out[l],
              peer_w_query[l], peer_sub_keys[l])
        peer_w = (peer_u[l], _transposed_bf16_blocks(peer_v[l]))
        zeros = jnp.zeros((nb_p, S5_STATES), F32)
        y_p, hr, hi, _ = _encoder_layer(
            y_p, mod[:nb_p], zeros, zeros, lp, peer_w,
            tr=256, scan_steps=32, mixer_rows=A_CHUNK, per_token_mod=False, emit_v=False)
        re_p.append(hr)
        im_p.append(hi)
        y_s, hr, hi, v_rows = _encoder_layer(
            y_s, mod[nb_p:nb_p + nb_s], state_ssm_re[l].reshape(nb_s, S5_STATES),
            state_ssm_im[l].reshape(nb_s, S5_STATES), lp, peer_w,
            tr=nb_s * t_s, scan_steps=t_s, mixer_rows=t_s, per_token_mod=True, emit_v=True)
        re_s.append(hr)
        im_s.append(hi)
        v_s.append(v_rows)
    return (y_p, y_s, jnp.stack(re_p), jnp.stack(im_p), jnp.stack(re_s), jnp.stack(im_s), jnp.stack(v_s))
```

```python
import functools

import jax
import jax.numpy as jnp
from jax import lax
from jax.experimental import pallas as pl
from jax.experimental.pallas import tpu as pltpu

F32 = jnp.float32
BF16 = jnp.bfloat16

D_MODEL = 4096
CHUNK = 64
A_CHUNK = 128
A_GROUPS = 16
A_GROUP_DIM = 128
A_WIDTH = A_GROUPS * A_GROUP_DIM
B_GROUP_DIM = 16
B_GROUPS = 128
B_WIDTH = B_GROUPS * B_GROUP_DIM
B_STATE = 64
S5_STATES = B_GROUPS * B_STATE
S5_GROUPS_PER_BLOCK = 8
S5_BLOCKS = B_GROUPS // S5_GROUPS_PER_BLOCK
S5_BLOCK_STATES = S5_GROUPS_PER_BLOCK * B_STATE
IN_WIDTH = 2 * A_WIDTH + B_WIDTH + 2 * D_MODEL
PEER_HEADS = 8
PEER_SUBDIM = 128
PEER_NKEYS = 128
PEER_EXPERTS = PEER_NKEYS * PEER_NKEYS
PEER_TOPK = 16
PEER_UNRANKED = 1e9
PEER_TE = 512
PEER_GATE_ROWS = 16
PEER_K_CHUNK = 256
PEER_M_CHUNK = 128
EPS = 1e-6

V7X_LANES = 128
V7X_SUBLANES = 8
V7X_BF16_ROWS = 16
V7X_VMEM_BYTES = 64 * 1024 * 1024
VMEM_LIMIT_BYTES = 56 * 1024 * 1024


def _params(*semantics, flags=None):
    return pltpu.CompilerParams(dimension_semantics=semantics, vmem_limit_bytes=VMEM_LIMIT_BYTES, flags=flags)


def _rms(x):
    return x * lax.rsqrt(jnp.mean(x * x, axis=-1, keepdims=True) + EPS)


def _ada_kernel(c_ref, w_ref, b_ref, o_ref):
    s = jax.nn.silu(c_ref[...]).astype(BF16)
    o_ref[...] = jnp.dot(s, w_ref[...].astype(BF16), preferred_element_type=F32) + b_ref[...]


def _ada(c, w, b):
    rows, d = c.shape
    n = w.shape[1]
    tn = 512
    return pl.pallas_call(
        _ada_kernel,
        grid=(n // tn,),
        in_specs=[pl.BlockSpec((rows, d), lambda j: (0, 0)),
                  pl.BlockSpec((d, tn), lambda j: (0, j)),
                  pl.BlockSpec((1, tn), lambda j: (0, j))],
        out_specs=pl.BlockSpec((rows, tn), lambda j: (0, j)),
        out_shape=jax.ShapeDtypeStruct((rows, n), F32),
        compiler_params=_params("parallel"),
        name="ada_mod",
    )(c, w, b.reshape(1, n))


def _mod_spec(mod, tiles_per_group):
    return pl.BlockSpec((None,) + mod.shape[1:], lambda i: (i // tiles_per_group, 0, 0))


def _prenorm_kernel(x_ref, g_ref, sc_ref, sh_ref, o_ref):
    h = _rms(x_ref[...]) * g_ref[...] * (1.0 + sc_ref[...]) + sh_ref[...]
    o_ref[...] = h.astype(BF16)


def _prenorm(x, g, sc, sh, tr, tiles_per_group):
    n, d = x.shape
    return pl.pallas_call(
        _prenorm_kernel,
        grid=(n // tr,),
        in_specs=[pl.BlockSpec((tr, d), lambda i: (i, 0)),
                  pl.BlockSpec((1, d), lambda i: (0, 0)),
                  _mod_spec(sc, tiles_per_group),
                  _mod_spec(sh, tiles_per_group)],
        out_specs=pl.BlockSpec((tr, d), lambda i: (i, 0)),
        out_shape=jax.ShapeDtypeStruct((n, d), BF16),
        compiler_params=_params("parallel"),
        name="prenorm_tok",
    )(x, g.reshape(1, d), sc, sh)


def _resid_kernel(x_ref, m_ref, gpost_ref, gt_ref, gpre_ref, sc_ref, sh_ref, x1_ref, h_ref, ht_ref):
    x1 = x_ref[...] + gt_ref[...] * (_rms(m_ref[...]) * gpost_ref[...])
    x1_ref[...] = x1
    h = _rms(x1) * gpre_ref[...] * (1.0 + sc_ref[...]) + sh_ref[...]
    h_ref[...] = h.astype(BF16)
    ht_ref[...] = h.T.astype(BF16)


def _resid_prenorm(x, m, gpost, gt, gpre, sc, sh, tr, tiles_per_group):
    n, d = x.shape
    row = pl.BlockSpec((tr, d), lambda i: (i, 0))
    vec = pl.BlockSpec((1, d), lambda i: (0, 0))
    return pl.pallas_call(
        _resid_kernel,
        grid=(n // tr,),
        in_specs=[row, row, vec, _mod_spec(gt, tiles_per_group), vec,
                  _mod_spec(sc, tiles_per_group), _mod_spec(sh, tiles_per_group)],
        out_specs=[row, row, pl.BlockSpec((d, tr), lambda i: (0, i))],
        out_shape=[jax.ShapeDtypeStruct((n, d), F32),
                   jax.ShapeDtypeStruct((n, d), BF16),
                   jax.ShapeDtypeStruct((d, n), BF16)],
        compiler_params=_params("parallel"),
        name="resid_prenorm_ch",
    )(x, m, gpost.reshape(1, d), gt, gpre.reshape(1, d), sc, sh)


def _final_kernel(x_ref, ft_ref, gpost_ref, gt_ref, o_ref):
    f = ft_ref[...].T
    o_ref[...] = x_ref[...] + gt_ref[...] * (_rms(f) * gpost_ref[...])


def _final_resid(x, ft, gpost, gt, tr, tiles_per_group):
    n, d = x.shape
    row = pl.BlockSpec((tr, d), lambda i: (i, 0))
    return pl.pallas_call(
        _final_kernel,
        grid=(n // tr,),
        in_specs=[row, pl.BlockSpec((d, tr), lambda i: (0, i)),
                  pl.BlockSpec((1, d), lambda i: (0, 0)), _mod_spec(gt, tiles_per_group)],
        out_specs=row,
        out_shape=jax.ShapeDtypeStruct((n, d), F32),
        compiler_params=_params("parallel"),
        name="final_resid",
    )(x, ft, gpost.reshape(1, d), gt)


def _mm_kernel(a_ref, w_ref, o_ref):
    o_ref[...] = jnp.dot(a_ref[...].astype(BF16), w_ref[...].astype(BF16),
                         preferred_element_type=F32).astype(o_ref.dtype)


def _matmul(a, w, tm, tn, out_dtype=F32, name="matmul"):
    m, k = a.shape
    n = w.shape[1]
    tm = min(tm, m)
    return pl.pallas_call(
        _mm_kernel,
        grid=(m // tm, n // tn),
        in_specs=[pl.BlockSpec((tm, k), lambda i, j: (i, 0), pipeline_mode=pl.Buffered(1)),
                  pl.BlockSpec((k, tn), lambda i, j: (0, j))],
        out_specs=pl.BlockSpec((tm, tn), lambda i, j: (i, j)),
        out_shape=jax.ShapeDtypeStruct((m, n), out_dtype),
        compiler_params=_params("parallel", "parallel"),
        name=name,
    )(a, w)


def _glu_kernel(y_ref, w_ref, b_ref, yj_ref, o_ref):
    acc = jnp.dot(y_ref[...].astype(BF16), w_ref[...].astype(BF16), preferred_element_type=F32)
    o_ref[...] = (yj_ref[...] * jax.nn.sigmoid(acc + b_ref[...])).astype(BF16)


def _glu(y, w, b, tm, tn):
    m, k = y.shape
    n = w.shape[1]
    tm = min(tm, m)
    return pl.pallas_call(
        _glu_kernel,
        grid=(m // tm, n // tn),
        in_specs=[pl.BlockSpec((tm, k), lambda i, j: (i, 0)),
                  pl.BlockSpec((k, tn), lambda i, j: (0, j)),
                  pl.BlockSpec((1, tn), lambda i, j: (0, j)),
                  pl.BlockSpec((tm, tn), lambda i, j: (i, j))],
        out_specs=pl.BlockSpec((tm, tn), lambda i, j: (i, j)),
        out_shape=jax.ShapeDtypeStruct((m, n), BF16),
        compiler_params=_params("parallel", "parallel"),
        name="s5_glu",
    )(y, w, b.reshape(1, n), y)


def _branch_kernel(ya_ref, yb_ref, wa_ref, wb_ref, ga_ref, gb_ref, o_ref):
    a = jnp.dot(ya_ref[...], wa_ref[...].astype(BF16), preferred_element_type=F32)
    b = jnp.dot(yb_ref[...], wb_ref[...].astype(BF16), preferred_element_type=F32)
    o_ref[...] = (jax.nn.sigmoid(ga_ref[...]) * a + jax.nn.sigmoid(gb_ref[...]) * b).astype(BF16)


def _branch_merge(ya, yb, w_branch, proj, tm, tn):
    m = ya.shape[0]
    n = w_branch.shape[1]
    tm = min(tm, m)
    ga_blk = (2 * A_WIDTH + B_WIDTH) // tn
    gb_blk = ga_blk + D_MODEL // tn
    return pl.pallas_call(
        _branch_kernel,
        grid=(n // tn, m // tm),
        in_specs=[pl.BlockSpec((tm, A_WIDTH), lambda j, i: (i, 0)),
                  pl.BlockSpec((tm, B_WIDTH), lambda j, i: (i, 0)),
                  pl.BlockSpec((A_WIDTH, tn), lambda j, i: (0, j)),
                  pl.BlockSpec((B_WIDTH, tn), lambda j, i: (A_WIDTH // B_WIDTH, j)),
                  pl.BlockSpec((tm, tn), lambda j, i: (i, ga_blk + j)),
                  pl.BlockSpec((tm, tn), lambda j, i: (i, gb_blk + j))],
        out_specs=pl.BlockSpec((tm, tn), lambda j, i: (i, j)),
        out_shape=jax.ShapeDtypeStruct((m, n), BF16),
        compiler_params=_params("parallel", "parallel"),
        name="branch_merge",
    )(ya, yb, w_branch, w_branch, proj, proj)


def _mixer_a_kernel(za_ref, ws_ref, bias_ref, lng_ref, lnb_ref, ya_ref, *v_refs, rows):
    z = jax.nn.gelu(za_ref[...])
    u = z[:, :A_WIDTH]
    v = z[:, A_WIDTH:]
    d = v - jnp.mean(v, axis=-1, keepdims=True)
    vn = d * lax.rsqrt(jnp.mean(d * d, axis=-1, keepdims=True) + EPS) * lng_ref[...] + lnb_ref[...]
    if v_refs:
        v_refs[0][...] = vn
    if rows < A_CHUNK:
        vn = jnp.concatenate([vn, jnp.zeros((A_CHUNK - rows, A_WIDTH), F32)], axis=0)
    vb = vn.astype(BF16)
    blk_i = lax.broadcasted_iota(jnp.int32, (A_CHUNK, A_CHUNK), 0) // CHUNK
    blk_j = lax.broadcasted_iota(jnp.int32, (A_CHUNK, A_CHUNK), 1) // CHUNK
    causal = blk_i >= blk_j
    for g in range(A_GROUPS):
        cols = slice(g * A_GROUP_DIM, (g + 1) * A_GROUP_DIM)
        w = jnp.where(causal, ws_ref[g], 0.0).astype(BF16)
        mixed = jnp.dot(w, vb[:, cols], preferred_element_type=F32) + bias_ref[:, cols]
        ya_ref[:, cols] = (u[:, cols] * mixed[:rows]).astype(BF16)


def _mixer_a(proj3, ws, bs, ln_g, ln_b, rows, emit_v):
    nb, t, _ = proj3.shape
    bias = jnp.repeat(bs.T, A_GROUP_DIM, axis=1)
    out_shape = [jax.ShapeDtypeStruct((nb, t, A_WIDTH), BF16)]
    out_specs = [pl.BlockSpec((None, rows, A_WIDTH), lambda b, c: (b, c, 0))]
    if emit_v:
        out_shape.append(jax.ShapeDtypeStruct((nb, t, A_WIDTH), F32))
        out_specs.append(pl.BlockSpec((None, rows, A_WIDTH), lambda b, c: (b, c, 0)))
    return pl.pallas_call(
        functools.partial(_mixer_a_kernel, rows=rows),
        grid=(nb, t // rows),
        in_specs=[pl.BlockSpec((None, rows, 2 * A_WIDTH), lambda b, c: (b, c, 0)),
                  pl.BlockSpec((A_GROUPS, A_CHUNK, A_CHUNK), lambda b, c: (0, 0, 0)),
                  pl.BlockSpec((A_CHUNK, A_WIDTH), lambda b, c: (0, 0)),
                  pl.BlockSpec((1, A_WIDTH), lambda b, c: (0, 0)),
                  pl.BlockSpec((1, A_WIDTH), lambda b, c: (0, 0))],
        out_specs=out_specs,
        out_shape=out_shape,
        compiler_params=_params("parallel", "parallel"),
        name="mixer_a",
    )(proj3, ws, bias, ln_g.reshape(1, A_WIDTH), ln_b.reshape(1, A_WIDTH))


def _s5_discretize(lam_re, lam_im, log_dt, b_re, b_im, c_re, c_im):
    dt = jnp.exp(log_dt.astype(F32))[:, None]
    lr, li = lam_re.astype(F32), lam_im.astype(F32)
    mag = jnp.exp(lr * dt)
    ab_re, ab_im = mag * jnp.cos(li * dt), mag * jnp.sin(li * dt)
    den = lr * lr + li * li
    num_re = ab_re - 1.0
    coef_re = (num_re * lr + ab_im * li) / den
    coef_im = (ab_im * lr - num_re * li) / den
    br, bi = b_re.astype(F32), b_im.astype(F32)
    bb_re = coef_re[..., None] * br - coef_im[..., None] * bi
    bb_im = coef_re[..., None] * bi + coef_im[..., None] * br
    eye = jnp.eye(S5_GROUPS_PER_BLOCK, dtype=F32)

    def pack_in(bb):
        t = bb.reshape(S5_BLOCKS, S5_GROUPS_PER_BLOCK, B_STATE, B_GROUP_DIM).transpose(0, 1, 3, 2)
        return jnp.einsum("kgcp,gm->kgcmp", t, eye).reshape(S5_BLOCKS, V7X_LANES, S5_BLOCK_STATES)

    def pack_out(cc):
        t = cc.reshape(S5_BLOCKS, S5_GROUPS_PER_BLOCK, B_GROUP_DIM, B_STATE).transpose(0, 1, 3, 2)
        return jnp.einsum("kgpc,gm->kgpmc", t, eye).reshape(S5_BLOCKS, S5_BLOCK_STATES, V7X_LANES)

    w_in = jnp.concatenate([pack_in(bb_re), pack_in(bb_im)], axis=-1).astype(BF16)
    w_out = jnp.concatenate([pack_out(c_re.astype(F32)), -pack_out(c_im.astype(F32))], axis=1).astype(BF16)
    return ab_re.reshape(1, S5_STATES), ab_im.reshape(1, S5_STATES), w_in, w_out


def _s5_kernel(xb_ref, perm_ref, unperm_ref, win_ref, wout_ref, are_ref, aim_ref, d_ref, h0r_ref, h0i_ref,
               ys_ref, hr_ref, hi_ref, sre, sim, g_tm, *, nb):
    @pl.when(pl.program_id(0) == 0)
    def _():
        hr_ref[...] = h0r_ref[...]
        hi_ref[...] = h0i_ref[...]

    x_sm = xb_ref[...].reshape(perm_ref.shape[0], B_WIDTH)
    x_hi = x_sm.astype(BF16)
    rest = x_sm - x_hi.astype(F32)
    x_mid = rest.astype(BF16)
    x_lo = (rest - x_mid.astype(F32)).astype(BF16)
    perm = perm_ref[...]
    x_top = jnp.dot(perm, x_hi, preferred_element_type=F32)
    x = (x_top + jnp.dot(perm, x_mid, preferred_element_type=F32)
         + jnp.dot(perm, x_lo, preferred_element_type=F32))
    xbf = x_top.astype(BF16)
    for k in range(S5_BLOCKS):
        drive = jnp.dot(xbf[:, k * V7X_LANES:(k + 1) * V7X_LANES], win_ref[k], preferred_element_type=F32)
        sre[:, k * S5_BLOCK_STATES:(k + 1) * S5_BLOCK_STATES] = drive[:, :S5_BLOCK_STATES]
        sim[:, k * S5_BLOCK_STATES:(k + 1) * S5_BLOCK_STATES] = drive[:, S5_BLOCK_STATES:]

    a_re = are_ref[...]
    a_im = aim_ref[...]
    sub = lax.broadcasted_iota(jnp.int32, (V7X_SUBLANES, S5_STATES), 0)
    for i in range(x.shape[0] // V7X_SUBLANES):
        rows = slice(i * V7X_SUBLANES, (i + 1) * V7X_SUBLANES)
        prev = slice((i - 1) * V7X_SUBLANES, i * V7X_SUBLANES)
        c_re = hr_ref[...] if i == 0 else sre[prev, :]
        c_im = hi_ref[...] if i == 0 else sim[prev, :]
        d_re = sre[rows, :]
        d_im = sim[rows, :]
        o_re = o_im = None
        for s in range(V7X_SUBLANES // nb):
            if nb < V7X_SUBLANES:
                c_re = pltpu.roll(c_re, nb, 0)
                c_im = pltpu.roll(c_im, nb, 0)
            n_re = a_re * c_re - a_im * c_im + d_re
            n_im = a_re * c_im + a_im * c_re + d_im
            o_re = n_re if s == 0 else jnp.where(sub >= s * nb, n_re, o_re)
            o_im = n_im if s == 0 else jnp.where(sub >= s * nb, n_im, o_im)
            c_re, c_im = n_re, n_im
        sre[rows, :] = o_re
        sim[rows, :] = o_im
    last = slice(x.shape[0] - V7X_SUBLANES, x.shape[0])
    hr_ref[...] = sre[last, :]
    hi_ref[...] = sim[last, :]

    for k in range(S5_BLOCKS):
        st = slice(k * S5_BLOCK_STATES, (k + 1) * S5_BLOCK_STATES)
        ch = slice(k * V7X_LANES, (k + 1) * V7X_LANES)
        y = (jnp.dot(sre[:, st].astype(BF16), wout_ref[k, :S5_BLOCK_STATES, :], preferred_element_type=F32)
             + jnp.dot(sim[:, st].astype(BF16), wout_ref[k, S5_BLOCK_STATES:, :], preferred_element_type=F32)
             + d_ref[:, ch] * x[:, ch])
        g_tm[:, ch] = jax.nn.gelu(y).astype(BF16)
    g_sm = jnp.dot(unperm_ref[...], g_tm[...], preferred_element_type=F32).astype(BF16)
    ys_ref[...] = g_sm.reshape(ys_ref.shape)


def _s5(proj3, disc, d_skip, h0_re, h0_im, steps):
    nb, t, _ = proj3.shape
    a_re, a_im, w_in, w_out = disc
    const2 = lambda c: (0, 0)
    const3 = lambda c: (0, 0, 0)
    blk = steps * nb
    src = jnp.arange(blk)
    perm = jax.nn.one_hot((src % nb) * steps + src // nb, blk, dtype=BF16)
    pad = jnp.zeros((V7X_SUBLANES - nb, S5_STATES), F32)
    state = pl.BlockSpec((V7X_SUBLANES, S5_STATES), const2)
    xb_blk = 2 * A_WIDTH // B_WIDTH
    ys, h_re, h_im = pl.pallas_call(
        functools.partial(_s5_kernel, nb=nb),
        grid=(t // steps,),
        in_specs=[pl.BlockSpec((nb, steps, B_WIDTH), lambda c: (0, c, xb_blk)),
                  pl.BlockSpec((blk, blk), const2),
                  pl.BlockSpec((blk, blk), const2),
                  pl.BlockSpec(w_in.shape, const3),
                  pl.BlockSpec(w_out.shape, const3),
                  pl.BlockSpec((1, S5_STATES), const2),
                  pl.BlockSpec((1, S5_STATES), const2),
                  pl.BlockSpec((1, B_WIDTH), const2),
                  state, state],
        out_specs=[pl.BlockSpec((nb, steps, B_WIDTH), lambda c: (0, c, 0)), state, state],
        out_shape=[jax.ShapeDtypeStruct((nb, t, B_WIDTH), BF16),
                   jax.ShapeDtypeStruct((V7X_SUBLANES, S5_STATES), F32),
                   jax.ShapeDtypeStruct((V7X_SUBLANES, S5_STATES), F32)],
        scratch_shapes=[pltpu.VMEM((blk, S5_STATES), F32),
                        pltpu.VMEM((blk, S5_STATES), F32),
                        pltpu.VMEM((blk, B_WIDTH), BF16)],
        compiler_params=_params("arbitrary"),
        name="s5_scan",
    )(proj3, perm, perm.T, w_in, w_out, a_re, a_im, d_skip.reshape(1, B_WIDTH),
      jnp.concatenate([pad, h0_re], axis=0), jnp.concatenate([pad, h0_im], axis=0))
    return ys, h_re[V7X_SUBLANES - nb:], h_im[V7X_SUBLANES - nb:]


def _cast_t_kernel(x_ref, o_ref):
    o_ref[...] = x_ref[...].T.astype(BF16)


def _transposed_bf16_blocks(x):
    r, c = x.shape
    tr, tc = PEER_TE, 2048
    return pl.pallas_call(
        _cast_t_kernel, grid=(r // tr, c // tc),
        in_specs=[pl.BlockSpec((tr, tc), lambda i, j: (i, j))],
        out_specs=pl.BlockSpec((None, tc, tr), lambda i, j: (i, j, 0)),
        out_shape=jax.ShapeDtypeStruct((r // tr, c, tr), BF16),
        compiler_params=_params("parallel", "parallel"), name="cast_transpose")(x)


_STAIR_LIMIT = [PEER_TOPK // (a + 1) for a in range(PEER_TOPK)]


def _extract_top(work, order, count, unranked=None):
    big = PEER_UNRANKED
    vals, picks = [], []
    rank = None if unranked is None else jnp.full(work.shape, unranked, F32)
    for r in range(count):
        m = jnp.max(work, axis=0, keepdims=True)
        pick = jnp.min(jnp.where(work == m, order, big), axis=0, keepdims=True)
        hit = order == pick
        work = jnp.where(hit, -jnp.inf, work)
        if rank is not None:
            rank = jnp.where(hit, float(r), rank)
        vals.append(m)
        picks.append(pick)
    return vals, picks, rank


def _extract_top_untied(work, count, unranked):
    vals = []
    rank = jnp.full(work.shape, unranked, F32)
    for r in range(count):
        m = jnp.max(work, axis=0, keepdims=True)
        hit = work == m
        work = jnp.where(hit, -jnp.inf, work)
        rank = jnp.where(hit, float(r), rank)
        vals.append(m)
    removed = jnp.sum(jnp.where(work == -jnp.inf, 1.0, 0.0))
    return vals, rank, removed


def _stack_rows(rows, row_idx):
    out = jnp.broadcast_to(rows[0], row_idx.shape)
    for r in range(1, len(rows)):
        out = jnp.where(row_idx == r, rows[r], out)
    return out


def _topk_kernel(q_ref, keys_ref, e1n_ref, lim_ref, e2_ref, r2_ref, sc_ref, top_ref, rank_ref, *, tm):
    key_idx = lax.broadcasted_iota(jnp.int32, (PEER_NKEYS, tm), 0).astype(F32)
    iota16 = lax.broadcasted_iota(jnp.int32, (PEER_TOPK, tm), 0).astype(F32)
    iota8 = iota16[:8]
    flat = [iota16] + [a * PEER_TOPK + iota8 for a in range(1, 8)] + [(iota8 + 8) * PEER_TOPK]
    flat = jnp.concatenate(flat, axis=0)
    unranked = (-1.0, PEER_UNRANKED)
    halves = [(h, s) for h in range(PEER_HEADS) for s in range(2)]
    removed = 0.0
    for h, s in halves:
        col = (2 * h + s) * PEER_SUBDIM
        qs = q_ref[:, col:col + PEER_SUBDIM].astype(BF16)
        ks = keys_ref[h, s].astype(BF16)
        sc = lax.dot_general(ks, qs, (((1,), (1,)), ((), ())), preferred_element_type=F32)
        vals, rank, gone = _extract_top_untied(sc, PEER_TOPK, unranked[s])
        sc_ref[h, s] = sc
        top_ref[h, s] = _stack_rows(vals, iota16)
        rank_ref[h, s] = rank
        removed = removed + gone

    @pl.when(removed != float(len(halves) * PEER_TOPK * tm))
    def _():
        for h, s in halves:
            vals, _, rank = _extract_top(sc_ref[h, s], key_idx, PEER_TOPK, unranked=unranked[s])
            top_ref[h, s] = _stack_rows(vals, iota16)
            rank_ref[h, s] = rank

    for h in range(PEER_HEADS):
        scores = [sc_ref[h, 0], sc_ref[h, 1]]
        ranks = [rank_ref[h, 0], rank_ref[h, 1]]
        top1, v2_16 = top_ref[h, 0], top_ref[h, 1]
        v1 = [top1[r:r + 1, :] for r in range(PEER_TOPK)]
        v2 = [v2_16[0:1, :]]
        v2_8 = v2_16[:8]
        cand = [v1[0] + v2_16]
        for a in range(1, 8):
            cand.append(jnp.where(iota8 < _STAIR_LIMIT[a], v1[a] + v2_8, -jnp.inf))
        cand.append(top1[8:] + v2[0])
        cand = jnp.concatenate(cand, axis=0)
        cv, picks, _ = _extract_top(cand, flat, PEER_TOPK)
        z = jnp.ones_like(cv[0])
        for r in range(1, PEER_TOPK):
            z = z + jnp.exp(cv[r] - cv[0])
        limit = jnp.full((PEER_TOPK, tm), -1.0, F32)
        for pick in picks:
            row = jnp.floor(pick * (1.0 / PEER_TOPK))
            col = pick - PEER_TOPK * row
            limit = jnp.maximum(limit, jnp.where(iota16 == row, col, -1.0))
        lim = jnp.full((PEER_NKEYS, tm), -1.0, F32)
        for a in range(PEER_TOPK):
            lim = jnp.where(ranks[0] == float(a), limit[a:a + 1, :], lim)
        e1n_ref[h] = jnp.exp(scores[0] - v1[0]) / z
        lim_ref[h] = lim
        e2_ref[h] = jnp.exp(scores[1] - v2[0]).astype(BF16)
        r2_ref[h] = ranks[1].astype(BF16)


def _peer_topk(q, keys, tm):
    n = q.shape[0]
    per_key = pl.BlockSpec((PEER_HEADS, PEER_NKEYS, tm), lambda i: (0, 0, i))
    shape = (PEER_HEADS, PEER_NKEYS, n)
    return pl.pallas_call(
        functools.partial(_topk_kernel, tm=tm),
        grid=(n // tm,),
        in_specs=[pl.BlockSpec((tm, q.shape[1]), lambda i: (i, 0)),
                  pl.BlockSpec(keys.shape, lambda i: (0, 0, 0, 0))],
        out_specs=[per_key] * 4,
        out_shape=[jax.ShapeDtypeStruct(shape, F32), jax.ShapeDtypeStruct(shape, F32),
                   jax.ShapeDtypeStruct(shape, BF16), jax.ShapeDtypeStruct(shape, BF16)],
        scratch_shapes=[pltpu.VMEM((PEER_HEADS, 2, PEER_NKEYS, tm), F32),
                        pltpu.VMEM((PEER_HEADS, 2, PEER_TOPK, tm), F32),
                        pltpu.VMEM((PEER_HEADS, 2, PEER_NKEYS, tm), F32)],
        compiler_params=_params("parallel"),
        name="peer_topk",
    )(q, keys)


def _peer_kernel(ht_ref, u_ref, vt_ref, e1n_ref, lim_ref, e2_ref, r2_ref,
                 o_ref, act_a, act_b, p_a, p_b, row_bc, *, te, nblocks, total):
    j = pl.program_id(0)
    slabs = te // PEER_NKEYS

    tm = o_ref.shape[1]
    tile = (V7X_BF16_ROWS, tm)

    @pl.when(j == 0)
    def _():
        p_a[...] = jnp.zeros_like(p_a)
        act_b[...] = jnp.zeros_like(act_b)

    @pl.when(jnp.maximum(j - 2, 0) % nblocks == 0)
    def _():
        o_ref[...] = jnp.zeros_like(o_ref)

    def step(act_new, act_old, p_new, p_old):
        first = (jnp.clip(j - 1, 0, total - 1) % nblocks) * slabs
        d = ht_ref.shape[0]

        def gate(il, r):
            i1 = first + il
            per_head = []
            for h in range(PEER_HEADS):
                if r == 0:
                    rows_i1 = [jnp.broadcast_to(ref[h, pl.ds(i1, 1), :], tile).astype(BF16)
                               for ref in (e1n_ref, lim_ref)]
                    for a, val in enumerate(rows_i1):
                        row_bc[a, il, h] = val
                else:
                    rows_i1 = [row_bc[a, il, h] for a in range(2)]
                per_head.append(rows_i1)
            for q in range(PEER_GATE_ROWS // V7X_BF16_ROWS):
                k0 = r * PEER_GATE_ROWS + q * V7X_BF16_ROWS
                keys = slice(k0, k0 + V7X_BF16_ROWS)
                w = jnp.zeros(tile, BF16)
                for h in range(PEER_HEADS):
                    e1, lim = per_head[h]
                    w = w + jnp.where(r2_ref[h, keys, :] <= lim, e2_ref[h, keys, :] * e1, jnp.zeros((), BF16))
                rows = slice(il * PEER_NKEYS + k0, il * PEER_NKEYS + k0 + V7X_BF16_ROWS)
                p_new[rows, :] = w * jax.nn.gelu(act_old[rows, :]).astype(BF16)

        def pre_act(kc):
            ks = slice(kc * PEER_K_CHUNK, (kc + 1) * PEER_K_CHUNK)
            part = jnp.dot(u_ref[:, ks].astype(BF16), ht_ref[ks, :], preferred_element_type=F32)
            if kc == 0:
                act_new[...] = part
            else:
                act_new[...] += part

        def accumulate(mc):
            ms = slice(mc * PEER_M_CHUNK, (mc + 1) * PEER_M_CHUNK)
            o_ref[ms, :] += jnp.dot(vt_ref[ms, :], p_old[...], preferred_element_type=F32)

        gates = [(il, r) for il in range(slabs) for r in range(PEER_NKEYS // PEER_GATE_ROWS)]
        n_k, n_m = d // PEER_K_CHUNK, d // PEER_M_CHUNK
        mxu = []
        for kc in range(n_k):
            mxu.append((functools.partial(pre_act, kc), PEER_K_CHUNK * te))
            mxu += [(functools.partial(accumulate, kc * (n_m // n_k) + m), PEER_M_CHUNK * te)
                    for m in range(n_m // n_k)]
        mxu_total = sum(cost for _, cost in mxu)
        issued = done = 0
        for piece, cost in mxu:
            piece()
            issued += cost
            upto = (len(gates) * issued) // mxu_total
            for il, r in gates[done:upto]:
                gate(il, r)
            done = upto

    @pl.when(j % 2 == 0)
    def _():
        step(act_a, act_b, p_b, p_a)

    @pl.when(j % 2 == 1)
    def _():
        step(act_b, act_a, p_a, p_b)


def _peer_dense(ht, expert_u, vt_bf, topk, tm):
    d, n = ht.shape
    nblocks, _, te = vt_bf.shape
    total = (n // tm) * nblocks
    item = lambda f, lag: jnp.clip(f - lag, 0, total - 1)
    per_tile = dict(pipeline_mode=pl.Buffered(1))
    per_key = pl.BlockSpec((PEER_HEADS, PEER_NKEYS, tm), lambda f: (0, 0, item(f, 1) // nblocks), **per_tile)
    return pl.pallas_call(
        functools.partial(_peer_kernel, te=te, nblocks=nblocks, total=total),
        grid=(total + 2,),
        in_specs=[pl.BlockSpec((d, tm), lambda f: (0, item(f, 0) // nblocks), **per_tile),
                  pl.BlockSpec((te, d), lambda f: (item(f, 0) % nblocks, 0)),
                  pl.BlockSpec((None, d, te), lambda f: (item(f, 2) % nblocks, 0, 0))]
                 + [per_key] * 4,
        out_specs=pl.BlockSpec((d, tm), lambda f: (0, item(f, 2) // nblocks)),
        out_shape=jax.ShapeDtypeStruct((d, n), F32),
        scratch_shapes=[pltpu.VMEM((te, tm), F32), pltpu.VMEM((te, tm), F32),
                        pltpu.VMEM((te, tm), BF16), pltpu.VMEM((te, tm), BF16),
                        pltpu.VMEM((2, te // PEER_NKEYS, PEER_HEADS, V7X_BF16_ROWS, tm), BF16)],
        compiler_params=_params("arbitrary"),
        name="peer_dense",
    )(ht, expert_u, vt_bf, *topk)


def _split_mod(mod, tokens_per_stream, per_token):
    if per_token:
        mod = jnp.repeat(mod, tokens_per_stream, axis=0)[None]
    else:
        mod = mod[:, None, :]
    return jnp.split(mod, 6, axis=-1)


def _encoder_layer(x, mod, h0_re, h0_im, lp, peer_w, *, tr, scan_steps, mixer_rows, per_token_mod, emit_v):
    (g_pre_tok, g_post_tok, g_pre_ch, g_post_ch, w_in, a_ws, a_bs, a_ln_g, a_ln_b, disc, s5_d,
     w_glu, b_glu, w_branch, w_out, peer_w_query, peer_sub_keys) = lp
    peer_u, vt_bf = peer_w
    nb, t, d = x.shape
    n = nb * t
    tiles_per_group = 1 if per_token_mod else t // tr
    sh_t, sc_t, gt_t, sh_c, sc_c, gt_c = _split_mod(mod, t, per_token_mod)
    x2 = x.reshape(n, d)

    h = _prenorm(x2, g_pre_tok, sc_t, sh_t, tr, tiles_per_group)
    proj = _matmul(h, w_in, 1024, 1024, name="in_proj")
    proj3 = proj.reshape(nb, t, IN_WIDTH)
    mix = _mixer_a(proj3, a_ws, a_bs, a_ln_g, a_ln_b, mixer_rows, emit_v)
    ya = mix[0].reshape(n, A_WIDTH)
    v_rows = mix[1] if emit_v else None
    ys, h_re, h_im = _s5(proj3, disc, s5_d, h0_re, h0_im, scan_steps)
    yb = _glu(ys.reshape(n, B_WIDTH), w_glu, b_glu, 1024, 512)
    merged = _branch_merge(ya, yb, w_branch, proj, 1024, 512)
    m = _matmul(merged, w_out, 1024, 1024, name="out_proj")

    x1, h2, h2t = _resid_prenorm(x2, m, g_post_tok, gt_t, g_pre_ch, sc_c, sh_c, tr, tiles_per_group)
    q = _matmul(h2, peer_w_query, 1024, 1024, name="peer_query")
    topk = _peer_topk(q, peer_sub_keys, min(128, n))
    ft = _peer_dense(h2t, peer_u, vt_bf, topk, min(512, n))
    y = _final_resid(x1, ft, g_post_ch, gt_c, tr, tiles_per_group)
    return (y.reshape(nb, t, d), h_re.reshape(nb, B_GROUPS, B_STATE), h_im.reshape(nb, B_GROUPS, B_STATE), v_rows)


def kernel(x_prompt, x_sample, c_prompt, c_sample, state_ssm_re, state_ssm_im, w_ada, b_ada, g_pre_tok,
           g_post_tok, g_pre_ch, g_post_ch, w_in, a_ws, a_bs, a_ln_g, a_ln_b, s5_lam_re, s5_lam_im,
           s5_log_dt, s5_b_re, s5_b_im, s5_c_re, s5_c_im, s5_d, w_glu, b_glu, w_branch, w_out,
           peer_w_query, peer_sub_keys, peer_u, peer_v):
    depth = w_ada.shape[0]
    nb_p, t_p, _ = x_prompt.shape
    nb_s, t_s, _ = x_sample.shape
    y_p, y_s = x_prompt, x_sample
    re_p, im_p, re_s, im_s, v_s = [], [], [], [], []
    pad = (-(nb_p + nb_s)) % 8
    for l in range(depth):
        c_all = jnp.concatenate([c_prompt, c_sample, jnp.zeros((pad, D_MODEL), F32)], axis=0)
        mod = _ada(c_all, w_ada[l], b_ada[l])
        disc = _s5_discretize(s5_lam_re[l], s5_lam_im[l], s5_log_dt[l], s5_b_re[l], s5_b_im[l],
                              s5_c_re[l], s5_c_im[l])
        lp = (g_pre_tok[l], g_post_tok[l], g_pre_ch[l], g_post_ch[l], w_in[l], a_ws[l], a_bs[l],
              a_ln_g[l], a_ln_b[l], disc, s5_d[l], w_glu[l], b_glu[l], w_branch[l], w_out[l],
              peer_w_query[l], peer_sub_keys[l])
        peer_w = (peer_u[l], _transposed_bf16_blocks(peer_v[l]))
        zeros = jnp.zeros((nb_p, S5_STATES), F32)
        y_p, hr, hi, _ = _encoder_layer(
            y_p, mod[:nb_p], zeros, zeros, lp, peer_w,
            tr=256, scan_steps=32, mixer_rows=A_CHUNK, per_token_mod=False, emit_v=False)
        re_p.append(hr)
        im_p.append(hi)
        y_s, hr, hi, v_rows = _encoder_layer(
            y_s, mod[nb_p:nb_p + nb_s], state_ssm_re[l].reshape(nb_s, S5_STATES),
            state_ssm_im[l].reshape(nb_s, S5_STATES), lp, peer_w,
            tr=nb_s * t_s, scan_steps=t_s, mixer_rows=t_s, per_token_mod=True, emit_v=True)
        re_s.append(hr)
        im_s.append(hi)
        v_s.append(v_rows)
    return (y_p, y_s, jnp.stack(re_p), jnp.stack(im_p), jnp.stack(re_s), jnp.stack(im_s), jnp.stack(v_s))
```

```python
import functools

import jax
import jax.numpy as jnp
from jax import lax
from jax.experimental import pallas as pl
from jax.experimental.pallas import tpu as pltpu

F32 = jnp.float32
BF16 = jnp.bfloat16

D_MODEL = 4096
CHUNK = 64
A_CHUNK = 128
A_GROUPS = 16
A_GROUP_DIM = 128
A_WIDTH = A_GROUPS * A_GROUP_DIM
B_GROUP_DIM = 16
B_GROUPS = 128
B_WIDTH = B_GROUPS * B_GROUP_DIM
B_STATE = 64
S5_STATES = B_GROUPS * B_STATE
S5_GROUPS_PER_BLOCK = 8
S5_BLOCKS = B_GROUPS // S5_GROUPS_PER_BLOCK
S5_BLOCK_STATES = S5_GROUPS_PER_BLOCK * B_STATE
IN_WIDTH = 2 * A_WIDTH + B_WIDTH + 2 * D_MODEL
PEER_HEADS = 8
PEER_SUBDIM = 128
PEER_NKEYS = 128
PEER_EXPERTS = PEER_NKEYS * PEER_NKEYS
PEER_TOPK = 16
PEER_UNRANKED = 1e9
PEER_TE = 512
PEER_GATE_ROWS = 16
PEER_K_CHUNK = 256
PEER_M_CHUNK = 128
EPS = 1e-6

V7X_LANES = 128
V7X_SUBLANES = 8
V7X_BF16_ROWS = 16
V7X_VMEM_BYTES = 64 * 1024 * 1024
VMEM_LIMIT_BYTES = 56 * 1024 * 1024


def _params(*semantics, flags=None):
    return pltpu.CompilerParams(dimension_semantics=semantics, vmem_limit_bytes=VMEM_LIMIT_BYTES, flags=flags)


def _rms(x):
    return x * lax.rsqrt(jnp.mean(x * x, axis=-1, keepdims=True) + EPS)


def _ada_kernel(c_ref, w_ref, b_ref, o_ref):
    s = jax.nn.silu(c_ref[...]).astype(BF16)
    o_ref[...] = jnp.dot(s, w_ref[...].astype(BF16), preferred_element_type=F32) + b_ref[...]


def _ada(c, w, b):
    rows, d = c.shape
    n = w.shape[1]
    tn = 512
    return pl.pallas_call(
        _ada_kernel,
        grid=(n // tn,),
        in_specs=[pl.BlockSpec((rows, d), lambda j: (0, 0)),
                  pl.BlockSpec((d, tn), lambda j: (0, j)),
                  pl.BlockSpec((1, tn), lambda j: (0, j))],
        out_specs=pl.BlockSpec((rows, tn), lambda j: (0, j)),
        out_shape=jax.ShapeDtypeStruct((rows, n), F32),
        compiler_params=_params("parallel"),
        name="ada_mod",
    )(c, w, b.reshape(1, n))


def _mod_spec(mod, tiles_per_group):
    return pl.BlockSpec((None,) + mod.shape[1:], lambda i: (i // tiles_per_group, 0, 0))


def _prenorm_kernel(x_ref, g_ref, sc_ref, sh_ref, o_ref):
    h = _rms(x_ref[...]) * g_ref[...] * (1.0 + sc_ref[...]) + sh_ref[...]
    o_ref[...] = h.astype(BF16)


def _prenorm(x, g, sc, sh, tr, tiles_per_group):
    n, d = x.shape
    return pl.pallas_call(
        _prenorm_kernel,
        grid=(n // tr,),
        in_specs=[pl.BlockSpec((tr, d), lambda i: (i, 0)),
                  pl.BlockSpec((1, d), lambda i: (0, 0)),
                  _mod_spec(sc, tiles_per_group),
                  _mod_spec(sh, tiles_per_group)],
        out_specs=pl.BlockSpec((tr, d), lambda i: (i, 0)),
        out_shape=jax.ShapeDtypeStruct((n, d), BF16),
        compiler_params=_params("parallel"),
        name="prenorm_tok",
    )(x, g.reshape(1, d), sc, sh)


def _resid_kernel(x_ref, m_ref, gpost_ref, gt_ref, gpre_ref, sc_ref, sh_ref, x1_ref, h_ref, ht_ref):
    x1 = x_ref[...] + gt_ref[...] * (_rms(m_ref[...]) * gpost_ref[...])
    x1_ref[...] = x1
    h = _rms(x1) * gpre_ref[...] * (1.0 + sc_ref[...]) + sh_ref[...]
    h_ref[...] = h.astype(BF16)
    ht_ref[...] = h.T.astype(BF16)


def _resid_prenorm(x, m, gpost, gt, gpre, sc, sh, tr, tiles_per_group):
    n, d = x.shape
    row = pl.BlockSpec((tr, d), lambda i: (i, 0))
    vec = pl.BlockSpec((1, d), lambda i: (0, 0))
    return pl.pallas_call(
        _resid_kernel,
        grid=(n // tr,),
        in_specs=[row, row, vec, _mod_spec(gt, tiles_per_group), vec,
                  _mod_spec(sc, tiles_per_group), _mod_spec(sh, tiles_per_group)],
        out_specs=[row, row, pl.BlockSpec((d, tr), lambda i: (0, i))],
        out_shape=[jax.ShapeDtypeStruct((n, d), F32),
                   jax.ShapeDtypeStruct((n, d), BF16),
                   jax.ShapeDtypeStruct((d, n), BF16)],
        compiler_params=_params("parallel"),
        name="resid_prenorm_ch",
    )(x, m, gpost.reshape(1, d), gt, gpre.reshape(1, d), sc, sh)


def _final_kernel(x_ref, ft_ref, gpost_ref, gt_ref, o_ref):
    f = ft_ref[...].T
    o_ref[...] = x_ref[...] + gt_ref[...] * (_rms(f) * gpost_ref[...])


def _final_resid(x, ft, gpost, gt, tr, tiles_per_group):
    n, d = x.shape
    row = pl.BlockSpec((tr, d), lambda i: (i, 0))
    return pl.pallas_call(
        _final_kernel,
        grid=(n // tr,),
        in_specs=[row, pl.BlockSpec((d, tr), lambda i: (0, i)),
                  pl.BlockSpec((1, d), lambda i: (0, 0)), _mod_spec(gt, tiles_per_group)],
        out_specs=row,
        out_shape=jax.ShapeDtypeStruct((n, d), F32),
        compiler_params=_params("parallel"),
        name="final_resid",
    )(x, ft, gpost.reshape(1, d), gt)


def _mm_kernel(a_ref, w_ref, o_ref):
    o_ref[...] = jnp.dot(a_ref[...].astype(BF16), w_ref[...].astype(BF16),
                         preferred_element_type=F32).astype(o_ref.dtype)


def _matmul(a, w, tm, tn, out_dtype=F32, name="matmul"):
    m, k = a.shape
    n = w.shape[1]
    tm = min(tm, m)
    return pl.pallas_call(
        _mm_kernel,
        grid=(m // tm, n // tn),
        in_specs=[pl.BlockSpec((tm, k), lambda i, j: (i, 0)),
                  pl.BlockSpec((k, tn), lambda i, j: (0, j))],
        out_specs=pl.BlockSpec((tm, tn), lambda i, j: (i, j)),
        out_shape=jax.ShapeDtypeStruct((m, n), out_dtype),
        compiler_params=_params("parallel", "parallel"),
        name=name,
    )(a, w)


def _glu_kernel(y_ref, w_ref, b_ref, yj_ref, o_ref):
    acc = jnp.dot(y_ref[...].astype(BF16), w_ref[...].astype(BF16), preferred_element_type=F32)
    o_ref[...] = (yj_ref[...] * jax.nn.sigmoid(acc + b_ref[...])).astype(BF16)


def _glu(y, w, b, tm, tn):
    m, k = y.shape
    n = w.shape[1]
    tm = min(tm, m)
    return pl.pallas_call(
        _glu_kernel,
        grid=(m // tm, n // tn),
        in_specs=[pl.BlockSpec((tm, k), lambda i, j: (i, 0)),
                  pl.BlockSpec((k, tn), lambda i, j: (0, j)),
                  pl.BlockSpec((1, tn), lambda i, j: (0, j)),
                  pl.BlockSpec((tm, tn), lambda i, j: (i, j))],
        out_specs=pl.BlockSpec((tm, tn), lambda i, j: (i, j)),
        out_shape=jax.ShapeDtypeStruct((m, n), BF16),
        compiler_params=_params("parallel", "parallel"),
        name="s5_glu",
    )(y, w, b.reshape(1, n), y)


def _branch_kernel(ya_ref, yb_ref, wa_ref, wb_ref, ga_ref, gb_ref, o_ref):
    a = jnp.dot(ya_ref[...], wa_ref[...].astype(BF16), preferred_element_type=F32)
    b = jnp.dot(yb_ref[...], wb_ref[...].astype(BF16), preferred_element_type=F32)
    o_ref[...] = (jax.nn.sigmoid(ga_ref[...]) * a + jax.nn.sigmoid(gb_ref[...]) * b).astype(BF16)


def _branch_merge(ya, yb, w_branch, proj, tm, tn):
    m = ya.shape[0]
    n = w_branch.shape[1]
    tm = min(tm, m)
    ga_blk = (2 * A_WIDTH + B_WIDTH) // tn
    gb_blk = ga_blk + D_MODEL // tn
    return pl.pallas_call(
        _branch_kernel,
        grid=(n // tn, m // tm),
        in_specs=[pl.BlockSpec((tm, A_WIDTH), lambda j, i: (i, 0)),
                  pl.BlockSpec((tm, B_WIDTH), lambda j, i: (i, 0)),
                  pl.BlockSpec((A_WIDTH, tn), lambda j, i: (0, j)),
                  pl.BlockSpec((B_WIDTH, tn), lambda j, i: (A_WIDTH // B_WIDTH, j)),
                  pl.BlockSpec((tm, tn), lambda j, i: (i, ga_blk + j)),
                  pl.BlockSpec((tm, tn), lambda j, i: (i, gb_blk + j))],
        out_specs=pl.BlockSpec((tm, tn), lambda j, i: (i, j)),
        out_shape=jax.ShapeDtypeStruct((m, n), BF16),
        compiler_params=_params("parallel", "parallel"),
        name="branch_merge",
    )(ya, yb, w_branch, w_branch, proj, proj)


def _mixer_a_kernel(za_ref, ws_ref, bias_ref, lng_ref, lnb_ref, ya_ref, *v_refs, rows):
    z = jax.nn.gelu(za_ref[...])
    u = z[:, :A_WIDTH]
    v = z[:, A_WIDTH:]
    d = v - jnp.mean(v, axis=-1, keepdims=True)
    vn = d * lax.rsqrt(jnp.mean(d * d, axis=-1, keepdims=True) + EPS) * lng_ref[...] + lnb_ref[...]
    if v_refs:
        v_refs[0][...] = vn
    if rows < A_CHUNK:
        vn = jnp.concatenate([vn, jnp.zeros((A_CHUNK - rows, A_WIDTH), F32)], axis=0)
    vb = vn.astype(BF16)
    blk_i = lax.broadcasted_iota(jnp.int32, (A_CHUNK, A_CHUNK), 0) // CHUNK
    blk_j = lax.broadcasted_iota(jnp.int32, (A_CHUNK, A_CHUNK), 1) // CHUNK
    causal = blk_i >= blk_j
    for g in range(A_GROUPS):
        cols = slice(g * A_GROUP_DIM, (g + 1) * A_GROUP_DIM)
        w = jnp.where(causal, ws_ref[g], 0.0).astype(BF16)
        mixed = jnp.dot(w, vb[:, cols], preferred_element_type=F32) + bias_ref[:, cols]
        ya_ref[:, cols] = (u[:, cols] * mixed[:rows]).astype(BF16)


def _mixer_a(proj3, ws, bs, ln_g, ln_b, rows, emit_v):
    nb, t, _ = proj3.shape
    bias = jnp.repeat(bs.T, A_GROUP_DIM, axis=1)
    out_shape = [jax.ShapeDtypeStruct((nb, t, A_WIDTH), BF16)]
    out_specs = [pl.BlockSpec((None, rows, A_WIDTH), lambda b, c: (b, c, 0))]
    if emit_v:
        out_shape.append(jax.ShapeDtypeStruct((nb, t, A_WIDTH), F32))
        out_specs.append(pl.BlockSpec((None, rows, A_WIDTH), lambda b, c: (b, c, 0)))
    return pl.pallas_call(
        functools.partial(_mixer_a_kernel, rows=rows),
        grid=(nb, t // rows),
        in_specs=[pl.BlockSpec((None, rows, 2 * A_WIDTH), lambda b, c: (b, c, 0)),
                  pl.BlockSpec((A_GROUPS, A_CHUNK, A_CHUNK), lambda b, c: (0, 0, 0)),
                  pl.BlockSpec((A_CHUNK, A_WIDTH), lambda b, c: (0, 0)),
                  pl.BlockSpec((1, A_WIDTH), lambda b, c: (0, 0)),
                  pl.BlockSpec((1, A_WIDTH), lambda b, c: (0, 0))],
        out_specs=out_specs,
        out_shape=out_shape,
        compiler_params=_params("parallel", "parallel"),
        name="mixer_a",
    )(proj3, ws, bias, ln_g.reshape(1, A_WIDTH), ln_b.reshape(1, A_WIDTH))


def _s5_discretize(lam_re, lam_im, log_dt, b_re, b_im, c_re, c_im):
    dt = jnp.exp(log_dt.astype(F32))[:, None]
    lr, li = lam_re.astype(F32), lam_im.astype(F32)
    mag = jnp.exp(lr * dt)
    ab_re, ab_im = mag * jnp.cos(li * dt), mag * jnp.sin(li * dt)
    den = lr * lr + li * li
    num_re = ab_re - 1.0
    coef_re = (num_re * lr + ab_im * li) / den
    coef_im = (ab_im * lr - num_re * li) / den
    br, bi = b_re.astype(F32), b_im.astype(F32)
    bb_re = coef_re[..., None] * br - coef_im[..., None] * bi
    bb_im = coef_re[..., None] * bi + coef_im[..., None] * br
    eye = jnp.eye(S5_GROUPS_PER_BLOCK, dtype=F32)

    def pack_in(bb):
        t = bb.reshape(S5_BLOCKS, S5_GROUPS_PER_BLOCK, B_STATE, B_GROUP_DIM).transpose(0, 1, 3, 2)
        return jnp.einsum("kgcp,gm->kgcmp", t, eye).reshape(S5_BLOCKS, V7X_LANES, S5_BLOCK_STATES)

    def pack_out(cc):
        t = cc.reshape(S5_BLOCKS, S5_GROUPS_PER_BLOCK, B_GROUP_DIM, B_STATE).transpose(0, 1, 3, 2)
        return jnp.einsum("kgpc,gm->kgpmc", t, eye).reshape(S5_BLOCKS, S5_BLOCK_STATES, V7X_LANES)

    w_in = jnp.concatenate([pack_in(bb_re), pack_in(bb_im)], axis=-1).astype(BF16)
    w_out = jnp.concatenate([pack_out(c_re.astype(F32)), -pack_out(c_im.astype(F32))], axis=1).astype(BF16)
    return ab_re.reshape(1, S5_STATES), ab_im.reshape(1, S5_STATES), w_in, w_out


def _s5_kernel(xb_ref, perm_ref, unperm_ref, win_ref, wout_ref, are_ref, aim_ref, d_ref, h0r_ref, h0i_ref,
               ys_ref, hr_ref, hi_ref, sre, sim, g_tm, *, nb):
    @pl.when(pl.program_id(0) == 0)
    def _():
        hr_ref[...] = h0r_ref[...]
        hi_ref[...] = h0i_ref[...]

    x_sm = xb_ref[...].reshape(perm_ref.shape[0], B_WIDTH)
    x_hi = x_sm.astype(BF16)
    rest = x_sm - x_hi.astype(F32)
    x_mid = rest.astype(BF16)
    x_lo = (rest - x_mid.astype(F32)).astype(BF16)
    perm = perm_ref[...]
    x_top = jnp.dot(perm, x_hi, preferred_element_type=F32)
    x = (x_top + jnp.dot(perm, x_mid, preferred_element_type=F32)
         + jnp.dot(perm, x_lo, preferred_element_type=F32))
    xbf = x_top.astype(BF16)
    for k in range(S5_BLOCKS):
        drive = jnp.dot(xbf[:, k * V7X_LANES:(k + 1) * V7X_LANES], win_ref[k], preferred_element_type=F32)
        sre[:, k * S5_BLOCK_STATES:(k + 1) * S5_BLOCK_STATES] = drive[:, :S5_BLOCK_STATES]
        sim[:, k * S5_BLOCK_STATES:(k + 1) * S5_BLOCK_STATES] = drive[:, S5_BLOCK_STATES:]

    a_re = are_ref[...]
    a_im = aim_ref[...]
    sub = lax.broadcasted_iota(jnp.int32, (V7X_SUBLANES, S5_STATES), 0)
    for i in range(x.shape[0] // V7X_SUBLANES):
        rows = slice(i * V7X_SUBLANES, (i + 1) * V7X_SUBLANES)
        prev = slice((i - 1) * V7X_SUBLANES, i * V7X_SUBLANES)
        c_re = hr_ref[...] if i == 0 else sre[prev, :]
        c_im = hi_ref[...] if i == 0 else sim[prev, :]
        d_re = sre[rows, :]
        d_im = sim[rows, :]
        o_re = o_im = None
        for s in range(V7X_SUBLANES // nb):
            if nb < V7X_SUBLANES:
                c_re = pltpu.roll(c_re, nb, 0)
                c_im = pltpu.roll(c_im, nb, 0)
            n_re = a_re * c_re - a_im * c_im + d_re
            n_im = a_re * c_im + a_im * c_re + d_im
            o_re = n_re if s == 0 else jnp.where(sub >= s * nb, n_re, o_re)
            o_im = n_im if s == 0 else jnp.where(sub >= s * nb, n_im, o_im)
            c_re, c_im = n_re, n_im
        sre[rows, :] = o_re
        sim[rows, :] = o_im
    last = slice(x.shape[0] - V7X_SUBLANES, x.shape[0])
    hr_ref[...] = sre[last, :]
    hi_ref[...] = sim[last, :]

    for k in range(S5_BLOCKS):
        st = slice(k * S5_BLOCK_STATES, (k + 1) * S5_BLOCK_STATES)
        ch = slice(k * V7X_LANES, (k + 1) * V7X_LANES)
        y = (jnp.dot(sre[:, st].astype(BF16), wout_ref[k, :S5_BLOCK_STATES, :], preferred_element_type=F32)
             + jnp.dot(sim[:, st].astype(BF16), wout_ref[k, S5_BLOCK_STATES:, :], preferred_element_type=F32)
             + d_ref[:, ch] * x[:, ch])
        g_tm[:, ch] = jax.nn.gelu(y).astype(BF16)
    g_sm = jnp.dot(unperm_ref[...], g_tm[...], preferred_element_type=F32).astype(BF16)
    ys_ref[...] = g_sm.reshape(ys_ref.shape)


def _s5(proj3, disc, d_skip, h0_re, h0_im, steps):
    nb, t, _ = proj3.shape
    a_re, a_im, w_in, w_out = disc
    const2 = lambda c: (0, 0)
    const3 = lambda c: (0, 0, 0)
    blk = steps * nb
    src = jnp.arange(blk)
    perm = jax.nn.one_hot((src % nb) * steps + src // nb, blk, dtype=BF16)
    pad = jnp.zeros((V7X_SUBLANES - nb, S5_STATES), F32)
    state = pl.BlockSpec((V7X_SUBLANES, S5_STATES), const2)
    xb_blk = 2 * A_WIDTH // B_WIDTH
    ys, h_re, h_im = pl.pallas_call(
        functools.partial(_s5_kernel, nb=nb),
        grid=(t // steps,),
        in_specs=[pl.BlockSpec((nb, steps, B_WIDTH), lambda c: (0, c, xb_blk)),
                  pl.BlockSpec((blk, blk), const2),
                  pl.BlockSpec((blk, blk), const2),
                  pl.BlockSpec(w_in.shape, const3),
                  pl.BlockSpec(w_out.shape, const3),
                  pl.BlockSpec((1, S5_STATES), const2),
                  pl.BlockSpec((1, S5_STATES), const2),
                  pl.BlockSpec((1, B_WIDTH), const2),
                  state, state],
        out_specs=[pl.BlockSpec((nb, steps, B_WIDTH), lambda c: (0, c, 0)), state, state],
        out_shape=[jax.ShapeDtypeStruct((nb, t, B_WIDTH), BF16),
                   jax.ShapeDtypeStruct((V7X_SUBLANES, S5_STATES), F32),
                   jax.ShapeDtypeStruct((V7X_SUBLANES, S5_STATES), F32)],
        scratch_shapes=[pltpu.VMEM((blk, S5_STATES), F32),
                        pltpu.VMEM((blk, S5_STATES), F32),
                        pltpu.VMEM((blk, B_WIDTH), BF16)],
        compiler_params=_params("arbitrary"),
        name="s5_scan",
    )(proj3, perm, perm.T, w_in, w_out, a_re, a_im, d_skip.reshape(1, B_WIDTH),
      jnp.concatenate([pad, h0_re], axis=0), jnp.concatenate([pad, h0_im], axis=0))
    return ys, h_re[V7X_SUBLANES - nb:], h_im[V7X_SUBLANES - nb:]


def _cast_t_kernel(x_ref, o_ref):
    o_ref[...] = x_ref[...].T.astype(BF16)


def _transposed_bf16_blocks(x):
    r, c = x.shape
    tr, tc = PEER_TE, 2048
    return pl.pallas_call(
        _cast_t_kernel, grid=(r // tr, c // tc),
        in_specs=[pl.BlockSpec((tr, tc), lambda i, j: (i, j))],
        out_specs=pl.BlockSpec((None, tc, tr), lambda i, j: (i, j, 0)),
        out_shape=jax.ShapeDtypeStruct((r // tr, c, tr), BF16),
        compiler_params=_params("parallel", "parallel"), name="cast_transpose")(x)


_STAIR_LIMIT = [PEER_TOPK // (a + 1) for a in range(PEER_TOPK)]


def _extract_top(work, order, count, unranked=None):
    big = PEER_UNRANKED
    vals, picks = [], []
    rank = None if unranked is None else jnp.full(work.shape, unranked, F32)
    for r in range(count):
        m = jnp.max(work, axis=0, keepdims=True)
        pick = jnp.min(jnp.where(work == m, order, big), axis=0, keepdims=True)
        hit = order == pick
        work = jnp.where(hit, -jnp.inf, work)
        if rank is not None:
            rank = jnp.where(hit, float(r), rank)
        vals.append(m)
        picks.append(pick)
    return vals, picks, rank


def _extract_top_untied(work, count, unranked):
    vals = []
    rank = jnp.full(work.shape, unranked, F32)
    for r in range(count):
        m = jnp.max(work, axis=0, keepdims=True)
        hit = work == m
        work = jnp.where(hit, -jnp.inf, work)
        rank = jnp.where(hit, float(r), rank)
        vals.append(m)
    removed = jnp.sum(jnp.where(work == -jnp.inf, 1.0, 0.0))
    return vals, rank, removed


def _stack_rows(rows, row_idx):
    out = jnp.broadcast_to(rows[0], row_idx.shape)
    for r in range(1, len(rows)):
        out = jnp.where(row_idx == r, rows[r], out)
    return out


def _topk_kernel(q_ref, keys_ref, e1n_ref, lim_ref, e2_ref, r2_ref, sc_ref, top_ref, rank_ref, *, tm):
    key_idx = lax.broadcasted_iota(jnp.int32, (PEER_NKEYS, tm), 0).astype(F32)
    iota16 = lax.broadcasted_iota(jnp.int32, (PEER_TOPK, tm), 0).astype(F32)
    iota8 = iota16[:8]
    flat = [iota16] + [a * PEER_TOPK + iota8 for a in range(1, 8)] + [(iota8 + 8) * PEER_TOPK]
    flat = jnp.concatenate(flat, axis=0)
    unranked = (-1.0, PEER_UNRANKED)
    halves = [(h, s) for h in range(PEER_HEADS) for s in range(2)]
    removed = 0.0
    for h, s in halves:
        col = (2 * h + s) * PEER_SUBDIM
        qs = q_ref[:, col:col + PEER_SUBDIM].astype(BF16)
        ks = keys_ref[h, s].astype(BF16)
        sc = lax.dot_general(ks, qs, (((1,), (1,)), ((), ())), preferred_element_type=F32)
        vals, rank, gone = _extract_top_untied(sc, PEER_TOPK, unranked[s])
        sc_ref[h, s] = sc
        top_ref[h, s] = _stack_rows(vals, iota16)
        rank_ref[h, s] = rank
        removed = removed + gone

    @pl.when(removed != float(len(halves) * PEER_TOPK * tm))
    def _():
        for h, s in halves:
            vals, _, rank = _extract_top(sc_ref[h, s], key_idx, PEER_TOPK, unranked=unranked[s])
            top_ref[h, s] = _stack_rows(vals, iota16)
            rank_ref[h, s] = rank

    for h in range(PEER_HEADS):
        scores = [sc_ref[h, 0], sc_ref[h, 1]]
        ranks = [rank_ref[h, 0], rank_ref[h, 1]]
        top1, v2_16 = top_ref[h, 0], top_ref[h, 1]
        v1 = [top1[r:r + 1, :] for r in range(PEER_TOPK)]
        v2 = [v2_16[0:1, :]]
        v2_8 = v2_16[:8]
        cand = [v1[0] + v2_16]
        for a in range(1, 8):
            cand.append(jnp.where(iota8 < _STAIR_LIMIT[a], v1[a] + v2_8, -jnp.inf))
        cand.append(top1[8:] + v2[0])
        cand = jnp.concatenate(cand, axis=0)
        cv, picks, _ = _extract_top(cand, flat, PEER_TOPK)
        z = jnp.ones_like(cv[0])
        for r in range(1, PEER_TOPK):
            z = z + jnp.exp(cv[r] - cv[0])
        limit = jnp.full((PEER_TOPK, tm), -1.0, F32)
        for pick in picks:
            row = jnp.floor(pick * (1.0 / PEER_TOPK))
            col = pick - PEER_TOPK * row
            limit = jnp.maximum(limit, jnp.where(iota16 == row, col, -1.0))
        lim = jnp.full((PEER_NKEYS, tm), -1.0, F32)
        for a in range(PEER_TOPK):
            lim = jnp.where(ranks[0] == float(a), limit[a:a + 1, :], lim)
        e1n_ref[h] = jnp.exp(scores[0] - v1[0]) / z
        lim_ref[h] = lim
        e2_ref[h] = jnp.exp(scores[1] - v2[0]).astype(BF16)
        r2_ref[h] = ranks[1].astype(BF16)


def _peer_topk(q, keys, tm):
    n = q.shape[0]
    per_key = pl.BlockSpec((PEER_HEADS, PEER_NKEYS, tm), lambda i: (0, 0, i))
    shape = (PEER_HEADS, PEER_NKEYS, n)
    return pl.pallas_call(
        functools.partial(_topk_kernel, tm=tm),
        grid=(n // tm,),
        in_specs=[pl.BlockSpec((tm, q.shape[1]), lambda i: (i, 0)),
                  pl.BlockSpec(keys.shape, lambda i: (0, 0, 0, 0))],
        out_specs=[per_key] * 4,
        out_shape=[jax.ShapeDtypeStruct(shape, F32), jax.ShapeDtypeStruct(shape, F32),
                   jax.ShapeDtypeStruct(shape, BF16), jax.ShapeDtypeStruct(shape, BF16)],
        scratch_shapes=[pltpu.VMEM((PEER_HEADS, 2, PEER_NKEYS, tm), F32),
                        pltpu.VMEM((PEER_HEADS, 2, PEER_TOPK, tm), F32),
                        pltpu.VMEM((PEER_HEADS, 2, PEER_NKEYS, tm), F32)],
        compiler_params=_params("parallel"),
        name="peer_topk",
    )(q, keys)


def _peer_kernel(ht_ref, u_ref, vt_ref, e1n_ref, lim_ref, e2_ref, r2_ref,
                 o_ref, act_a, p_a, p_b, row_bc, *, te, nblocks, total):
    j = pl.program_id(0)
    slabs = te // PEER_NKEYS

    tm = o_ref.shape[1]
    tile = (V7X_BF16_ROWS, tm)

    @pl.when(j == 0)
    def _():
        p_a[...] = jnp.zeros_like(p_a)

    @pl.when(jnp.maximum(j - 1, 0) % nblocks == 0)
    def _():
        o_ref[...] = jnp.zeros_like(o_ref)

    def step(act_new, act_old, p_new, p_old):
        first = (jnp.minimum(j, total - 1) % nblocks) * slabs
        d = ht_ref.shape[0]

        def gate(il, r):
            i1 = first + il
            per_head = []
            for h in range(PEER_HEADS):
                if r == 0:
                    rows_i1 = [jnp.broadcast_to(ref[h, pl.ds(i1, 1), :], tile).astype(BF16)
                               for ref in (e1n_ref, lim_ref)]
                    for a, val in enumerate(rows_i1):
                        row_bc[a, il, h] = val
                else:
                    rows_i1 = [row_bc[a, il, h] for a in range(2)]
                per_head.append(rows_i1)
            for q in range(PEER_GATE_ROWS // V7X_BF16_ROWS):
                k0 = r * PEER_GATE_ROWS + q * V7X_BF16_ROWS
                keys = slice(k0, k0 + V7X_BF16_ROWS)
                w = jnp.zeros(tile, BF16)
                for h in range(PEER_HEADS):
                    e1, lim = per_head[h]
                    w = w + jnp.where(r2_ref[h, keys, :] <= lim, e2_ref[h, keys, :] * e1, jnp.zeros((), BF16))
                rows = slice(il * PEER_NKEYS + k0, il * PEER_NKEYS + k0 + V7X_BF16_ROWS)
                p_new[rows, :] = w * jax.nn.gelu(act_old[rows, :]).astype(BF16)

        def pre_act(kc):
            ks = slice(kc * PEER_K_CHUNK, (kc + 1) * PEER_K_CHUNK)
            part = jnp.dot(u_ref[:, ks].astype(BF16), ht_ref[ks, :], preferred_element_type=F32)
            if kc == 0:
                act_new[...] = part
            else:
                act_new[...] += part

        def accumulate(mc):
            ms = slice(mc * PEER_M_CHUNK, (mc + 1) * PEER_M_CHUNK)
            o_ref[ms, :] += jnp.dot(vt_ref[ms, :], p_old[...], preferred_element_type=F32)

        gates = [(il, r) for il in range(slabs) for r in range(PEER_NKEYS // PEER_GATE_ROWS)]
        n_k, n_m = d // PEER_K_CHUNK, d // PEER_M_CHUNK
        for kc in range(n_k):
            pre_act(kc)
        mxu = [(functools.partial(accumulate, m), PEER_M_CHUNK * te) for m in range(n_m)]
        mxu_total = sum(cost for _, cost in mxu)
        issued = done = 0
        for piece, cost in mxu:
            piece()
            issued += cost
            upto = (len(gates) * issued) // mxu_total
            for il, r in gates[done:upto]:
                gate(il, r)
            done = upto

    @pl.when(j % 2 == 0)
    def _():
        step(act_a, act_a, p_b, p_a)

    @pl.when(j % 2 == 1)
    def _():
        step(act_a, act_a, p_a, p_b)


def _peer_dense(ht, expert_u, vt_bf, topk, tm):
    d, n = ht.shape
    nblocks, _, te = vt_bf.shape
    total = (n // tm) * nblocks
    item = lambda f, lag: jnp.clip(f - lag, 0, total - 1)
    per_tile = dict(pipeline_mode=pl.Buffered(1))
    per_key = pl.BlockSpec((PEER_HEADS, PEER_NKEYS, tm), lambda f: (0, 0, item(f, 0) // nblocks), **per_tile)
    return pl.pallas_call(
        functools.partial(_peer_kernel, te=te, nblocks=nblocks, total=total),
        grid=(total + 1,),
        in_specs=[pl.BlockSpec((d, tm), lambda f: (0, item(f, 0) // nblocks), **per_tile),
                  pl.BlockSpec((te, d), lambda f: (item(f, 0) % nblocks, 0)),
                  pl.BlockSpec((None, d, te), lambda f: (item(f, 1) % nblocks, 0, 0))]
                 + [per_key] * 4,
        out_specs=pl.BlockSpec((d, tm), lambda f: (0, item(f, 1) // nblocks)),
        out_shape=jax.ShapeDtypeStruct((d, n), F32),
        scratch_shapes=[pltpu.VMEM((te, tm), F32),
                        pltpu.VMEM((te, tm), BF16), pltpu.VMEM((te, tm), BF16),
                        pltpu.VMEM((2, te // PEER_NKEYS, PEER_HEADS, V7X_BF16_ROWS, tm), BF16)],
        compiler_params=_params("arbitrary"),
        name="peer_dense",
    )(ht, expert_u, vt_bf, *topk)


def _split_mod(mod, tokens_per_stream, per_token):
    if per_token:
        mod = jnp.repeat(mod, tokens_per_stream, axis=0)[None]
    else:
        mod = mod[:, None, :]
    return jnp.split(mod, 6, axis=-1)


def _encoder_layer(x, mod, h0_re, h0_im, lp, peer_w, *, tr, scan_steps, mixer_rows, per_token_mod, emit_v):
    (g_pre_tok, g_post_tok, g_pre_ch, g_post_ch, w_in, a_ws, a_bs, a_ln_g, a_ln_b, disc, s5_d,
     w_glu, b_glu, w_branch, w_out, peer_w_query, peer_sub_keys) = lp
    peer_u, vt_bf = peer_w
    nb, t, d = x.shape
    n = nb * t
    tiles_per_group = 1 if per_token_mod else t // tr
    sh_t, sc_t, gt_t, sh_c, sc_c, gt_c = _split_mod(mod, t, per_token_mod)
    x2 = x.reshape(n, d)

    h = _prenorm(x2, g_pre_tok, sc_t, sh_t, tr, tiles_per_group)
    proj = _matmul(h, w_in, 1024, 512, name="in_proj")
    proj3 = proj.reshape(nb, t, IN_WIDTH)
    mix = _mixer_a(proj3, a_ws, a_bs, a_ln_g, a_ln_b, mixer_rows, emit_v)
    ya = mix[0].reshape(n, A_WIDTH)
    v_rows = mix[1] if emit_v else None
    ys, h_re, h_im = _s5(proj3, disc, s5_d, h0_re, h0_im, scan_steps)
    yb = _glu(ys.reshape(n, B_WIDTH), w_glu, b_glu, 1024, 512)
    merged = _branch_merge(ya, yb, w_branch, proj, 1024, 512)
    m = _matmul(merged, w_out, 1024, 512, name="out_proj")

    x1, h2, h2t = _resid_prenorm(x2, m, g_post_tok, gt_t, g_pre_ch, sc_c, sh_c, tr, tiles_per_group)
    q = _matmul(h2, peer_w_query, 1024, 512, name="peer_query")
    topk = _peer_topk(q, peer_sub_keys, min(128, n))
    ft = _peer_dense(h2t, peer_u, vt_bf, topk, min(512, n))
    y = _final_resid(x1, ft, g_post_ch, gt_c, tr, tiles_per_group)
    return (y.reshape(nb, t, d), h_re.reshape(nb, B_GROUPS, B_STATE), h_im.reshape(nb, B_GROUPS, B_STATE), v_rows)


def kernel(x_prompt, x_sample, c_prompt, c_sample, state_ssm_re, state_ssm_im, w_ada, b_ada, g_pre_tok,
           g_post_tok, g_pre_ch, g_post_ch, w_in, a_ws, a_bs, a_ln_g, a_ln_b, s5_lam_re, s5_lam_im,
           s5_log_dt, s5_b_re, s5_b_im, s5_c_re, s5_c_im, s5_d, w_glu, b_glu, w_branch, w_out,
           peer_w_query, peer_sub_keys, peer_u, peer_v):
    depth = w_ada.shape[0]
    nb_p, t_p, _ = x_prompt.shape
    nb_s, t_s, _ = x_sample.shape
    y_p, y_s = x_prompt, x_sample
    re_p, im_p, re_s, im_s, v_s = [], [], [], [], []
    pad = (-(nb_p + nb_s)) % 8
    for l in range(depth):
        c_all = jnp.concatenate([c_prompt, c_sample, jnp.zeros((pad, D_MODEL), F32)], axis=0)
        mod = _ada(c_all, w_ada[l], b_ada[l])
        disc = _s5_discretize(s5_lam_re[l], s5_lam_im[l], s5_log_dt[l], s5_b_re[l], s5_b_im[l],
                              s5_c_re[l], s5_c_im[l])
        lp = (g_pre_tok[l], g_post_tok[l], g_pre_ch[l], g_post_ch[l], w_in[l], a_ws[l], a_bs[l],
              a_ln_g[l], a_ln_b[l], disc, s5_d[l], w_glu[l], b_glu[l], w_branch[l], w_out[l],
              peer_w_query[l], peer_sub_keys[l])
        peer_w = (peer_u[l], _transposed_bf16_blocks(peer_v[l]))
        zeros = jnp.zeros((nb_p, S5_STATES), F32)
        y_p, hr, hi, _ = _encoder_layer(
            y_p, mod[:nb_p], zeros, zeros, lp, peer_w,
            tr=256, scan_steps=32, mixer_rows=A_CHUNK, per_token_mod=False, emit_v=False)
        re_p.append(hr)
        im_p.append(hi)
        y_s, hr, hi, v_rows = _encoder_layer(
            y_s, mod[nb_p:nb_p + nb_s], state_ssm_re[l].reshape(nb_s, S5_STATES),
            state_ssm_im[l].reshape(nb_s, S5_STATES), lp, peer_w,
            tr=nb_s * t_s, scan_steps=t_s, mixer_rows=t_s, per_token_mod=True, emit_v=True)
        re_s.append(hr)
        im_s.append(hi)
        v_s.append(v_rows)
    return (y_p, y_s, jnp.stack(re_p), jnp.stack(im_p), jnp.stack(re_s), jnp.stack(im_s), jnp.stack(v_s))
```
